```python
import math
import jax, jax.numpy as jnp
from jax import lax
import numpy as np

D_MODEL = 1024
BATCH = 8
SEQ = 4096
DEPTH = 4

CHUNK = 64
Q_BLOCK = 2 * CHUNK

ATT_HEAD_DIM = 64
ATT_WIDTH = D_MODEL // 2
ATT_HEADS = ATT_WIDTH // ATT_HEAD_DIM
POOL_WINDOWS = (2, 4, 8, 16)
POOL_GROUPS = 4
POOL_WIDTH = D_MODEL // 4
POOL_GROUP_DIM = POOL_WIDTH // POOL_GROUPS
SSM_WIDTH = D_MODEL // 4
SSM_GROUP_DIM = 16
SSM_GROUPS = SSM_WIDTH // SSM_GROUP_DIM
SSM_STATE = 64
DT_MIN = 1e-3
DT_MAX = 1e-1
N_BRANCH = 3
N_EXPERTS = 32
TOP_K = 4
D_FF = D_MODEL
SWIGLU_LIMIT = 7.0
SWIGLU_ALPHA = 1.702
MOE_BLOCK = 256
LN_EPS = 1e-5
DEEPNORM_ALPHA = (2.0 * DEPTH) ** 0.25
DEEPNORM_BETA = (8.0 * DEPTH) ** -0.25

IN_SPLIT_SIZES = (ATT_WIDTH, ATT_WIDTH, ATT_WIDTH, ATT_HEADS, POOL_WIDTH, SSM_WIDTH, N_BRANCH * D_MODEL)
N_IN = 3 * ATT_WIDTH + ATT_HEADS + POOL_WIDTH + SSM_WIDTH + N_BRANCH * D_MODEL

kernel_name = "hybrid_fox_pool_s5_moe_deepnorm"


def _split_points():
    pts, acc = [], 0
    for s in IN_SPLIT_SIZES[:-1]:
        acc += s
        pts.append(acc)
    return pts


def _layer_norm(x, g, b):
    xf = x.astype(jnp.float32)
    mu = xf.mean(-1, keepdims=True)
    var = jnp.square(xf - mu).mean(-1, keepdims=True)
    y = (xf - mu) * lax.rsqrt(var + LN_EPS) * g.astype(jnp.float32) + b.astype(jnp.float32)
    return y.astype(x.dtype)


def _forgetting_attention(q, k, v, f_logit, b_f):
    B_, S_, H, Dh = q.shape
    log_f = jax.nn.log_sigmoid(f_logit.astype(jnp.float32) + b_f.astype(jnp.float32))
    cum = jnp.cumsum(log_f, axis=1)
    cum_k = cum.transpose(0, 2, 1)
    nb = S_ // Q_BLOCK
    scale = Dh ** -0.5
    kf = k.astype(jnp.float32)
    qb = q.reshape(B_, nb, Q_BLOCK, H, Dh).transpose(1, 0, 2, 3, 4)
    cb = cum.reshape(B_, nb, Q_BLOCK, H).transpose(1, 0, 3, 2)
    key_pos = jnp.arange(S_)

    def block(args):
        i, q_i, c_i = args
        s = jnp.einsum('bqhd,bkhd->bhqk', q_i.astype(jnp.float32), kf) * scale
        s = s + (c_i[..., :, None] - cum_k[..., None, :])
        q_pos = i * Q_BLOCK + jnp.arange(Q_BLOCK)
        s = jnp.where(key_pos[None, :] <= q_pos[:, None], s, -jnp.inf)
        p = jax.nn.softmax(s, axis=-1)
        return jnp.einsum('bhqk,bkhd->bqhd', p.astype(v.dtype), v)

    out = lax.map(block, (jnp.arange(nb), qb, cb))
    return out.transpose(1, 0, 2, 3, 4).reshape(B_, S_, H * Dh)


def _multiscale_pool(u, w_pool, pool_scale):
    B_, S_, _ = u.shape
    ug = u.reshape(B_, S_, POOL_GROUPS, POOL_GROUP_DIM).astype(jnp.float32)
    cs = jnp.cumsum(ug, axis=1)
    cs0 = jnp.pad(cs, ((0, 0), (1, 0), (0, 0), (0, 0)))
    win = jnp.array(POOL_WINDOWS, jnp.int32)
    t = jnp.arange(S_, dtype=jnp.int32)[:, None]
    start = jnp.maximum(t + 1 - win[None, :], 0)
    g_idx = jnp.arange(POOL_GROUPS)[None, :]
    lower = cs0[:, start, g_idx]
    count = jnp.minimum(t + 1, win[None, :]).astype(jnp.float32)
    pooled = (cs - lower) / count[None, :, :, None]
    mixed = (pooled - ug).astype(u.dtype)
    y = jnp.einsum('bsgc,gcd->bsgd', mixed, w_pool).reshape(B_, S_, POOL_WIDTH)
    return y * pool_scale


def _s5(u, lam_re, lam_im, log_dt, b_re, b_im, c_re, c_im, d_skip, w_glu, b_glu):
    f32 = jnp.float32
    B_, S_, _ = u.shape
    uf = u.reshape(B_, S_, SSM_GROUPS, SSM_GROUP_DIM).astype(f32)
    dt = jnp.exp(log_dt.astype(f32))[:, None]
    lr = lam_re.astype(f32)
    li = lam_im.astype(f32)
    mag = jnp.exp(lr * dt)
    ar = mag * jnp.cos(li * dt)
    ai = mag * jnp.sin(li * dt)
    den = lr * lr + li * li
    nr = ar - 1.0
    zr = (nr * lr + ai * li) / den
    zi = (ai * lr - nr * li) / den
    br = b_re.astype(f32)
    bi = b_im.astype(f32)
    bbr = zr[..., None] * br - zi[..., None] * bi
    bbi = zr[..., None] * bi + zi[..., None] * br
    xr = jnp.einsum('bsgh,gph->bsgp', uf, bbr)
    xi = jnp.einsum('bsgh,gph->bsgp', uf, bbi)
    a_r = jnp.broadcast_to(ar[None, None], (1, S_) + ar.shape)
    a_i = jnp.broadcast_to(ai[None, None], (1, S_) + ai.shape)

    def combine(e1, e2):
        a1r, a1i, b1r, b1i = e1
        a2r, a2i, b2r, b2i = e2
        return (a2r * a1r - a2i * a1i,
                a2r * a1i + a2i * a1r,
                a2r * b1r - a2i * b1i + b2r,
                a2r * b1i + a2i * b1r + b2i)

    _, _, hr, hi = lax.associative_scan(combine, (a_r, a_i, xr, xi), axis=1)
    y = (jnp.einsum('bsgp,ghp->bsgh', hr, c_re.astype(f32))
         - jnp.einsum('bsgp,ghp->bsgh', hi, c_im.astype(f32))
         + d_skip.astype(f32) * uf)
    y = jax.nn.gelu(y.reshape(B_, S_, SSM_WIDTH).astype(u.dtype))
    return y * jax.nn.sigmoid(y @ w_glu + b_glu)


def _moe(h, w_router, b_router, w_up, b_up, w_down, b_down):
    f32 = jnp.float32
    B_, S_, D = h.shape
    xt = h.reshape(-1, D)
    N = xt.shape[0]
    logits = (xt @ w_router).astype(f32) + b_router.astype(f32)
    top_val, top_idx = lax.top_k(logits, TOP_K)
    gate = jax.nn.softmax(top_val, axis=-1)
    M = N * TOP_K
    flat_e = top_idx.reshape(-1)
    order = jnp.argsort(flat_e)
    sorted_e = flat_e[order]
    counts = jnp.bincount(flat_e, length=N_EXPERTS)
    padded = (counts + MOE_BLOCK - 1) // MOE_BLOCK * MOE_BLOCK
    start = jnp.cumsum(counts) - counts
    pend = jnp.cumsum(padded)
    pstart = pend - padded
    dest = pstart[sorted_e] + jnp.arange(M, dtype=counts.dtype) - start[sorted_e]
    P = M + N_EXPERTS * MOE_BLOCK
    nblk = P // MOE_BLOCK
    row_tok = jnp.zeros((P,), jnp.int32).at[dest].set((order // TOP_K).astype(jnp.int32))
    row_w = jnp.zeros((P,), f32).at[dest].set(gate.reshape(-1)[order])
    blk_e = jnp.minimum(jnp.searchsorted(pend, jnp.arange(nblk) * MOE_BLOCK, side='right'), N_EXPERTS - 1)
    xb = xt[row_tok].reshape(nblk, MOE_BLOCK, D)

    def expert_block(args):
        x_b, e = args
        hu = x_b @ w_up[e] + b_up[e]
        x_glu = jnp.minimum(hu[:, ::2], SWIGLU_LIMIT)
        x_lin = jnp.clip(hu[:, 1::2], -SWIGLU_LIMIT, SWIGLU_LIMIT)
        act = x_glu * jax.nn.sigmoid(SWIGLU_ALPHA * x_glu) * (x_lin + 1.0)
        return act @ w_down[e] + b_down[e]

    yb = lax.map(expert_block, (xb, blk_e)).reshape(P, D)
    y = jnp.zeros((N, D), h.dtype).at[row_tok].add(yb * row_w[:, None].astype(h.dtype))
    return y.reshape(B_, S_, D)


def setup_inputs(seed: int = 0) -> dict:
    key = jax.random.key(seed)
    ks = jax.random.split(key, 32)
    L, D = DEPTH, D_MODEL
    G, H, P = SSM_GROUPS, SSM_GROUP_DIM, SSM_STATE

    def nrm(k, shape, scale):
        return jax.random.normal(k, shape, jnp.float32) * scale

    x = nrm(ks[0], (BATCH, SEQ, D), 1.0)
    col_scale = jnp.concatenate([
        jnp.ones((2 * ATT_WIDTH,), jnp.float32),
        jnp.full((ATT_WIDTH,), DEEPNORM_BETA, jnp.float32),
        jnp.ones((N_IN - 3 * ATT_WIDTH,), jnp.float32)])
    w_in = nrm(ks[1], (L, D, N_IN), D ** -0.5) * col_scale
    b_forget = jnp.linspace(1.0, 6.0, ATT_HEADS, dtype=jnp.float32)[None, :] + nrm(ks[2], (L, ATT_HEADS), 0.1)
    w_pool = nrm(ks[3], (L, POOL_GROUPS, POOL_GROUP_DIM, POOL_GROUP_DIM), POOL_GROUP_DIM ** -0.5)
    pool_scale = 1.0 + nrm(ks[4], (L, POOL_WIDTH), 0.1)
    ssm_lambda_re = -0.5 + nrm(ks[5], (L, G, P), 0.01)
    ssm_lambda_im = jnp.broadcast_to(math.pi * jnp.arange(P, dtype=jnp.float32), (L, G, P))
    ssm_log_dt = jax.random.uniform(ks[6], (L, G), jnp.float32, math.log(DT_MIN), math.log(DT_MAX))
    ssm_b_re = nrm(ks[7], (L, G, P, H), (2.0 * H) ** -0.5)
    ssm_b_im = nrm(ks[8], (L, G, P, H), (2.0 * H) ** -0.5)
    ssm_c_re = nrm(ks[9], (L, G, H, P), P ** -0.5)
    ssm_c_im = nrm(ks[10], (L, G, H, P), P ** -0.5)
    ssm_d = nrm(ks[11], (L, G, H), 1.0)
    w_glu = nrm(ks[12], (L, SSM_WIDTH, SSM_WIDTH), SSM_WIDTH ** -0.5)
    b_glu = nrm(ks[13], (L, SSM_WIDTH), 0.01)
    w_branch_a = nrm(ks[14], (L, ATT_WIDTH, D), ATT_WIDTH ** -0.5 * DEEPNORM_BETA)
    w_branch_b = nrm(ks[15], (L, POOL_WIDTH, D), POOL_WIDTH ** -0.5 * DEEPNORM_BETA)
    w_branch_c = nrm(ks[16], (L, SSM_WIDTH, D), SSM_WIDTH ** -0.5 * DEEPNORM_BETA)
    w_out = nrm(ks[17], (L, D, D), D ** -0.5 * DEEPNORM_BETA)
    ln1_g = 1.0 + nrm(ks[18], (L, D), 0.02)
    ln1_b = nrm(ks[19], (L, D), 0.02)
    w_router = nrm(ks[20], (L, D, N_EXPERTS), D ** -0.5)
    b_router = nrm(ks[21], (L, N_EXPERTS), 0.01)
    w_up = nrm(ks[22], (L, N_EXPERTS, D, 2 * D_FF), D ** -0.5)
    b_up = nrm(ks[23], (L, N_EXPERTS, 2 * D_FF), 0.01)
    w_down = nrm(ks[24], (L, N_EXPERTS, D_FF, D), D_FF ** -0.5 * DEEPNORM_BETA)
    b_down = nrm(ks[25], (L, N_EXPERTS, D), 0.01)
    ln2_g = 1.0 + nrm(ks[26], (L, D), 0.02)
    ln2_b = nrm(ks[27], (L, D), 0.02)
    return {"x": x, "w_in": w_in, "b_forget": b_forget, "w_pool": w_pool, "pool_scale": pool_scale,
            "ssm_lambda_re": ssm_lambda_re, "ssm_lambda_im": ssm_lambda_im, "ssm_log_dt": ssm_log_dt,
            "ssm_b_re": ssm_b_re, "ssm_b_im": ssm_b_im, "ssm_c_re": ssm_c_re, "ssm_c_im": ssm_c_im,
            "ssm_d": ssm_d, "w_glu": w_glu, "b_glu": b_glu, "w_branch_a": w_branch_a,
            "w_branch_b": w_branch_b, "w_branch_c": w_branch_c, "w_out": w_out,
            "ln1_g": ln1_g, "ln1_b": ln1_b, "w_router": w_router, "b_router": b_router,
            "w_up": w_up, "b_up": b_up, "w_down": w_down, "b_down": b_down,
            "ln2_g": ln2_g, "ln2_b": ln2_b}


def reference(x, w_in, b_forget, w_pool, pool_scale, ssm_lambda_re, ssm_lambda_im, ssm_log_dt,
              ssm_b_re, ssm_b_im, ssm_c_re, ssm_c_im, ssm_d, w_glu, b_glu, w_branch_a,
              w_branch_b, w_branch_c, w_out, ln1_g, ln1_b, w_router, b_router,
              w_up, b_up, w_down, b_down, ln2_g, ln2_b):
    B_, S_, D = x.shape
    split_pts = _split_points()
    for l in range(DEPTH):
        h = x @ w_in[l]
        q, k, v, f_logit, u_pool, u_ssm, g = jnp.split(h, split_pts, axis=-1)
        shp = (B_, S_, ATT_HEADS, ATT_HEAD_DIM)
        y_a = _forgetting_attention(q.reshape(shp), k.reshape(shp), v.reshape(shp), f_logit, b_forget[l])
        y_b = _multiscale_pool(u_pool, w_pool[l], pool_scale[l])
        y_c = _s5(u_ssm, ssm_lambda_re[l], ssm_lambda_im[l], ssm_log_dt[l], ssm_b_re[l], ssm_b_im[l],
                  ssm_c_re[l], ssm_c_im[l], ssm_d[l], w_glu[l], b_glu[l])
        gates = jax.nn.sigmoid(g.reshape(B_, S_, N_BRANCH, D))
        merged = (gates[:, :, 0] * (y_a @ w_branch_a[l])
                  + gates[:, :, 1] * (y_b @ w_branch_b[l])
                  + gates[:, :, 2] * (y_c @ w_branch_c[l]))
        mix = merged @ w_out[l]
        x = _layer_norm(DEEPNORM_ALPHA * x + mix, ln1_g[l], ln1_b[l])
        ffn = _moe(x, w_router[l], b_router[l], w_up[l], b_up[l], w_down[l], b_down[l])
        x = _layer_norm(DEEPNORM_ALPHA * x + ffn, ln2_g[l], ln2_b[l])
    return x
```

```python
import functools
import math

import jax
import jax.numpy as jnp
import numpy as np
from jax import lax
from jax.experimental import pallas as pl
from jax.experimental.pallas import tpu as pltpu

F32 = jnp.float32
BF16 = jnp.bfloat16
I32 = jnp.int32
U32 = jnp.uint32

D_MODEL = 1024
DEPTH = 4
ATT_HEADS = 8
ATT_HEAD_DIM = 64
ATT_WIDTH = ATT_HEADS * ATT_HEAD_DIM
POOL_WINDOWS = (2, 4, 8, 16)
POOL_GROUPS = 4
POOL_WIDTH = 256
POOL_GROUP_DIM = 64
MAX_WINDOW = max(POOL_WINDOWS)
SSM_WIDTH = 256
SSM_GROUP_DIM = 16
SSM_GROUPS = 16
SSM_STATE = 64
SSM_STATES = SSM_GROUPS * SSM_STATE
N_BRANCH = 3
N_EXPERTS = 32
TOP_K = 4
D_FF = D_MODEL
SWIGLU_LIMIT = 7.0
SWIGLU_ALPHA = 1.702
LN_EPS = 1e-5
DEEPNORM_ALPHA = (2.0 * DEPTH) ** 0.25
GELU_C = math.sqrt(2.0 / math.pi)

LANES = 128
HALF_D = D_MODEL // 2
F_PAD = LANES
SMALL_W = POOL_WIDTH + SSM_WIDTH + F_PAD
QKV_W = 3 * ATT_WIDTH
TM_PROJ = 512
T_ATT = 512
TM_POOL = 256
TT_SSM = 64
TM_MERGE = 256
TM_ROUTE = 512
TM_ROW = 256
BM_EXP = 256
VMEM_LIMIT = 52 * 1024 * 1024

_NEG_BIG = -1e30


def _cparams(sem, **kw):
    return pltpu.CompilerParams(dimension_semantics=sem, vmem_limit_bytes=VMEM_LIMIT, **kw)


def _sigmoid(x):
    return 1.0 / (1.0 + jnp.exp(-x))


def _layer_norm(z, g, b):
    mu = jnp.mean(z, axis=-1, keepdims=True)
    zc = z - mu
    var = jnp.mean(zc * zc, axis=-1, keepdims=True)
    return zc * lax.rsqrt(var + LN_EPS) * g + b


def _pack_rows(y):
    u = pltpu.bitcast(y.astype(BF16).astype(F32), U32)
    return u[:, :HALF_D] | (u[:, HALF_D:] >> 16)


def _unpack_rows(p):
    hi = pltpu.bitcast(p & jnp.uint32(0xFFFF0000), F32)
    lo = pltpu.bitcast(p << 16, F32)
    return hi, lo


def _proj_kernel(x_ref, wqkv_ref, ws_ref, q_ref, k_ref, v_ref, up_ref, us_ref, f_ref):
    xb = x_ref[...].astype(BF16)
    for part, ref in enumerate((q_ref, k_ref, v_ref)):
        h = jnp.dot(xb, wqkv_ref[:, part * ATT_WIDTH:(part + 1) * ATT_WIDTH], preferred_element_type=F32)
        for hh in range(ATT_HEADS):
            ref[0, hh] = h[:, hh * ATT_HEAD_DIM:(hh + 1) * ATT_HEAD_DIM].astype(BF16)
    hs = jnp.dot(xb, ws_ref[...], preferred_element_type=F32)
    up_ref[...] = hs[:, :POOL_WIDTH]
    us_ref[...] = hs[:, POOL_WIDTH:POOL_WIDTH + SSM_WIDTH]
    f_ref[...] = hs[:, POOL_WIDTH + SSM_WIDTH:]


def _proj(x2, wqkv, wsmall, B, S):
    N = B * S
    nS = S // TM_PROJ
    hm = jax.ShapeDtypeStruct((B, ATT_HEADS, S, ATT_HEAD_DIM), BF16)
    hm_spec = pl.BlockSpec((1, ATT_HEADS, TM_PROJ, ATT_HEAD_DIM), lambda b, s: (b, 0, s, 0))
    return pl.pallas_call(
        _proj_kernel,
        grid=(B, nS),
        in_specs=[
            pl.BlockSpec((TM_PROJ, D_MODEL), lambda b, s: (b * nS + s, 0)),
            pl.BlockSpec((D_MODEL, QKV_W), lambda b, s: (0, 0)),
            pl.BlockSpec((D_MODEL, SMALL_W), lambda b, s: (0, 0)),
        ],
        out_specs=[
            hm_spec, hm_spec, hm_spec,
            pl.BlockSpec((TM_PROJ, POOL_WIDTH), lambda b, s: (b * nS + s, 0)),
            pl.BlockSpec((TM_PROJ, SSM_WIDTH), lambda b, s: (s, b)),
            pl.BlockSpec((TM_PROJ, F_PAD), lambda b, s: (b * nS + s, 0)),
        ],
        out_shape=[
            hm, hm, hm,
            jax.ShapeDtypeStruct((N, POOL_WIDTH), F32),
            jax.ShapeDtypeStruct((S, B * SSM_WIDTH), F32),
            jax.ShapeDtypeStruct((N, F_PAD), F32),
        ],
        compiler_params=_cparams(("parallel", "parallel")),
        name="in_proj",
    )(x2, wqkv, wsmall)


def _fcum_kernel(f_ref, b_ref, c_ref):
    rows, S = f_ref.shape
    lane = lax.broadcasted_iota(I32, (rows, LANES), 1)
    carry = jnp.zeros((rows, 1), F32)
    for ch in range(S // LANES):
        z = f_ref[:, ch * LANES:(ch + 1) * LANES] + b_ref[...]
        lf = jnp.minimum(z, 0.0) - jnp.log1p(jnp.exp(-jnp.abs(z)))
        sh = 1
        while sh < LANES:
            lf = lf + jnp.where(lane >= sh, pltpu.roll(lf, sh, 1), 0.0)
            sh *= 2
        lf = lf + carry
        c_ref[:, ch * LANES:(ch + 1) * LANES] = lf
        carry = lf[:, LANES - 1:LANES]


def _fcum(f_rows, b_rows):
    rows, S = f_rows.shape
    return pl.pallas_call(
        _fcum_kernel,
        out_shape=jax.ShapeDtypeStruct((rows, S), F32),
        compiler_params=pltpu.CompilerParams(vmem_limit_bytes=VMEM_LIMIT),
        name="forget_cumsum",
    )(f_rows, b_rows)


def _attn_kernel(q_ref, k_ref, v_ref, c_ref, o_ref):
    T = T_ATT
    qi = pl.program_id(2)
    row = lax.broadcasted_iota(I32, (T, T), 0)
    col = lax.broadcasted_iota(I32, (T, T), 1)
    causal = col <= row
    for hh in range(2):
        q = q_ref[0, hh]

        def block(j, carry, masked, hh=hh, q=q):
            m, l, acc = carry
            start = pl.multiple_of(j * T, T)
            k = k_ref[0, hh, pl.ds(start, T), :]
            v = v_ref[0, hh, pl.ds(start, T), :]
            s = lax.dot_general(q, k, (((1,), (1,)), ((), ())), preferred_element_type=F32)
            s = s - c_ref[0, hh, :, pl.ds(start, T)]
            if masked:
                s = jnp.where(causal, s, -jnp.inf)
            m_new = jnp.maximum(m, jnp.max(s, axis=-1, keepdims=True))
            p = jnp.exp(s - m_new)
            a = jnp.exp(m - m_new)
            l = a * l + jnp.sum(p, axis=-1, keepdims=True)
            acc = a * acc + jnp.dot(p.astype(BF16), v, preferred_element_type=F32)
            return m_new, l, acc

        init = (jnp.full((T, 1), -jnp.inf, F32), jnp.zeros((T, 1), F32), jnp.zeros((T, ATT_HEAD_DIM), F32))
        carry = lax.fori_loop(0, qi, functools.partial(block, masked=False), init)
        m, l, acc = block(qi, carry, True)
        o_ref[0, :, hh * ATT_HEAD_DIM:(hh + 1) * ATT_HEAD_DIM] = (acc / l).astype(BF16)


def _attention(q, k, v, c, B, S):
    nq = S // T_ATT
    qspec = pl.BlockSpec((1, 2, T_ATT, ATT_HEAD_DIM), lambda b, hp, i: (b, hp, i, 0))
    kvspec = pl.BlockSpec((1, 2, S, ATT_HEAD_DIM), lambda b, hp, i: (b, hp, 0, 0))
    return pl.pallas_call(
        _attn_kernel,
        grid=(B, ATT_HEADS // 2, nq),
        in_specs=[qspec, kvspec, kvspec,
                  pl.BlockSpec((1, 2, 1, S), lambda b, hp, i: (b, hp, 0, 0))],
        out_specs=pl.BlockSpec((1, T_ATT, 2 * ATT_HEAD_DIM), lambda b, hp, i: (b, i, hp)),
        out_shape=jax.ShapeDtypeStruct((B, S, ATT_WIDTH), BF16),
        compiler_params=_cparams(("parallel", "parallel", "arbitrary")),
        name="fox_attention",
    )(q, k, v, c)


def _pool_kernel(u_ref, w_ref, sc_ref, o_ref, pad_ref):
    S = u_ref.shape[1]
    R = TM_POOL
    pad_ref[0:MAX_WINDOW, :] = jnp.zeros((MAX_WINDOW, POOL_WIDTH), F32)
    pad_ref[MAX_WINDOW:, :] = u_ref[0]
    lane = lax.broadcasted_iota(I32, (R, POOL_WIDTH), 1)
    trow = lax.broadcasted_iota(I32, (R, POOL_WIDTH), 0)
    grp = lane // POOL_GROUP_DIM
    win = jnp.where(grp == 0, POOL_WINDOWS[0],
                    jnp.where(grp == 1, POOL_WINDOWS[1], jnp.where(grp == 2, POOL_WINDOWS[2], POOL_WINDOWS[3])))
    for i in range(S // R):
        base = MAX_WINDOW + i * R
        u0 = pad_ref[base:base + R, :]
        acc = u0
        sums = {}
        for kk in range(1, MAX_WINDOW):
            acc = acc + pad_ref[base - kk:base - kk + R, :]
            if kk + 1 in POOL_WINDOWS:
                sums[kk + 1] = acc
        total = jnp.where(grp == 0, sums[POOL_WINDOWS[0]],
                          jnp.where(grp == 1, sums[POOL_WINDOWS[1]],
                                    jnp.where(grp == 2, sums[POOL_WINDOWS[2]], sums[POOL_WINDOWS[3]])))
        cnt = jnp.minimum(trow + (i * R + 1), win).astype(F32)
        mixed = total / cnt - u0
        y = jnp.dot(mixed.astype(BF16), w_ref[...], preferred_element_type=F32) * sc_ref[...]
        o_ref[0, i * R:(i + 1) * R, :] = y.astype(BF16)


def _pool(u3, w_bd, scale):
    B, S, _ = u3.shape
    return pl.pallas_call(
        _pool_kernel,
        grid=(B,),
        in_specs=[pl.BlockSpec((1, S, POOL_WIDTH), lambda b: (b, 0, 0)),
                  pl.BlockSpec((POOL_WIDTH, POOL_WIDTH), lambda b: (0, 0)),
                  pl.BlockSpec((1, POOL_WIDTH), lambda b: (0, 0))],
        out_specs=pl.BlockSpec((1, S, POOL_WIDTH), lambda b: (b, 0, 0)),
        out_shape=jax.ShapeDtypeStruct((B, S, POOL_WIDTH), BF16),
        scratch_shapes=[pltpu.VMEM((S + MAX_WINDOW, POOL_WIDTH), F32)],
        compiler_params=_cparams(("parallel",)),
        name="multiscale_pool",
    )(u3, w_bd, scale)


def _ssm_prep_kernel(lr_ref, li_ref, ldt_ref, brT_ref, biT_ref, ar_ref, ai_ref, bbr_ref, bbi_ref):
    lr = lr_ref[...]
    li = li_ref[...]
    dt = jnp.exp(ldt_ref[...])
    mag = jnp.exp(lr * dt)
    ar = mag * jnp.cos(li * dt)
    ai = mag * jnp.sin(li * dt)
    den = lr * lr + li * li
    nr = ar - 1.0
    zr = (nr * lr + ai * li) / den
    zi = (ai * lr - nr * li) / den
    ar_ref[...] = ar
    ai_ref[...] = ai
    br = brT_ref[...]
    bi = biT_ref[...]
    bbr_ref[...] = zr[:, None, :] * br - zi[:, None, :] * bi
    bbi_ref[...] = zr[:, None, :] * bi + zi[:, None, :] * br


def _ssm_prep(lr, li, log_dt, b_re, b_im):
    G, P, H = b_re.shape
    gp = jax.ShapeDtypeStruct((G, P), F32)
    ghp = jax.ShapeDtypeStruct((G, H, P), F32)
    return pl.pallas_call(
        _ssm_prep_kernel,
        out_shape=[gp, gp, ghp, ghp],
        name="s5_discretise",
    )(lr, li, log_dt.reshape(G, 1), b_re.transpose(0, 2, 1), b_im.transpose(0, 2, 1))


def _gelu_tanh(y):
    return 0.5 * y * (1.0 + jnp.tanh(GELU_C * (y + 0.044715 * (y * y * y))))


def _ssm_kernel(u_ref, bb_ref, ar_ref, ai_ref, cc_ref, d_ref, wg_ref, bg_ref, o_ref, x_scr, hr_scr, hi_scr):
    nb = hr_scr.shape[0]
    tt = u_ref.shape[0] // nb

    @pl.when(pl.program_id(0) == 0)
    def _():
        hr_scr[...] = jnp.zeros_like(hr_scr)
        hi_scr[...] = jnp.zeros_like(hi_scr)

    u = u_ref[...]
    x_scr[...] = jnp.dot(u.astype(BF16), bb_ref[...], preferred_element_type=F32)

    def step(t, carry):
        hr, hi = carry
        r0 = pl.multiple_of(t * nb, nb)
        xr = x_scr[pl.ds(r0, nb), 0:SSM_STATES]
        xi = x_scr[pl.ds(r0, nb), SSM_STATES:2 * SSM_STATES]
        ar = ar_ref[...]
        ai = ai_ref[...]
        nr = ar * hr - ai * hi + xr
        ni = ar * hi + ai * hr + xi
        x_scr[pl.ds(r0, nb), 0:SSM_STATES] = nr
        x_scr[pl.ds(r0, nb), SSM_STATES:2 * SSM_STATES] = ni
        return nr, ni

    hr, hi = lax.fori_loop(0, tt, step, (hr_scr[...], hi_scr[...]))
    hr_scr[...] = hr
    hi_scr[...] = hi
    y = jnp.dot(x_scr[...].astype(BF16), cc_ref[...], preferred_element_type=F32) + d_ref[...] * u
    y = _gelu_tanh(y)
    gl = jnp.dot(y.astype(BF16), wg_ref[...], preferred_element_type=F32) + bg_ref[...]
    o_ref[...] = (y * _sigmoid(gl)).astype(BF16)


def _ssm(u_tm, bblk, ar_b, ai_b, cblk, dvec, wglu, bglu, B, S):
    rows = TT_SSM * B
    const = lambda shape: pl.BlockSpec(shape, lambda i: (0, 0))
    return pl.pallas_call(
        _ssm_kernel,
        grid=(S // TT_SSM,),
        in_specs=[pl.BlockSpec((rows, SSM_WIDTH), lambda i: (i, 0)),
                  const((SSM_WIDTH, 2 * SSM_STATES)),
                  const((B, SSM_STATES)), const((B, SSM_STATES)),
                  const((2 * SSM_STATES, SSM_WIDTH)),
                  const((1, SSM_WIDTH)), const((SSM_WIDTH, SSM_WIDTH)), const((1, SSM_WIDTH))],
        out_specs=pl.BlockSpec((rows, SSM_WIDTH), lambda i: (i, 0)),
        out_shape=jax.ShapeDtypeStruct((S * B, SSM_WIDTH), BF16),
        scratch_shapes=[pltpu.VMEM((rows, 2 * SSM_STATES), F32),
                        pltpu.VMEM((B, SSM_STATES), F32), pltpu.VMEM((B, SSM_STATES), F32)],
        compiler_params=_cparams(("arbitrary",)),
        name="s5_scan",
    )(u_tm, bblk, ar_b, ai_b, cblk, dvec, wglu, bglu)


def _merge_kernel(x_ref, ya_ref, yb_ref, yc_ref, wg_ref, wa_ref, wb_ref, wc_ref, wo_ref, g_ref, b_ref, o_ref):
    x = x_ref[...]
    xb = x.astype(BF16)
    merged = None
    for i, (y_ref, w_ref) in enumerate(((ya_ref, wa_ref), (yb_ref, wb_ref), (yc_ref, wc_ref))):
        gate = _sigmoid(jnp.dot(xb, wg_ref[:, i * D_MODEL:(i + 1) * D_MODEL], preferred_element_type=F32))
        term = gate * jnp.dot(y_ref[...], w_ref[...], preferred_element_type=F32)
        merged = term if merged is None else merged + term
    mix = jnp.dot(merged.astype(BF16), wo_ref[...], preferred_element_type=F32)
    o_ref[...] = _layer_norm(DEEPNORM_ALPHA * x + mix, g_ref[...], b_ref[...])


def _merge(x2, ya, yb, yc_tm, wg, wa, wb, wc, wo, g, b, B, S):
    N = B * S
    nS = S // TM_MERGE
    const = lambda shape: pl.BlockSpec(shape, lambda bb, s: (0, 0))
    row = lambda w: pl.BlockSpec((TM_MERGE, w), lambda bb, s: (bb * nS + s, 0))
    return pl.pallas_call(
        _merge_kernel,
        grid=(B, nS),
        in_specs=[row(D_MODEL), row(ATT_WIDTH), row(POOL_WIDTH),
                  pl.BlockSpec((TM_MERGE, SSM_WIDTH), lambda bb, s: (s, bb)),
                  const((D_MODEL, N_BRANCH * D_MODEL)), const((ATT_WIDTH, D_MODEL)),
                  const((POOL_WIDTH, D_MODEL)), const((SSM_WIDTH, D_MODEL)), const((D_MODEL, D_MODEL)),
                  const((1, D_MODEL)), const((1, D_MODEL))],
        out_specs=row(D_MODEL),
        out_shape=jax.ShapeDtypeStruct((N, D_MODEL), F32),
        compiler_params=_cparams(("parallel", "parallel")),
        name="merge_ln1",
    )(x2, ya, yb, yc_tm, wg, wa, wb, wc, wo, g, b)


def _route_kernel(x_ref, wr_ref, br_ref, idx_ref, gate_ref, rank_ref, cnt_ref, base_scr):
    tm = x_ref.shape[0]

    @pl.when(pl.program_id(0) == 0)
    def _():
        base_scr[...] = jnp.zeros_like(base_scr)

    logits = jnp.dot(x_ref[...], wr_ref[...], preferred_element_type=F32,
                     precision=lax.Precision.HIGHEST) + br_ref[...]
    lane = lax.broadcasted_iota(I32, (tm, LANES), 1)
    lane_f = lane.astype(F32)
    work = logits
    vals, idxs = [], []
    multihot = jnp.zeros((tm, LANES), F32)
    for _ in range(TOP_K):
        mx = jnp.max(work, axis=-1, keepdims=True)
        ix = jnp.min(jnp.where(work == mx, lane_f, float(LANES)), axis=-1, keepdims=True)
        sel = lane_f == ix
        work = jnp.where(sel, -jnp.inf, work)
        multihot = multihot + sel.astype(F32)
        vals.append(mx)
        idxs.append(ix)
    exps = [jnp.exp(v - vals[0]) for v in vals]
    denom = exps[0] + exps[1] + exps[2] + exps[3]
    r = lax.broadcasted_iota(I32, (tm, tm), 0)
    c = lax.broadcasted_iota(I32, (tm, tm), 1)
    lower = (c < r).astype(BF16)
    before = jnp.dot(lower, multihot.astype(BF16), preferred_element_type=F32) + base_scr[...]
    idx_out = jnp.zeros((tm, LANES), F32)
    gate_out = jnp.zeros((tm, LANES), F32)
    rank_out = jnp.zeros((tm, LANES), F32)
    for kk in range(TOP_K):
        rk = jnp.sum(jnp.where(lane_f == idxs[kk], before, 0.0), axis=-1, keepdims=True)
        here = lane == kk
        idx_out = jnp.where(here, idxs[kk], idx_out)
        gate_out = jnp.where(here, exps[kk] / denom, gate_out)
        rank_out = jnp.where(here, rk, rank_out)
    idx_ref[...] = idx_out.astype(I32)
    gate_ref[...] = gate_out
    rank_ref[...] = rank_out.astype(I32)
    base_new = base_scr[...] + jnp.sum(multihot, axis=0, keepdims=True)
    base_scr[...] = base_new
    cnt_ref[...] = base_new.astype(I32)


def _route(x1, wr, br):
    N = x1.shape[0]
    row = pl.BlockSpec((TM_ROUTE, LANES), lambda i: (i, 0))
    return pl.pallas_call(
        _route_kernel,
        grid=(N // TM_ROUTE,),
        in_specs=[pl.BlockSpec((TM_ROUTE, D_MODEL), lambda i: (i, 0)),
                  pl.BlockSpec((D_MODEL, LANES), lambda i: (0, 0)),
                  pl.BlockSpec((1, LANES), lambda i: (0, 0))],
        out_specs=[row, row, row, pl.BlockSpec((1, LANES), lambda i: (0, 0))],
        out_shape=[jax.ShapeDtypeStruct((N, LANES), I32), jax.ShapeDtypeStruct((N, LANES), F32),
                   jax.ShapeDtypeStruct((N, LANES), I32), jax.ShapeDtypeStruct((1, LANES), I32)],
        scratch_shapes=[pltpu.VMEM((1, LANES), F32)],
        compiler_params=_cparams(("arbitrary",)),
        name="router_top4",
    )(x1, wr, br)


def _dispatch_kernel(dest_ref, x_ref, xs_hbm, pk_scr, sem):
    tm = x_ref.shape[0]
    pk_scr[...] = _pack_rows(x_ref[...])

    def row_copy(r, kk):
        d = dest_ref[0, 0, r * TOP_K + kk]
        return pltpu.make_async_copy(pk_scr.at[pl.ds(r, 1)], xs_hbm.at[pl.ds(d, 1)], sem)

    def issue(r, _):
        for kk in range(TOP_K):
            row_copy(r, kk).start()
        return 0

    def drain(r, _):
        for kk in range(TOP_K):
            row_copy(r, kk).wait()
        return 0

    lax.fori_loop(0, tm, issue, 0)
    lax.fori_loop(0, tm, drain, 0)


def _dispatch(dest3, x1):
    N = x1.shape[0]
    M = N * TOP_K
    return pl.pallas_call(
        _dispatch_kernel,
        grid=(N // TM_ROW,),
        in_specs=[pl.BlockSpec((1, 1, TM_ROW * TOP_K), lambda i: (i, 0, 0), memory_space=pltpu.SMEM),
                  pl.BlockSpec((TM_ROW, D_MODEL), lambda i: (i, 0))],
        out_specs=pl.BlockSpec(memory_space=pl.ANY),
        out_shape=jax.ShapeDtypeStruct((M, HALF_D), U32),
        scratch_shapes=[pltpu.VMEM((TM_ROW, HALF_D), U32), pltpu.SemaphoreType.DMA],
        compiler_params=_cparams(("arbitrary",), has_side_effects=True),
        name="moe_dispatch",
    )(dest3, x1)


def _expert_kernel(blk_ref, exp_ref, lo_ref, hi_ref, xs_ref, wg_ref, bg_ref, wl_ref, bl_ref, wd_ref, bd_ref, o_ref):
    w = pl.program_id(0)
    lo = lo_ref[w]
    hi = hi_ref[w]

    @pl.when(hi > lo)
    def _():
        xh, xl = _unpack_rows(xs_ref[...])
        x = jnp.concatenate([xh.astype(BF16), xl.astype(BF16)], axis=1)
        g = jnp.dot(x, wg_ref[0], preferred_element_type=F32) + bg_ref[0]
        l = jnp.dot(x, wl_ref[0], preferred_element_type=F32) + bl_ref[0]
        g = jnp.minimum(g, SWIGLU_LIMIT)
        l = jnp.clip(l, -SWIGLU_LIMIT, SWIGLU_LIMIT)
        act = g * _sigmoid(SWIGLU_ALPHA * g) * (l + 1.0)
        y = jnp.dot(act.astype(BF16), wd_ref[0], preferred_element_type=F32) + bd_ref[0]
        packed = _pack_rows(y)

        @pl.when(lo == 0)
        def _():
            o_ref[...] = packed

        @pl.when(lo > 0)
        def _():
            row = lax.broadcasted_iota(I32, packed.shape, 0)
            o_ref[...] = jnp.where((row >= lo) & (row < hi), packed, o_ref[...])


def _experts(meta, xs, wg, bg, wl, bl, wd, bd):
    M = xs.shape[0]
    W = meta[0].shape[0]
    wspec = pl.BlockSpec((1, D_MODEL, D_FF), lambda w, blk, ex, lo, hi: (ex[w], 0, 0))
    bspec = pl.BlockSpec((1, 1, D_FF), lambda w, blk, ex, lo, hi: (ex[w], 0, 0))
    rows = pl.BlockSpec((BM_EXP, HALF_D), lambda w, blk, ex, lo, hi: (blk[w], 0))
    return pl.pallas_call(
        _expert_kernel,
        grid_spec=pltpu.PrefetchScalarGridSpec(
            num_scalar_prefetch=4,
            grid=(W,),
            in_specs=[rows, wspec, bspec, wspec, bspec, wspec, bspec],
            out_specs=rows,
        ),
        out_shape=jax.ShapeDtypeStruct((M, HALF_D), U32),
        compiler_params=_cparams(("arbitrary",)),
        name="moe_experts",
    )(*meta, xs, wg, bg, wl, bl, wd, bd)


def _combine_kernel(dest_ref, x_ref, gate_ref, g_ref, b_ref, yb_hbm, o_ref, buf, sem):
    tm = x_ref.shape[0]

    def row_copy(r, kk):
        d = dest_ref[0, 0, r * TOP_K + kk]
        return pltpu.make_async_copy(yb_hbm.at[pl.ds(d, 1)], buf.at[kk, pl.ds(r, 1)], sem)

    def issue(r, _):
        for kk in range(TOP_K):
            row_copy(r, kk).start()
        return 0

    def drain(r, _):
        for kk in range(TOP_K):
            row_copy(r, kk).wait()
        return 0

    lax.fori_loop(0, tm, issue, 0)
    lax.fori_loop(0, tm, drain, 0)
    gates = gate_ref[...]
    yh = jnp.zeros((tm, HALF_D), F32)
    yl = jnp.zeros((tm, HALF_D), F32)
    for kk in range(TOP_K):
        hi, lo = _unpack_rows(buf[kk])
        gk = gates[:, kk:kk + 1]
        yh = yh + gk * hi
        yl = yl + gk * lo
    z = DEEPNORM_ALPHA * x_ref[...] + jnp.concatenate([yh, yl], axis=1)
    o_ref[...] = _layer_norm(z, g_ref[...], b_ref[...])


def _combine(dest3, x1, gate, g, b, yb):
    N = x1.shape[0]
    return pl.pallas_call(
        _combine_kernel,
        grid=(N // TM_ROW,),
        in_specs=[pl.BlockSpec((1, 1, TM_ROW * TOP_K), lambda i: (i, 0, 0), memory_space=pltpu.SMEM),
                  pl.BlockSpec((TM_ROW, D_MODEL), lambda i: (i, 0)),
                  pl.BlockSpec((TM_ROW, LANES), lambda i: (i, 0)),
                  pl.BlockSpec((1, D_MODEL), lambda i: (0, 0)),
                  pl.BlockSpec((1, D_MODEL), lambda i: (0, 0)),
                  pl.BlockSpec(memory_space=pl.ANY)],
        out_specs=pl.BlockSpec((TM_ROW, D_MODEL), lambda i: (i, 0)),
        out_shape=jax.ShapeDtypeStruct((N, D_MODEL), F32),
        scratch_shapes=[pltpu.VMEM((TOP_K, TM_ROW, HALF_D), U32), pltpu.SemaphoreType.DMA],
        compiler_params=_cparams(("arbitrary",)),
        name="moe_combine_ln2",
    )(dest3, x1, gate, g, b, yb)


def _work_items(counts, M):
    nblk = M // BM_EXP
    W = nblk + N_EXPERTS
    ends = jnp.cumsum(counts)
    starts = ends - counts
    first = starts // BM_EXP
    last = jnp.maximum(ends - 1, 0) // BM_EXP
    n_items = jnp.where(counts > 0, last - first + 1, 0)
    item_end = jnp.cumsum(n_items)
    item_start = item_end - n_items
    total = item_end[-1]
    w = jnp.arange(W, dtype=I32)
    e = jnp.minimum(jnp.searchsorted(item_end, w, side="right"), N_EXPERTS - 1).astype(I32)
    blk = first[e] + (w - item_start[e])
    lo = jnp.clip(starts[e] - blk * BM_EXP, 0, BM_EXP)
    hi = jnp.clip(ends[e] - blk * BM_EXP, 0, BM_EXP)
    valid = w < total
    e_last = e[jnp.maximum(total - 1, 0)]
    blk = jnp.where(valid, blk, nblk - 1)
    e = jnp.where(valid, e, e_last)
    lo = jnp.where(valid, lo, 0)
    hi = jnp.where(valid, hi, 0)
    return (blk.astype(I32), e.astype(I32), lo.astype(I32), hi.astype(I32)), starts


def _block_diag(blocks):
    G, a, b = blocks.shape
    eye = jnp.eye(G, dtype=blocks.dtype)
    return jnp.einsum("gab,gh->gahb", blocks, eye).reshape(G * a, G * b)


def kernel(x, w_in, b_forget, w_pool, pool_scale, ssm_lambda_re, ssm_lambda_im, ssm_log_dt, ssm_b_re, ssm_b_im, ssm_c_re, ssm_c_im, ssm_d, w_glu, b_glu, w_branch_a, w_branch_b, w_branch_c, w_out, ln1_g, ln1_b, w_router, b_router, w_up, b_up, w_down, b_down, ln2_g, ln2_b):
    B, S, D = x.shape
    assert D == D_MODEL and S % T_ATT == 0 and S % TM_PROJ == 0 and (B * S) % TM_ROUTE == 0
    N = B * S
    M = N * TOP_K
    x2 = x.reshape(N, D)
    scale = ATT_HEAD_DIM ** -0.5
    for l in range(DEPTH):
        wl = w_in[l]
        wq = wl[:, :ATT_WIDTH] * scale
        wqkv = jnp.concatenate([wq, wl[:, ATT_WIDTH:QKV_W]], axis=1).astype(BF16)
        c0 = QKV_W
        wf = wl[:, c0:c0 + ATT_HEADS]
        wsmall = jnp.concatenate(
            [wl[:, c0 + ATT_HEADS:c0 + ATT_HEADS + POOL_WIDTH + SSM_WIDTH], wf,
             jnp.zeros((D, F_PAD - ATT_HEADS), F32)], axis=1).astype(BF16)
        wgates = wl[:, c0 + ATT_HEADS + POOL_WIDTH + SSM_WIDTH:].astype(BF16)

        q, k, v, u_pool, u_ssm_tm, f_pad = _proj(x2, wqkv, wsmall, B, S)

        f_rows = f_pad[:, :ATT_HEADS].reshape(B, S, ATT_HEADS).transpose(0, 2, 1).reshape(B * ATT_HEADS, S)
        b_rows = jnp.tile(b_forget[l], B).reshape(B * ATT_HEADS, 1)
        c = _fcum(f_rows, b_rows).reshape(B, ATT_HEADS, 1, S)
        y_a = _attention(q, k, v, c, B, S).reshape(N, ATT_WIDTH)

        w_pool_bd = _block_diag(w_pool[l]).astype(BF16)
        y_b = _pool(u_pool.reshape(B, S, POOL_WIDTH), w_pool_bd, pool_scale[l].reshape(1, POOL_WIDTH)).reshape(N, POOL_WIDTH)

        ar, ai, bbrT, bbiT = _ssm_prep(ssm_lambda_re[l], ssm_lambda_im[l], ssm_log_dt[l], ssm_b_re[l], ssm_b_im[l])
        bblk = jnp.concatenate([_block_diag(bbrT), _block_diag(bbiT)], axis=1).astype(BF16)
        cblk = jnp.concatenate([_block_diag(ssm_c_re[l].transpose(0, 2, 1)),
                                -_block_diag(ssm_c_im[l].transpose(0, 2, 1))], axis=0).astype(BF16)
        ar_b = jnp.broadcast_to(ar.reshape(1, SSM_STATES), (B, SSM_STATES))
        ai_b = jnp.broadcast_to(ai.reshape(1, SSM_STATES), (B, SSM_STATES))
        y_c_tm = _ssm(u_ssm_tm.reshape(S * B, SSM_WIDTH), bblk, ar_b, ai_b, cblk,
                      ssm_d[l].reshape(1, SSM_WIDTH), w_glu[l].astype(BF16), b_glu[l].reshape(1, SSM_WIDTH), B, S)

        x1 = _merge(x2, y_a, y_b, y_c_tm.reshape(S, B * SSM_WIDTH), wgates,
                    w_branch_a[l].astype(BF16), w_branch_b[l].astype(BF16), w_branch_c[l].astype(BF16),
                    w_out[l].astype(BF16), ln1_g[l].reshape(1, D), ln1_b[l].reshape(1, D), B, S)

        wr = jnp.concatenate([w_router[l], jnp.zeros((D, LANES - N_EXPERTS), F32)], axis=1)
        br = jnp.concatenate([b_router[l], jnp.full((LANES - N_EXPERTS,), _NEG_BIG, F32)]).reshape(1, LANES)
        idx, gate, rank, cnt = _route(x1, wr, br)
        counts = cnt[0, :N_EXPERTS]
        meta, starts = _work_items(counts, M)
        top_idx = idx[:, :TOP_K]
        onehot = top_idx[:, :, None] == jnp.arange(N_EXPERTS, dtype=I32)[None, None, :]
        dest = rank[:, :TOP_K] + jnp.sum(jnp.where(onehot, starts[None, None, :], 0), axis=-1)
        dest3 = dest.astype(I32).reshape(N // TM_ROW, 1, TM_ROW * TOP_K)
        xs = _dispatch(dest3, x1)
        wu = w_up[l]
        yb = _experts(meta, xs,
                      wu[:, :, 0::2].astype(BF16), b_up[l][:, None, 0::2],
                      wu[:, :, 1::2].astype(BF16), b_up[l][:, None, 1::2],
                      w_down[l].astype(BF16), b_down[l][:, None, :])
        x2 = _combine(dest3, x1, gate, ln2_g[l].reshape(1, D), ln2_b[l].reshape(1, D), yb)
    return x2.reshape(B, S, D)
```

```python
import functools
import math

import jax
import jax.numpy as jnp
import numpy as np
from jax import lax
from jax.experimental import pallas as pl
from jax.experimental.pallas import tpu as pltpu

F32 = jnp.float32
BF16 = jnp.bfloat16
I32 = jnp.int32
U32 = jnp.uint32

D_MODEL = 1024
DEPTH = 4
ATT_HEADS = 8
ATT_HEAD_DIM = 64
ATT_WIDTH = ATT_HEADS * ATT_HEAD_DIM
POOL_WINDOWS = (2, 4, 8, 16)
POOL_GROUPS = 4
POOL_WIDTH = 256
POOL_GROUP_DIM = 64
MAX_WINDOW = max(POOL_WINDOWS)
SSM_WIDTH = 256
SSM_GROUP_DIM = 16
SSM_GROUPS = 16
SSM_STATE = 64
SSM_STATES = SSM_GROUPS * SSM_STATE
N_BRANCH = 3
N_EXPERTS = 32
TOP_K = 4
D_FF = D_MODEL
SWIGLU_LIMIT = 7.0
SWIGLU_ALPHA = 1.702
LN_EPS = 1e-5
DEEPNORM_ALPHA = (2.0 * DEPTH) ** 0.25
GELU_C = math.sqrt(2.0 / math.pi)

LANES = 128
HALF_D = D_MODEL // 2
F_PAD = LANES
SMALL_W = POOL_WIDTH + SSM_WIDTH + F_PAD
QKV_W = 3 * ATT_WIDTH
TM_PROJ = 512
T_ATT = 512
TM_POOL = 256
TT_SSM = 64
TM_MERGE = 256
TM_ROUTE = 512
TM_ROW = 256
BM_EXP = 256
VMEM_LIMIT = 52 * 1024 * 1024

_NEG_BIG = -1e30


def _cparams(sem, **kw):
    return pltpu.CompilerParams(dimension_semantics=sem, vmem_limit_bytes=VMEM_LIMIT, **kw)


def _sigmoid(x):
    return 1.0 / (1.0 + jnp.exp(-x))


def _layer_norm(z, g, b):
    mu = jnp.mean(z, axis=-1, keepdims=True)
    zc = z - mu
    var = jnp.mean(zc * zc, axis=-1, keepdims=True)
    return zc * lax.rsqrt(var + LN_EPS) * g + b


def _pack_rows(y):
    u = pltpu.bitcast(y.astype(BF16).astype(F32), U32)
    return u[:, :HALF_D] | (u[:, HALF_D:] >> 16)


def _unpack_rows(p):
    hi = pltpu.bitcast(p & jnp.uint32(0xFFFF0000), F32)
    lo = pltpu.bitcast(p << 16, F32)
    return hi, lo


def _proj_kernel(x_ref, wqkv_ref, ws_ref, q_ref, k_ref, v_ref, up_ref, us_ref, f_ref):
    xb = x_ref[...].astype(BF16)
    for part, ref in enumerate((q_ref, k_ref, v_ref)):
        h = jnp.dot(xb, wqkv_ref[:, part * ATT_WIDTH:(part + 1) * ATT_WIDTH], preferred_element_type=F32)
        for hh in range(ATT_HEADS):
            ref[0, hh] = h[:, hh * ATT_HEAD_DIM:(hh + 1) * ATT_HEAD_DIM].astype(BF16)
    hs = jnp.dot(xb, ws_ref[...], preferred_element_type=F32)
    up_ref[...] = hs[:, :POOL_WIDTH]
    us_ref[...] = hs[:, POOL_WIDTH:POOL_WIDTH + SSM_WIDTH]
    f_ref[...] = hs[:, POOL_WIDTH + SSM_WIDTH:]


def _proj(x2, wqkv, wsmall, B, S):
    N = B * S
    nS = S // TM_PROJ
    hm = jax.ShapeDtypeStruct((B, ATT_HEADS, S, ATT_HEAD_DIM), BF16)
    hm_spec = pl.BlockSpec((1, ATT_HEADS, TM_PROJ, ATT_HEAD_DIM), lambda b, s: (b, 0, s, 0))
    return pl.pallas_call(
        _proj_kernel,
        grid=(B, nS),
        in_specs=[
            pl.BlockSpec((TM_PROJ, D_MODEL), lambda b, s: (b * nS + s, 0)),
            pl.BlockSpec((D_MODEL, QKV_W), lambda b, s: (0, 0)),
            pl.BlockSpec((D_MODEL, SMALL_W), lambda b, s: (0, 0)),
        ],
        out_specs=[
            hm_spec, hm_spec, hm_spec,
            pl.BlockSpec((TM_PROJ, POOL_WIDTH), lambda b, s: (b * nS + s, 0)),
            pl.BlockSpec((TM_PROJ, SSM_WIDTH), lambda b, s: (s, b)),
            pl.BlockSpec((TM_PROJ, F_PAD), lambda b, s: (b * nS + s, 0)),
        ],
        out_shape=[
            hm, hm, hm,
            jax.ShapeDtypeStruct((N, POOL_WIDTH), F32),
            jax.ShapeDtypeStruct((S, B * SSM_WIDTH), F32),
            jax.ShapeDtypeStruct((N, F_PAD), F32),
        ],
        compiler_params=_cparams(("parallel", "parallel")),
        name="in_proj",
    )(x2, wqkv, wsmall)


def _fcum_kernel(f_ref, b_ref, c_ref):
    rows, S = f_ref.shape
    lane = lax.broadcasted_iota(I32, (rows, LANES), 1)
    carry = jnp.zeros((rows, 1), F32)
    for ch in range(S // LANES):
        z = f_ref[:, ch * LANES:(ch + 1) * LANES] + b_ref[...]
        lf = jnp.minimum(z, 0.0) - jnp.log1p(jnp.exp(-jnp.abs(z)))
        sh = 1
        while sh < LANES:
            lf = lf + jnp.where(lane >= sh, pltpu.roll(lf, sh, 1), 0.0)
            sh *= 2
        lf = lf + carry
        c_ref[:, ch * LANES:(ch + 1) * LANES] = lf
        carry = lf[:, LANES - 1:LANES]


def _fcum(f_rows, b_rows):
    rows, S = f_rows.shape
    return pl.pallas_call(
        _fcum_kernel,
        out_shape=jax.ShapeDtypeStruct((rows, S), F32),
        compiler_params=pltpu.CompilerParams(vmem_limit_bytes=VMEM_LIMIT),
        name="forget_cumsum",
    )(f_rows, b_rows)


def _attn_kernel(q_ref, k_ref, v_ref, c_ref, o_ref):
    T = T_ATT
    qi = pl.program_id(2)
    row = lax.broadcasted_iota(I32, (T, T), 0)
    col = lax.broadcasted_iota(I32, (T, T), 1)
    causal = col <= row
    for hh in range(2):
        q = q_ref[0, hh]

        def block(j, carry, masked, hh=hh, q=q):
            m, l, acc = carry
            start = pl.multiple_of(j * T, T)
            k = k_ref[0, hh, pl.ds(start, T), :]
            v = v_ref[0, hh, pl.ds(start, T), :]
            s = lax.dot_general(q, k, (((1,), (1,)), ((), ())), preferred_element_type=F32)
            s = s - c_ref[0, hh, :, pl.ds(start, T)]
            if masked:
                s = jnp.where(causal, s, -jnp.inf)
            m_new = jnp.maximum(m, jnp.max(s, axis=-1, keepdims=True))
            p = jnp.exp(s - m_new)
            a = jnp.exp(m - m_new)
            l = a * l + jnp.sum(p, axis=-1, keepdims=True)
            acc = a * acc + jnp.dot(p.astype(BF16), v, preferred_element_type=F32)
            return m_new, l, acc

        init = (jnp.full((T, 1), -jnp.inf, F32), jnp.zeros((T, 1), F32), jnp.zeros((T, ATT_HEAD_DIM), F32))
        carry = lax.fori_loop(0, qi, functools.partial(block, masked=False), init)
        m, l, acc = block(qi, carry, True)
        o_ref[0, :, hh * ATT_HEAD_DIM:(hh + 1) * ATT_HEAD_DIM] = (acc / l).astype(BF16)


def _attention(q, k, v, c, B, S):
    nq = S // T_ATT
    qspec = pl.BlockSpec((1, 2, T_ATT, ATT_HEAD_DIM), lambda b, hp, i: (b, hp, i, 0))
    kvspec = pl.BlockSpec((1, 2, S, ATT_HEAD_DIM), lambda b, hp, i: (b, hp, 0, 0))
    return pl.pallas_call(
        _attn_kernel,
        grid=(B, ATT_HEADS // 2, nq),
        in_specs=[qspec, kvspec, kvspec,
                  pl.BlockSpec((1, 2, 1, S), lambda b, hp, i: (b, hp, 0, 0))],
        out_specs=pl.BlockSpec((1, T_ATT, 2 * ATT_HEAD_DIM), lambda b, hp, i: (b, i, hp)),
        out_shape=jax.ShapeDtypeStruct((B, S, ATT_WIDTH), BF16),
        compiler_params=_cparams(("parallel", "parallel", "arbitrary")),
        name="fox_attention",
    )(q, k, v, c)


def _pool_kernel(u_ref, w_ref, sc_ref, o_ref, pad_ref):
    S = u_ref.shape[1]
    R = TM_POOL
    pad_ref[0:MAX_WINDOW, :] = jnp.zeros((MAX_WINDOW, POOL_WIDTH), F32)
    pad_ref[MAX_WINDOW:, :] = u_ref[0]
    lane = lax.broadcasted_iota(I32, (R, POOL_WIDTH), 1)
    trow = lax.broadcasted_iota(I32, (R, POOL_WIDTH), 0)
    grp = lane // POOL_GROUP_DIM
    win = jnp.where(grp == 0, POOL_WINDOWS[0],
                    jnp.where(grp == 1, POOL_WINDOWS[1], jnp.where(grp == 2, POOL_WINDOWS[2], POOL_WINDOWS[3])))
    for i in range(S // R):
        base = MAX_WINDOW + i * R
        u0 = pad_ref[base:base + R, :]
        acc = u0
        sums = {}
        for kk in range(1, MAX_WINDOW):
            acc = acc + pad_ref[base - kk:base - kk + R, :]
            if kk + 1 in POOL_WINDOWS:
                sums[kk + 1] = acc
        total = jnp.where(grp == 0, sums[POOL_WINDOWS[0]],
                          jnp.where(grp == 1, sums[POOL_WINDOWS[1]],
                                    jnp.where(grp == 2, sums[POOL_WINDOWS[2]], sums[POOL_WINDOWS[3]])))
        cnt = jnp.minimum(trow + (i * R + 1), win).astype(F32)
        mixed = total / cnt - u0
        y = jnp.dot(mixed.astype(BF16), w_ref[...], preferred_element_type=F32) * sc_ref[...]
        o_ref[0, i * R:(i + 1) * R, :] = y.astype(BF16)


def _pool(u3, w_bd, scale):
    B, S, _ = u3.shape
    return pl.pallas_call(
        _pool_kernel,
        grid=(B,),
        in_specs=[pl.BlockSpec((1, S, POOL_WIDTH), lambda b: (b, 0, 0)),
                  pl.BlockSpec((POOL_WIDTH, POOL_WIDTH), lambda b: (0, 0)),
                  pl.BlockSpec((1, POOL_WIDTH), lambda b: (0, 0))],
        out_specs=pl.BlockSpec((1, S, POOL_WIDTH), lambda b: (b, 0, 0)),
        out_shape=jax.ShapeDtypeStruct((B, S, POOL_WIDTH), BF16),
        scratch_shapes=[pltpu.VMEM((S + MAX_WINDOW, POOL_WIDTH), F32)],
        compiler_params=_cparams(("parallel",)),
        name="multiscale_pool",
    )(u3, w_bd, scale)


def _ssm_prep_kernel(lr_ref, li_ref, ldt_ref, brT_ref, biT_ref, ar_ref, ai_ref, bbr_ref, bbi_ref):
    lr = lr_ref[...]
    li = li_ref[...]
    dt = jnp.exp(ldt_ref[...])
    mag = jnp.exp(lr * dt)
    ar = mag * jnp.cos(li * dt)
    ai = mag * jnp.sin(li * dt)
    den = lr * lr + li * li
    nr = ar - 1.0
    zr = (nr * lr + ai * li) / den
    zi = (ai * lr - nr * li) / den
    ar_ref[...] = ar
    ai_ref[...] = ai
    br = brT_ref[...]
    bi = biT_ref[...]
    bbr_ref[...] = zr[:, None, :] * br - zi[:, None, :] * bi
    bbi_ref[...] = zr[:, None, :] * bi + zi[:, None, :] * br


def _ssm_prep(lr, li, log_dt, b_re, b_im):
    G, P, H = b_re.shape
    gp = jax.ShapeDtypeStruct((G, P), F32)
    ghp = jax.ShapeDtypeStruct((G, H, P), F32)
    return pl.pallas_call(
        _ssm_prep_kernel,
        out_shape=[gp, gp, ghp, ghp],
        name="s5_discretise",
    )(lr, li, log_dt.reshape(G, 1), b_re.transpose(0, 2, 1), b_im.transpose(0, 2, 1))


def _gelu_tanh(y):
    return 0.5 * y * (1.0 + jnp.tanh(GELU_C * (y + 0.044715 * (y * y * y))))


def _ssm_kernel(u_ref, bb_ref, ar_ref, ai_ref, cc_ref, d_ref, wg_ref, bg_ref, o_ref, x_scr, hr_scr, hi_scr):
    nb = hr_scr.shape[0]
    tt = u_ref.shape[0] // nb

    @pl.when(pl.program_id(0) == 0)
    def _():
        hr_scr[...] = jnp.zeros_like(hr_scr)
        hi_scr[...] = jnp.zeros_like(hi_scr)

    u = u_ref[...]
    x_scr[...] = jnp.dot(u.astype(BF16), bb_ref[...], preferred_element_type=F32)

    def step(t, carry):
        hr, hi = carry
        r0 = pl.multiple_of(t * nb, nb)
        xr = x_scr[pl.ds(r0, nb), 0:SSM_STATES]
        xi = x_scr[pl.ds(r0, nb), SSM_STATES:2 * SSM_STATES]
        ar = ar_ref[...]
        ai = ai_ref[...]
        nr = ar * hr - ai * hi + xr
        ni = ar * hi + ai * hr + xi
        x_scr[pl.ds(r0, nb), 0:SSM_STATES] = nr
        x_scr[pl.ds(r0, nb), SSM_STATES:2 * SSM_STATES] = ni
        return nr, ni

    hr, hi = lax.fori_loop(0, tt, step, (hr_scr[...], hi_scr[...]))
    hr_scr[...] = hr
    hi_scr[...] = hi
    y = jnp.dot(x_scr[...].astype(BF16), cc_ref[...], preferred_element_type=F32) + d_ref[...] * u
    y = _gelu_tanh(y)
    gl = jnp.dot(y.astype(BF16), wg_ref[...], preferred_element_type=F32) + bg_ref[...]
    o_ref[...] = (y * _sigmoid(gl)).astype(BF16)


def _ssm(u_tm, bblk, ar_b, ai_b, cblk, dvec, wglu, bglu, B, S):
    rows = TT_SSM * B
    const = lambda shape: pl.BlockSpec(shape, lambda i: (0, 0))
    return pl.pallas_call(
        _ssm_kernel,
        grid=(S // TT_SSM,),
        in_specs=[pl.BlockSpec((rows, SSM_WIDTH), lambda i: (i, 0)),
                  const((SSM_WIDTH, 2 * SSM_STATES)),
                  const((B, SSM_STATES)), const((B, SSM_STATES)),
                  const((2 * SSM_STATES, SSM_WIDTH)),
                  const((1, SSM_WIDTH)), const((SSM_WIDTH, SSM_WIDTH)), const((1, SSM_WIDTH))],
        out_specs=pl.BlockSpec((rows, SSM_WIDTH), lambda i: (i, 0)),
        out_shape=jax.ShapeDtypeStruct((S * B, SSM_WIDTH), BF16),
        scratch_shapes=[pltpu.VMEM((rows, 2 * SSM_STATES), F32),
                        pltpu.VMEM((B, SSM_STATES), F32), pltpu.VMEM((B, SSM_STATES), F32)],
        compiler_params=_cparams(("arbitrary",)),
        name="s5_scan",
    )(u_tm, bblk, ar_b, ai_b, cblk, dvec, wglu, bglu)


def _merge_kernel(x_ref, ya_ref, yb_ref, yc_ref, wg_ref, wa_ref, wb_ref, wc_ref, wo_ref, g_ref, b_ref, o_ref):
    x = x_ref[...]
    xb = x.astype(BF16)
    merged = None
    for i, (y_ref, w_ref) in enumerate(((ya_ref, wa_ref), (yb_ref, wb_ref), (yc_ref, wc_ref))):
        gate = _sigmoid(jnp.dot(xb, wg_ref[:, i * D_MODEL:(i + 1) * D_MODEL], preferred_element_type=F32))
        term = gate * jnp.dot(y_ref[...], w_ref[...], preferred_element_type=F32)
        merged = term if merged is None else merged + term
    mix = jnp.dot(merged.astype(BF16), wo_ref[...], preferred_element_type=F32)
    o_ref[...] = _layer_norm(DEEPNORM_ALPHA * x + mix, g_ref[...], b_ref[...])


def _merge(x2, ya, yb, yc_tm, wg, wa, wb, wc, wo, g, b, B, S):
    N = B * S
    nS = S // TM_MERGE
    const = lambda shape: pl.BlockSpec(shape, lambda bb, s: (0, 0))
    row = lambda w: pl.BlockSpec((TM_MERGE, w), lambda bb, s: (bb * nS + s, 0))
    return pl.pallas_call(
        _merge_kernel,
        grid=(B, nS),
        in_specs=[row(D_MODEL), row(ATT_WIDTH), row(POOL_WIDTH),
                  pl.BlockSpec((TM_MERGE, SSM_WIDTH), lambda bb, s: (s, bb)),
                  const((D_MODEL, N_BRANCH * D_MODEL)), const((ATT_WIDTH, D_MODEL)),
                  const((POOL_WIDTH, D_MODEL)), const((SSM_WIDTH, D_MODEL)), const((D_MODEL, D_MODEL)),
                  const((1, D_MODEL)), const((1, D_MODEL))],
        out_specs=row(D_MODEL),
        out_shape=jax.ShapeDtypeStruct((N, D_MODEL), F32),
        compiler_params=_cparams(("parallel", "parallel")),
        name="merge_ln1",
    )(x2, ya, yb, yc_tm, wg, wa, wb, wc, wo, g, b)


def _route_kernel(x_ref, wr_ref, br_ref, idx_ref, gate_ref, rank_ref, cnt_ref, base_scr):
    tm = x_ref.shape[0]

    @pl.when(pl.program_id(0) == 0)
    def _():
        base_scr[...] = jnp.zeros_like(base_scr)

    logits = jnp.dot(x_ref[...], wr_ref[...], preferred_element_type=F32,
                     precision=lax.Precision.HIGHEST) + br_ref[...]
    lane = lax.broadcasted_iota(I32, (tm, LANES), 1)
    lane_f = lane.astype(F32)
    work = logits
    vals, idxs = [], []
    multihot = jnp.zeros((tm, LANES), F32)
    for _ in range(TOP_K):
        mx = jnp.max(work, axis=-1, keepdims=True)
        ix = jnp.min(jnp.where(work == mx, lane_f, float(LANES)), axis=-1, keepdims=True)
        sel = lane_f == ix
        work = jnp.where(sel, -jnp.inf, work)
        multihot = multihot + sel.astype(F32)
        vals.append(mx)
        idxs.append(ix)
    exps = [jnp.exp(v - vals[0]) for v in vals]
    denom = exps[0] + exps[1] + exps[2] + exps[3]
    r = lax.broadcasted_iota(I32, (tm, tm), 0)
    c = lax.broadcasted_iota(I32, (tm, tm), 1)
    lower = (c < r).astype(BF16)
    before = jnp.dot(lower, multihot.astype(BF16), preferred_element_type=F32) + base_scr[...]
    idx_out = jnp.zeros((tm, LANES), F32)
    gate_out = jnp.zeros((tm, LANES), F32)
    rank_out = jnp.zeros((tm, LANES), F32)
    for kk in range(TOP_K):
        rk = jnp.sum(jnp.where(lane_f == idxs[kk], before, 0.0), axis=-1, keepdims=True)
        here = lane == kk
        idx_out = jnp.where(here, idxs[kk], idx_out)
        gate_out = jnp.where(here, exps[kk] / denom, gate_out)
        rank_out = jnp.where(here, rk, rank_out)
    idx_ref[...] = idx_out.astype(I32)
    gate_ref[...] = gate_out
    rank_ref[...] = rank_out.astype(I32)
    base_new = base_scr[...] + jnp.sum(multihot, axis=0, keepdims=True)
    base_scr[...] = base_new
    cnt_ref[...] = base_new.astype(I32)


def _route(x1, wr, br):
    N = x1.shape[0]
    row = pl.BlockSpec((TM_ROUTE, LANES), lambda i: (i, 0))
    return pl.pallas_call(
        _route_kernel,
        grid=(N // TM_ROUTE,),
        in_specs=[pl.BlockSpec((TM_ROUTE, D_MODEL), lambda i: (i, 0)),
                  pl.BlockSpec((D_MODEL, LANES), lambda i: (0, 0)),
                  pl.BlockSpec((1, LANES), lambda i: (0, 0))],
        out_specs=[row, row, row, pl.BlockSpec((1, LANES), lambda i: (0, 0))],
        out_shape=[jax.ShapeDtypeStruct((N, LANES), I32), jax.ShapeDtypeStruct((N, LANES), F32),
                   jax.ShapeDtypeStruct((N, LANES), I32), jax.ShapeDtypeStruct((1, LANES), I32)],
        scratch_shapes=[pltpu.VMEM((1, LANES), F32)],
        compiler_params=_cparams(("arbitrary",)),
        name="router_top4",
    )(x1, wr, br)


def _dispatch_kernel(dest_ref, x_ref, xs_hbm, pk_scr, sem):
    tm = x_ref.shape[0]
    pk_scr[...] = _pack_rows(x_ref[...])

    def row_copy(r, kk):
        d = dest_ref[0, 0, r * TOP_K + kk]
        return pltpu.make_async_copy(pk_scr.at[pl.ds(r, 1)], xs_hbm.at[pl.ds(d, 1)], sem)

    def issue(r, _):
        for kk in range(TOP_K):
            row_copy(r, kk).start()
        return 0

    def drain(r, _):
        for kk in range(TOP_K):
            row_copy(r, kk).wait()
        return 0

    lax.fori_loop(0, tm, issue, 0)
    lax.fori_loop(0, tm, drain, 0)


def _dispatch(dest3, x1):
    N = x1.shape[0]
    M = N * TOP_K
    return pl.pallas_call(
        _dispatch_kernel,
        grid=(N // TM_ROW,),
        in_specs=[pl.BlockSpec((1, 1, TM_ROW * TOP_K), lambda i: (i, 0, 0), memory_space=pltpu.SMEM),
                  pl.BlockSpec((TM_ROW, D_MODEL), lambda i: (i, 0))],
        out_specs=pl.BlockSpec(memory_space=pl.ANY),
        out_shape=jax.ShapeDtypeStruct((M, HALF_D), U32),
        scratch_shapes=[pltpu.VMEM((TM_ROW, HALF_D), U32), pltpu.SemaphoreType.DMA],
        compiler_params=_cparams(("arbitrary",), has_side_effects=True),
        name="moe_dispatch",
    )(dest3, x1)


def _expert_kernel(blk_ref, exp_ref, lo_ref, hi_ref, xs_ref, wu_ref, bu_ref, wd_ref, bd_ref, o_ref):
    w = pl.program_id(0)
    lo = lo_ref[w]
    hi = hi_ref[w]

    @pl.when(hi > lo)
    def _():
        xh, xl = _unpack_rows(xs_ref[...])
        x = jnp.concatenate([xh.astype(BF16), xl.astype(BF16)], axis=1)
        rows = x.shape[0]
        even = (lax.broadcasted_iota(I32, (rows, LANES), 1) & 1) == 0
        halves = []
        for half, shift in ((0, LANES - 1), (1, 1)):
            chunks = []
            hu_half = (jnp.dot(x, wu_ref[0, :, half * D_FF:(half + 1) * D_FF], preferred_element_type=F32)
                       + bu_ref[0, :, half * D_FF:(half + 1) * D_FF])
            for cc in range(D_FF // LANES):
                hu = hu_half[:, cc * LANES:(cc + 1) * LANES]
                g = jnp.minimum(hu, SWIGLU_LIMIT)
                p = jnp.where(even, g * _sigmoid(SWIGLU_ALPHA * g), jnp.clip(hu, -SWIGLU_LIMIT, SWIGLU_LIMIT) + 1.0)
                chunks.append(p * pltpu.roll(p, shift, 1))
            halves.append(chunks)
        act = jnp.concatenate([jnp.where(even, a, b) for a, b in zip(*halves)], axis=1)
        y = jnp.dot(act.astype(BF16), wd_ref[0], preferred_element_type=F32) + bd_ref[0]
        packed = _pack_rows(y)

        @pl.when(lo == 0)
        def _():
            o_ref[...] = packed

        @pl.when(lo > 0)
        def _():
            row = lax.broadcasted_iota(I32, packed.shape, 0)
            o_ref[...] = jnp.where((row >= lo) & (row < hi), packed, o_ref[...])


def _experts(meta, xs, wu, bu, wd, bd):
    M = xs.shape[0]
    W = meta[0].shape[0]
    by_expert = lambda *shape: pl.BlockSpec((1,) + shape, lambda w, blk, ex, lo, hi: (ex[w], 0, 0))
    rows = pl.BlockSpec((BM_EXP, HALF_D), lambda w, blk, ex, lo, hi: (blk[w], 0))
    return pl.pallas_call(
        _expert_kernel,
        grid_spec=pltpu.PrefetchScalarGridSpec(
            num_scalar_prefetch=4,
            grid=(W,),
            in_specs=[rows, by_expert(D_MODEL, 2 * D_FF), by_expert(1, 2 * D_FF),
                      by_expert(D_FF, D_MODEL), by_expert(1, D_MODEL)],
            out_specs=rows,
        ),
        out_shape=jax.ShapeDtypeStruct((M, HALF_D), U32),
        compiler_params=_cparams(("arbitrary",)),
        name="moe_experts",
    )(*meta, xs, wu, bu, wd, bd)


def _combine_kernel(dest_ref, x_ref, gate_ref, g_ref, b_ref, yb_hbm, o_ref, buf, sem):
    tm = x_ref.shape[0]

    def row_copy(r, kk):
        d = dest_ref[0, 0, r * TOP_K + kk]
        return pltpu.make_async_copy(yb_hbm.at[pl.ds(d, 1)], buf.at[kk, pl.ds(r, 1)], sem)

    def issue(r, _):
        for kk in range(TOP_K):
            row_copy(r, kk).start()
        return 0

    def drain(r, _):
        for kk in range(TOP_K):
            row_copy(r, kk).wait()
        return 0

    lax.fori_loop(0, tm, issue, 0)
    lax.fori_loop(0, tm, drain, 0)
    gates = gate_ref[...]
    yh = jnp.zeros((tm, HALF_D), F32)
    yl = jnp.zeros((tm, HALF_D), F32)
    for kk in range(TOP_K):
        hi, lo = _unpack_rows(buf[kk])
        gk = gates[:, kk:kk + 1]
        yh = yh + gk * hi
        yl = yl + gk * lo
    z = DEEPNORM_ALPHA * x_ref[...] + jnp.concatenate([yh, yl], axis=1)
    o_ref[...] = _layer_norm(z, g_ref[...], b_ref[...])


def _combine(dest3, x1, gate, g, b, yb):
    N = x1.shape[0]
    return pl.pallas_call(
        _combine_kernel,
        grid=(N // TM_ROW,),
        in_specs=[pl.BlockSpec((1, 1, TM_ROW * TOP_K), lambda i: (i, 0, 0), memory_space=pltpu.SMEM),
                  pl.BlockSpec((TM_ROW, D_MODEL), lambda i: (i, 0)),
                  pl.BlockSpec((TM_ROW, LANES), lambda i: (i, 0)),
                  pl.BlockSpec((1, D_MODEL), lambda i: (0, 0)),
                  pl.BlockSpec((1, D_MODEL), lambda i: (0, 0)),
                  pl.BlockSpec(memory_space=pl.ANY)],
        out_specs=pl.BlockSpec((TM_ROW, D_MODEL), lambda i: (i, 0)),
        out_shape=jax.ShapeDtypeStruct((N, D_MODEL), F32),
        scratch_shapes=[pltpu.VMEM((TOP_K, TM_ROW, HALF_D), U32), pltpu.SemaphoreType.DMA],
        compiler_params=_cparams(("arbitrary",)),
        name="moe_combine_ln2",
    )(dest3, x1, gate, g, b, yb)


def _work_items(counts, M):
    nblk = M // BM_EXP
    W = nblk + N_EXPERTS
    ends = jnp.cumsum(counts)
    starts = ends - counts
    first = starts // BM_EXP
    last = jnp.maximum(ends - 1, 0) // BM_EXP
    n_items = jnp.where(counts > 0, last - first + 1, 0)
    item_end = jnp.cumsum(n_items)
    item_start = item_end - n_items
    total = item_end[-1]
    w = jnp.arange(W, dtype=I32)
    w_eff = jnp.minimum(w, total - 1)
    owner = (item_end[None, :] <= w_eff[:, None]).sum(axis=1).astype(I32)
    sel = owner[:, None] == jnp.arange(N_EXPERTS, dtype=I32)[None, :]
    pick = lambda table: jnp.sum(jnp.where(sel, table[None, :], 0), axis=1)
    blk = pick(first) + (w_eff - pick(item_start))
    lo = jnp.clip(pick(starts) - blk * BM_EXP, 0, BM_EXP)
    hi = jnp.clip(pick(ends) - blk * BM_EXP, 0, BM_EXP)
    valid = w < total
    lo = jnp.where(valid, lo, 0)
    hi = jnp.where(valid, hi, 0)
    return (blk.astype(I32), owner, lo.astype(I32), hi.astype(I32)), starts


def _block_diag(blocks):
    G, a, b = blocks.shape
    eye = jnp.eye(G, dtype=blocks.dtype)
    return jnp.einsum("gab,gh->gahb", blocks, eye).reshape(G * a, G * b)


def kernel(x, w_in, b_forget, w_pool, pool_scale, ssm_lambda_re, ssm_lambda_im, ssm_log_dt, ssm_b_re, ssm_b_im, ssm_c_re, ssm_c_im, ssm_d, w_glu, b_glu, w_branch_a, w_branch_b, w_branch_c, w_out, ln1_g, ln1_b, w_router, b_router, w_up, b_up, w_down, b_down, ln2_g, ln2_b):
    B, S, D = x.shape
    assert D == D_MODEL and S % T_ATT == 0 and S % TM_PROJ == 0 and (B * S) % TM_ROUTE == 0
    N = B * S
    M = N * TOP_K
    x2 = x.reshape(N, D)
    scale = ATT_HEAD_DIM ** -0.5
    for l in range(DEPTH):
        wl = w_in[l]
        wq = wl[:, :ATT_WIDTH] * scale
        wqkv = jnp.concatenate([wq, wl[:, ATT_WIDTH:QKV_W]], axis=1).astype(BF16)
        c0 = QKV_W
        wf = wl[:, c0:c0 + ATT_HEADS]
        wsmall = jnp.concatenate(
            [wl[:, c0 + ATT_HEADS:c0 + ATT_HEADS + POOL_WIDTH + SSM_WIDTH], wf,
             jnp.zeros((D, F_PAD - ATT_HEADS), F32)], axis=1).astype(BF16)
        wgates = wl[:, c0 + ATT_HEADS + POOL_WIDTH + SSM_WIDTH:].astype(BF16)

        q, k, v, u_pool, u_ssm_tm, f_pad = _proj(x2, wqkv, wsmall, B, S)

        f_rows = f_pad[:, :ATT_HEADS].reshape(B, S, ATT_HEADS).transpose(0, 2, 1).reshape(B * ATT_HEADS, S)
        b_rows = jnp.tile(b_forget[l], B).reshape(B * ATT_HEADS, 1)
        c = _fcum(f_rows, b_rows).reshape(B, ATT_HEADS, 1, S)
        y_a = _attention(q, k, v, c, B, S).reshape(N, ATT_WIDTH)

        w_pool_bd = _block_diag(w_pool[l]).astype(BF16)
        y_b = _pool(u_pool.reshape(B, S, POOL_WIDTH), w_pool_bd, pool_scale[l].reshape(1, POOL_WIDTH)).reshape(N, POOL_WIDTH)

        ar, ai, bbrT, bbiT = _ssm_prep(ssm_lambda_re[l], ssm_lambda_im[l], ssm_log_dt[l], ssm_b_re[l], ssm_b_im[l])
        bblk = jnp.concatenate([_block_diag(bbrT), _block_diag(bbiT)], axis=1).astype(BF16)
        cblk = jnp.concatenate([_block_diag(ssm_c_re[l].transpose(0, 2, 1)),
                                -_block_diag(ssm_c_im[l].transpose(0, 2, 1))], axis=0).astype(BF16)
        ar_b = jnp.broadcast_to(ar.reshape(1, SSM_STATES), (B, SSM_STATES))
        ai_b = jnp.broadcast_to(ai.reshape(1, SSM_STATES), (B, SSM_STATES))
        y_c_tm = _ssm(u_ssm_tm.reshape(S * B, SSM_WIDTH), bblk, ar_b, ai_b, cblk,
                      ssm_d[l].reshape(1, SSM_WIDTH), w_glu[l].astype(BF16), b_glu[l].reshape(1, SSM_WIDTH), B, S)

        x1 = _merge(x2, y_a, y_b, y_c_tm.reshape(S, B * SSM_WIDTH), wgates,
                    w_branch_a[l].astype(BF16), w_branch_b[l].astype(BF16), w_branch_c[l].astype(BF16),
                    w_out[l].astype(BF16), ln1_g[l].reshape(1, D), ln1_b[l].reshape(1, D), B, S)

        wr = jnp.concatenate([w_router[l], jnp.zeros((D, LANES - N_EXPERTS), F32)], axis=1)
        br = jnp.concatenate([b_router[l], jnp.full((LANES - N_EXPERTS,), _NEG_BIG, F32)]).reshape(1, LANES)
        idx, gate, rank, cnt = _route(x1, wr, br)
        counts = cnt[0, :N_EXPERTS]
        meta, starts = _work_items(counts, M)
        top_idx = idx[:, :TOP_K]
        onehot = top_idx[:, :, None] == jnp.arange(N_EXPERTS, dtype=I32)[None, None, :]
        dest = rank[:, :TOP_K] + jnp.sum(jnp.where(onehot, starts[None, None, :], 0), axis=-1)
        dest3 = dest.astype(I32).reshape(N // TM_ROW, 1, TM_ROW * TOP_K)
        xs = _dispatch(dest3, x1)
        wd = w_down[l].reshape(N_EXPERTS, 2, D_FF // 2, D).transpose(0, 2, 1, 3).reshape(N_EXPERTS, D_FF, D)
        yb = _experts(meta, xs, w_up[l].astype(BF16), b_up[l][:, None, :],
                      wd.astype(BF16), b_down[l][:, None, :])
        x2 = _combine(dest3, x1, gate, ln2_g[l].reshape(1, D), ln2_b[l].reshape(1, D), yb)
    return x2.reshape(B, S, D)
```

```python
import functools
import math

import jax
import jax.numpy as jnp
import numpy as np
from jax import lax
from jax.experimental import pallas as pl
from jax.experimental.pallas import tpu as pltpu

F32 = jnp.float32
BF16 = jnp.bfloat16
I32 = jnp.int32
U32 = jnp.uint32

D_MODEL = 1024
DEPTH = 4
ATT_HEADS = 8
ATT_HEAD_DIM = 64
ATT_WIDTH = ATT_HEADS * ATT_HEAD_DIM
POOL_WINDOWS = (2, 4, 8, 16)
POOL_GROUPS = 4
POOL_WIDTH = 256
POOL_GROUP_DIM = 64
MAX_WINDOW = max(POOL_WINDOWS)
SSM_WIDTH = 256
SSM_GROUP_DIM = 16
SSM_GROUPS = 16
SSM_STATE = 64
SSM_STATES = SSM_GROUPS * SSM_STATE
N_BRANCH = 3
N_EXPERTS = 32
TOP_K = 4
D_FF = D_MODEL
SWIGLU_LIMIT = 7.0
SWIGLU_ALPHA = 1.702
LN_EPS = 1e-5
DEEPNORM_ALPHA = (2.0 * DEPTH) ** 0.25
GELU_C = math.sqrt(2.0 / math.pi)

LANES = 128
HALF_D = D_MODEL // 2
F_PAD = LANES
V_PAD = LANES
SMALL_W = POOL_WIDTH + SSM_WIDTH + F_PAD
QKV_W = 3 * ATT_WIDTH
TM_PROJ = 512
T_ATT = 512
TM_POOL = 256
TT_SSM = 64
TM_MERGE = 256
SUBLANES = 8
TM_ROW = 256
ROWS_TILE = TM_ROW * TOP_K + N_EXPERTS * SUBLANES
GROUPS_TILE = ROWS_TILE // SUBLANES
TAB_W = 2 * LANES
assert TAB_W >= GROUPS_TILE + 2
BM_EXP = 256
VMEM_LIMIT = 52 * 1024 * 1024

_NEG_BIG = -1e30


def _cparams(sem, **kw):
    return pltpu.CompilerParams(dimension_semantics=sem, vmem_limit_bytes=VMEM_LIMIT, **kw)


def _sigmoid(x):
    return 1.0 / (1.0 + jnp.exp(-x))


def _layer_norm(z, g, b):
    mu = jnp.mean(z, axis=-1, keepdims=True)
    zc = z - mu
    var = jnp.mean(zc * zc, axis=-1, keepdims=True)
    return zc * lax.rsqrt(var + LN_EPS) * g + b


def _pack_rows(y):
    u = pltpu.bitcast(y.astype(BF16).astype(F32), U32)
    return u[:, :HALF_D] | (u[:, HALF_D:] >> 16)


def _unpack_rows(p):
    hi = pltpu.bitcast(p & jnp.uint32(0xFFFF0000), F32)
    lo = pltpu.bitcast(p << 16, F32)
    return hi, lo


def _proj_kernel(x_ref, wqkv_ref, ws_ref, q_ref, k_ref, v_ref, up_ref, us_ref, f_ref):
    xb = x_ref[...].astype(BF16)
    for part, ref in enumerate((q_ref, k_ref)):
        h = jnp.dot(xb, wqkv_ref[:, part * ATT_WIDTH:(part + 1) * ATT_WIDTH], preferred_element_type=F32)
        for hh in range(ATT_HEADS):
            ref[0, hh] = h[:, hh * ATT_HEAD_DIM:(hh + 1) * ATT_HEAD_DIM].astype(BF16)
    h = jnp.dot(xb, wqkv_ref[:, 2 * ATT_WIDTH:], preferred_element_type=F32)
    lane = lax.broadcasted_iota(I32, (h.shape[0], V_PAD), 1)
    tail = jnp.where(lane == ATT_HEAD_DIM, 1.0, 0.0)
    for pair in range(ATT_HEADS // 2):
        slab = h[:, pair * V_PAD:(pair + 1) * V_PAD]
        v_ref[0, 2 * pair] = jnp.where(lane < ATT_HEAD_DIM, slab, tail).astype(BF16)
        v_ref[0, 2 * pair + 1] = jnp.where(lane < ATT_HEAD_DIM, pltpu.roll(slab, ATT_HEAD_DIM, 1), tail).astype(BF16)
    hs = jnp.dot(xb, ws_ref[...], preferred_element_type=F32)
    up_ref[...] = hs[:, :POOL_WIDTH]
    us_ref[...] = hs[:, POOL_WIDTH:POOL_WIDTH + SSM_WIDTH]
    f_ref[...] = hs[:, POOL_WIDTH + SSM_WIDTH:]


def _proj(x2, wqkv, wsmall, B, S):
    N = B * S
    nS = S // TM_PROJ
    hm = jax.ShapeDtypeStruct((B, ATT_HEADS, S, ATT_HEAD_DIM), BF16)
    hm_spec = pl.BlockSpec((1, ATT_HEADS, TM_PROJ, ATT_HEAD_DIM), lambda b, s: (b, 0, s, 0))
    hv = jax.ShapeDtypeStruct((B, ATT_HEADS, S, V_PAD), BF16)
    hv_spec = pl.BlockSpec((1, ATT_HEADS, TM_PROJ, V_PAD), lambda b, s: (b, 0, s, 0))
    return pl.pallas_call(
        _proj_kernel,
        grid=(B, nS),
        in_specs=[
            pl.BlockSpec((TM_PROJ, D_MODEL), lambda b, s: (b * nS + s, 0)),
            pl.BlockSpec((D_MODEL, QKV_W), lambda b, s: (0, 0)),
            pl.BlockSpec((D_MODEL, SMALL_W), lambda b, s: (0, 0)),
        ],
        out_specs=[
            hm_spec, hm_spec, hv_spec,
            pl.BlockSpec((TM_PROJ, POOL_WIDTH), lambda b, s: (b * nS + s, 0)),
            pl.BlockSpec((TM_PROJ, SSM_WIDTH), lambda b, s: (s, b)),
            pl.BlockSpec((TM_PROJ, F_PAD), lambda b, s: (b * nS + s, 0)),
        ],
        out_shape=[
            hm, hm, hv,
            jax.ShapeDtypeStruct((N, POOL_WIDTH), F32),
            jax.ShapeDtypeStruct((S, B * SSM_WIDTH), F32),
            jax.ShapeDtypeStruct((N, F_PAD), F32),
        ],
        compiler_params=_cparams(("parallel", "parallel")),
        name="in_proj",
    )(x2, wqkv, wsmall)


def _fcum_kernel(f_ref, b_ref, c_ref):
    rows, S = f_ref.shape
    lane = lax.broadcasted_iota(I32, (rows, LANES), 1)
    carry = jnp.zeros((rows, 1), F32)
    for ch in range(S // LANES):
        z = f_ref[:, ch * LANES:(ch + 1) * LANES] + b_ref[...]
        lf = jnp.minimum(z, 0.0) - jnp.log1p(jnp.exp(-jnp.abs(z)))
        sh = 1
        while sh < LANES:
            lf = lf + jnp.where(lane >= sh, pltpu.roll(lf, sh, 1), 0.0)
            sh *= 2
        lf = lf + carry
        c_ref[:, ch * LANES:(ch + 1) * LANES] = lf
        carry = lf[:, LANES - 1:LANES]


def _fcum(f_rows, b_rows):
    rows, S = f_rows.shape
    return pl.pallas_call(
        _fcum_kernel,
        out_shape=jax.ShapeDtypeStruct((rows, S), F32),
        compiler_params=pltpu.CompilerParams(vmem_limit_bytes=VMEM_LIMIT),
        name="forget_cumsum",
    )(f_rows, b_rows)


def _attn_kernel(q_ref, k_ref, v_ref, c_ref, o_ref):
    T = T_ATT
    qi = pl.program_id(2)
    row = lax.broadcasted_iota(I32, (T, T), 0)
    col = lax.broadcasted_iota(I32, (T, T), 1)
    causal = col <= row
    qs = [q_ref[0, hh] for hh in range(2)]

    def block(j, carry, masked):
        start = pl.multiple_of(j * T, T)
        out = []
        for hh in range(2):
            m, acc = carry[2 * hh], carry[2 * hh + 1]
            k = k_ref[0, hh, pl.ds(start, T), :]
            v = v_ref[0, hh, pl.ds(start, T), :]
            s = lax.dot_general(qs[hh], k, (((1,), (1,)), ((), ())), preferred_element_type=F32)
            s = s - c_ref[0, hh, :, pl.ds(start, T)]
            if masked:
                s = jnp.where(causal, s, -jnp.inf)
            m_new = jnp.maximum(m, jnp.max(s, axis=-1, keepdims=True))
            p = jnp.exp(s - m_new)
            acc = jnp.exp(m - m_new) * acc + jnp.dot(p.astype(BF16), v, preferred_element_type=F32)
            out += [m_new, acc]
        return tuple(out)

    m0 = jnp.full((T, 1), -jnp.inf, F32)
    acc0 = jnp.zeros((T, V_PAD), F32)
    carry = lax.fori_loop(0, qi, functools.partial(block, masked=False), (m0, acc0, m0, acc0))
    carry = block(qi, carry, True)
    for hh in range(2):
        acc = carry[2 * hh + 1]
        out = acc[:, :ATT_HEAD_DIM] / acc[:, ATT_HEAD_DIM:ATT_HEAD_DIM + 1]
        o_ref[0, :, hh * ATT_HEAD_DIM:(hh + 1) * ATT_HEAD_DIM] = out.astype(BF16)


def _attention(q, k, v, c, B, S):
    nq = S // T_ATT
    qspec = pl.BlockSpec((1, 2, T_ATT, ATT_HEAD_DIM), lambda b, hp, i: (b, hp, i, 0))
    kspec = pl.BlockSpec((1, 2, S, ATT_HEAD_DIM), lambda b, hp, i: (b, hp, 0, 0))
    vspec = pl.BlockSpec((1, 2, S, V_PAD), lambda b, hp, i: (b, hp, 0, 0))
    return pl.pallas_call(
        _attn_kernel,
        grid=(B, ATT_HEADS // 2, nq),
        in_specs=[qspec, kspec, vspec,
                  pl.BlockSpec((1, 2, 1, S), lambda b, hp, i: (b, hp, 0, 0))],
        out_specs=pl.BlockSpec((1, T_ATT, 2 * ATT_HEAD_DIM), lambda b, hp, i: (b, i, hp)),
        out_shape=jax.ShapeDtypeStruct((B, S, ATT_WIDTH), BF16),
        compiler_params=_cparams(("parallel", "parallel", "arbitrary")),
        name="fox_attention",
    )(q, k, v, c)


def _pool_kernel(u_ref, w_ref, sc_ref, o_ref, pad_ref):
    S = u_ref.shape[1]
    R = TM_POOL
    pad_ref[0:MAX_WINDOW, :] = jnp.zeros((MAX_WINDOW, POOL_WIDTH), F32)
    pad_ref[MAX_WINDOW:, :] = u_ref[0]
    lane = lax.broadcasted_iota(I32, (R, POOL_WIDTH), 1)
    trow = lax.broadcasted_iota(I32, (R, POOL_WIDTH), 0)
    grp = lane // POOL_GROUP_DIM
    win = jnp.where(grp == 0, POOL_WINDOWS[0],
                    jnp.where(grp == 1, POOL_WINDOWS[1], jnp.where(grp == 2, POOL_WINDOWS[2], POOL_WINDOWS[3])))
    for i in range(S // R):
        base = MAX_WINDOW + i * R
        u0 = pad_ref[base:base + R, :]
        acc = u0
        sums = {}
        for kk in range(1, MAX_WINDOW):
            acc = acc + pad_ref[base - kk:base - kk + R, :]
            if kk + 1 in POOL_WINDOWS:
                sums[kk + 1] = acc
        total = jnp.where(grp == 0, sums[POOL_WINDOWS[0]],
                          jnp.where(grp == 1, sums[POOL_WINDOWS[1]],
                                    jnp.where(grp == 2, sums[POOL_WINDOWS[2]], sums[POOL_WINDOWS[3]])))
        cnt = jnp.minimum(trow + (i * R + 1), win).astype(F32)
        mixed = total / cnt - u0
        y = jnp.dot(mixed.astype(BF16), w_ref[...], preferred_element_type=F32) * sc_ref[...]
        o_ref[0, i * R:(i + 1) * R, :] = y.astype(BF16)


def _pool(u3, w_bd, scale):
    B, S, _ = u3.shape
    return pl.pallas_call(
        _pool_kernel,
        grid=(B,),
        in_specs=[pl.BlockSpec((1, S, POOL_WIDTH), lambda b: (b, 0, 0)),
                  pl.BlockSpec((POOL_WIDTH, POOL_WIDTH), lambda b: (0, 0)),
                  pl.BlockSpec((1, POOL_WIDTH), lambda b: (0, 0))],
        out_specs=pl.BlockSpec((1, S, POOL_WIDTH), lambda b: (b, 0, 0)),
        out_shape=jax.ShapeDtypeStruct((B, S, POOL_WIDTH), BF16),
        scratch_shapes=[pltpu.VMEM((S + MAX_WINDOW, POOL_WIDTH), F32)],
        compiler_params=_cparams(("parallel",)),
        name="multiscale_pool",
    )(u3, w_bd, scale)


def _ssm_prep_kernel(lr_ref, li_ref, ldt_ref, brT_ref, biT_ref, ar_ref, ai_ref, bbr_ref, bbi_ref):
    lr = lr_ref[...]
    li = li_ref[...]
    dt = jnp.exp(ldt_ref[...])
    mag = jnp.exp(lr * dt)
    ar = mag * jnp.cos(li * dt)
    ai = mag * jnp.sin(li * dt)
    den = lr * lr + li * li
    nr = ar - 1.0
    zr = (nr * lr + ai * li) / den
    zi = (ai * lr - nr * li) / den
    ar_ref[...] = ar
    ai_ref[...] = ai
    br = brT_ref[...]
    bi = biT_ref[...]
    bbr_ref[...] = zr[:, None, :] * br - zi[:, None, :] * bi
    bbi_ref[...] = zr[:, None, :] * bi + zi[:, None, :] * br


def _ssm_prep(lr, li, log_dt, b_re, b_im):
    G, P, H = b_re.shape
    gp = jax.ShapeDtypeStruct((G, P), F32)
    ghp = jax.ShapeDtypeStruct((G, H, P), F32)
    return pl.pallas_call(
        _ssm_prep_kernel,
        out_shape=[gp, gp, ghp, ghp],
        name="s5_discretise",
    )(lr, li, log_dt.reshape(G, 1), b_re.transpose(0, 2, 1), b_im.transpose(0, 2, 1))


def _gelu_tanh(y):
    return 0.5 * y * (1.0 + jnp.tanh(GELU_C * (y + 0.044715 * (y * y * y))))


def _ssm_kernel(u_ref, bb_ref, ar_ref, ai_ref, cc_ref, d_ref, wg_ref, bg_ref, o_ref, x_scr, hr_scr, hi_scr):
    nb = hr_scr.shape[0]
    tt = u_ref.shape[0] // nb

    @pl.when(pl.program_id(0) == 0)
    def _():
        hr_scr[...] = jnp.zeros_like(hr_scr)
        hi_scr[...] = jnp.zeros_like(hi_scr)

    u = u_ref[...]
    x_scr[...] = jnp.dot(u.astype(BF16), bb_ref[...], preferred_element_type=F32)

    def step(t, carry):
        hr, hi = carry
        r0 = pl.multiple_of(t * nb, nb)
        xr = x_scr[pl.ds(r0, nb), 0:SSM_STATES]
        xi = x_scr[pl.ds(r0, nb), SSM_STATES:2 * SSM_STATES]
        ar = ar_ref[...]
        ai = ai_ref[...]
        nr = ar * hr - ai * hi + xr
        ni = ar * hi + ai * hr + xi
        x_scr[pl.ds(r0, nb), 0:SSM_STATES] = nr
        x_scr[pl.ds(r0, nb), SSM_STATES:2 * SSM_STATES] = ni
        return nr, ni

    hr, hi = lax.fori_loop(0, tt, step, (hr_scr[...], hi_scr[...]))
    hr_scr[...] = hr
    hi_scr[...] = hi
    y = jnp.dot(x_scr[...].astype(BF16), cc_ref[...], preferred_element_type=F32) + d_ref[...] * u
    y = _gelu_tanh(y)
    gl = jnp.dot(y.astype(BF16), wg_ref[...], preferred_element_type=F32) + bg_ref[...]
    o_ref[...] = (y * _sigmoid(gl)).astype(BF16)


def _ssm(u_tm, bblk, ar_b, ai_b, cblk, dvec, wglu, bglu, B, S):
    rows = TT_SSM * B
    const = lambda shape: pl.BlockSpec(shape, lambda i: (0, 0))
    return pl.pallas_call(
        _ssm_kernel,
        grid=(S // TT_SSM,),
        in_specs=[pl.BlockSpec((rows, SSM_WIDTH), lambda i: (i, 0)),
                  const((SSM_WIDTH, 2 * SSM_STATES)),
                  const((B, SSM_STATES)), const((B, SSM_STATES)),
                  const((2 * SSM_STATES, SSM_WIDTH)),
                  const((1, SSM_WIDTH)), const((SSM_WIDTH, SSM_WIDTH)), const((1, SSM_WIDTH))],
        out_specs=pl.BlockSpec((rows, SSM_WIDTH), lambda i: (i, 0)),
        out_shape=jax.ShapeDtypeStruct((S * B, SSM_WIDTH), BF16),
        scratch_shapes=[pltpu.VMEM((rows, 2 * SSM_STATES), F32),
                        pltpu.VMEM((B, SSM_STATES), F32), pltpu.VMEM((B, SSM_STATES), F32)],
        compiler_params=_cparams(("arbitrary",)),
        name="s5_scan",
    )(u_tm, bblk, ar_b, ai_b, cblk, dvec, wglu, bglu)


def _merge_kernel(x_ref, ya_ref, yb_ref, yc_ref, wg_ref, wa_ref, wb_ref, wc_ref, wo_ref, g_ref, b_ref, o_ref):
    x = x_ref[...]
    xb = x.astype(BF16)
    merged = None
    for i, (y_ref, w_ref) in enumerate(((ya_ref, wa_ref), (yb_ref, wb_ref), (yc_ref, wc_ref))):
        gate = _sigmoid(jnp.dot(xb, wg_ref[:, i * D_MODEL:(i + 1) * D_MODEL], preferred_element_type=F32))
        term = gate * jnp.dot(y_ref[...], w_ref[...], preferred_element_type=F32)
        merged = term if merged is None else merged + term
    mix = jnp.dot(merged.astype(BF16), wo_ref[...], preferred_element_type=F32)
    o_ref[...] = _layer_norm(DEEPNORM_ALPHA * x + mix, g_ref[...], b_ref[...])


def _merge(x2, ya, yb, yc_tm, wg, wa, wb, wc, wo, g, b, B, S):
    N = B * S
    nS = S // TM_MERGE
    const = lambda shape: pl.BlockSpec(shape, lambda bb, s: (0, 0))
    row = lambda w: pl.BlockSpec((TM_MERGE, w), lambda bb, s: (bb * nS + s, 0))
    return pl.pallas_call(
        _merge_kernel,
        grid=(B, nS),
        in_specs=[row(D_MODEL), row(ATT_WIDTH), row(POOL_WIDTH),
                  pl.BlockSpec((TM_MERGE, SSM_WIDTH), lambda bb, s: (s, bb)),
                  const((D_MODEL, N_BRANCH * D_MODEL)), const((ATT_WIDTH, D_MODEL)),
                  const((POOL_WIDTH, D_MODEL)), const((SSM_WIDTH, D_MODEL)), const((D_MODEL, D_MODEL)),
                  const((1, D_MODEL)), const((1, D_MODEL))],
        out_specs=row(D_MODEL),
        out_shape=jax.ShapeDtypeStruct((N, D_MODEL), F32),
        compiler_params=_cparams(("parallel", "parallel")),
        name="merge_ln1",
    )(x2, ya, yb, yc_tm, wg, wa, wb, wc, wo, g, b)


def _route_kernel(x_ref, wr_ref, br_ref, gate_ref, slot_ref, slott_ref, cnt_ref):
    tm = x_ref.shape[0]
    logits = jnp.dot(x_ref[...], wr_ref[...], preferred_element_type=F32,
                     precision=lax.Precision.HIGHEST) + br_ref[...]
    lane = lax.broadcasted_iota(I32, (tm, LANES), 1)
    lane_f = lane.astype(F32)
    work = logits
    vals, idxs = [], []
    multihot = jnp.zeros((tm, LANES), F32)
    for _ in range(TOP_K):
        mx = jnp.max(work, axis=-1, keepdims=True)
        ix = jnp.min(jnp.where(work == mx, lane_f, float(LANES)), axis=-1, keepdims=True)
        sel = lane_f == ix
        work = jnp.where(sel, -jnp.inf, work)
        multihot = multihot + sel.astype(F32)
        vals.append(mx)
        idxs.append(ix)
    exps = [jnp.exp(v - vals[0]) for v in vals]
    denom = exps[0] + exps[1] + exps[2] + exps[3]
    r = lax.broadcasted_iota(I32, (tm, tm), 0)
    c = lax.broadcasted_iota(I32, (tm, tm), 1)
    lower = jnp.where(c < r, 1.0, 0.0).astype(BF16)
    before = jnp.dot(lower, multihot.astype(BF16), preferred_element_type=F32)
    n = jnp.sum(multihot, axis=0, keepdims=True)
    n_pad = jnp.floor((n + float(SUBLANES - 1)) * (1.0 / SUBLANES)) * float(SUBLANES)
    lane8 = lax.broadcasted_iota(I32, (SUBLANES, LANES), 1)
    incl = jnp.broadcast_to(n_pad, (SUBLANES, LANES))
    sh = 1
    while sh < LANES:
        incl = incl + jnp.where(lane8 >= sh, pltpu.roll(incl, sh, 1), 0.0)
        sh *= 2
    run_start = incl[0:1, :] - n_pad
    where_in_tile = before + run_start
    gate_out = jnp.zeros((tm, LANES), F32)
    slot_out = jnp.zeros((tm, LANES), F32)
    for kk in range(TOP_K):
        sl = jnp.sum(jnp.where(lane_f == idxs[kk], where_in_tile, 0.0), axis=-1, keepdims=True)
        here = lane == kk
        gate_out = jnp.where(here, exps[kk] / denom, gate_out)
        slot_out = jnp.where(here, sl, slot_out)
    gate_ref[...] = gate_out
    slot_ref[...] = slot_out.astype(I32)
    slott_ref[0] = slot_out.T[0:SUBLANES, :].astype(I32)
    cnt_ref[0] = n_pad.astype(I32)


def _route(x1, wr, br):
    N = x1.shape[0]
    nt = N // TM_ROW
    row = pl.BlockSpec((TM_ROW, LANES), lambda i: (i, 0))
    return pl.pallas_call(
        _route_kernel,
        grid=(nt,),
        in_specs=[pl.BlockSpec((TM_ROW, D_MODEL), lambda i: (i, 0)),
                  pl.BlockSpec((D_MODEL, LANES), lambda i: (0, 0)),
                  pl.BlockSpec((1, LANES), lambda i: (0, 0))],
        out_specs=[row, row,
                   pl.BlockSpec((1, SUBLANES, TM_ROW), lambda i: (i, 0, 0)),
                   pl.BlockSpec((1, 1, LANES), lambda i: (i, 0, 0))],
        out_shape=[jax.ShapeDtypeStruct((N, LANES), F32), jax.ShapeDtypeStruct((N, LANES), I32),
                   jax.ShapeDtypeStruct((nt, SUBLANES, TM_ROW), I32),
                   jax.ShapeDtypeStruct((nt, 1, LANES), I32)],
        compiler_params=_cparams(("parallel",)),
        name="router_top4",
    )(x1, wr, br)


def _group_copies(tab_ref, make_copy):
    n_groups = tab_ref[0, 0, 0]

    def copy(u):
        local = pl.multiple_of(u * SUBLANES, SUBLANES)
        remote = pl.multiple_of(tab_ref[0, 0, 1 + u], SUBLANES)
        return make_copy(local, remote)

    def issue(u, _):
        copy(u).start()
        return 0

    def drain(u, _):
        copy(u).wait()
        return 0

    return (lambda: lax.fori_loop(0, n_groups, issue, 0)), (lambda: lax.fori_loop(0, n_groups, drain, 0))


def _dispatch_kernel(tab_ref, x_ref, slott_ref, xs_hbm, srt_scr, sem):
    tm = x_ref.shape[0]
    r = lax.broadcasted_iota(I32, (ROWS_TILE, tm), 0)
    hit = r == slott_ref[0, 0:1, :]
    for kk in range(1, TOP_K):
        hit = hit | (r == slott_ref[0, kk:kk + 1, :])
    sel = jnp.where(hit, 1.0, 0.0).astype(BF16)
    srt = jnp.dot(sel, x_ref[...].astype(BF16), preferred_element_type=F32)
    srt_scr[...] = _pack_rows(srt)
    issue, drain = _group_copies(
        tab_ref, lambda local, remote: pltpu.make_async_copy(
            srt_scr.at[pl.ds(local, SUBLANES)], xs_hbm.at[pl.ds(remote, SUBLANES)], sem))
    issue()
    drain()

    @pl.when(pl.program_id(0) == pl.num_programs(0) - 1)
    def _():
        used = pl.multiple_of(tab_ref[0, 0, TAB_W - 1], SUBLANES)
        free = xs_hbm.shape[0] - used
        n_big = free // BM_EXP
        n_small = (free - n_big * BM_EXP) // SUBLANES
        srt_scr[0:BM_EXP, :] = jnp.zeros((BM_EXP, HALF_D), U32)

        def big(i):
            return pltpu.make_async_copy(
                srt_scr.at[pl.ds(0, BM_EXP)],
                xs_hbm.at[pl.ds(pl.multiple_of(used + i * BM_EXP, SUBLANES), BM_EXP)], sem)

        def small(i):
            return pltpu.make_async_copy(
                srt_scr.at[pl.ds(0, SUBLANES)],
                xs_hbm.at[pl.ds(pl.multiple_of(used + n_big * BM_EXP + i * SUBLANES, SUBLANES), SUBLANES)], sem)

        def for_each(make, count, op):
            def body(i, c):
                op(make(i))
                return c
            lax.fori_loop(0, count, body, 0)

        for op in (lambda cp: cp.start(), lambda cp: cp.wait()):
            for_each(big, n_big, op)
            for_each(small, n_small, op)


def _dispatch(tab, x1, slott, m_pad):
    N = x1.shape[0]
    return pl.pallas_call(
        _dispatch_kernel,
        grid=(N // TM_ROW,),
        in_specs=[pl.BlockSpec((1, 1, TAB_W), lambda i: (i, 0, 0), memory_space=pltpu.SMEM),
                  pl.BlockSpec((TM_ROW, D_MODEL), lambda i: (i, 0)),
                  pl.BlockSpec((1, SUBLANES, TM_ROW), lambda i: (i, 0, 0))],
        out_specs=pl.BlockSpec(memory_space=pl.ANY),
        out_shape=jax.ShapeDtypeStruct((m_pad, HALF_D), U32),
        scratch_shapes=[pltpu.VMEM((ROWS_TILE, HALF_D), U32), pltpu.SemaphoreType.DMA],
        compiler_params=_cparams(("arbitrary",), has_side_effects=True),
        name="moe_dispatch",
    )(tab, x1, slott)


def _expert_kernel(blk_ref, exp_ref, lo_ref, hi_ref, xs_ref, wu_ref, bu_ref, wd_ref, bd_ref, o_ref):
    w = pl.program_id(0)
    lo = lo_ref[w]
    hi = hi_ref[w]

    @pl.when(lo > hi)
    def _():
        o_ref[...] = jnp.zeros_like(o_ref)

    @pl.when(hi > lo)
    def _():
        xh, xl = _unpack_rows(xs_ref[...])
        x = jnp.concatenate([xh.astype(BF16), xl.astype(BF16)], axis=1)
        hu = jnp.dot(x, wu_ref[0], preferred_element_type=F32) + bu_ref[0]
        chunks = []
        for cc in range(D_FF // LANES):
            g = jnp.minimum(hu[:, 2 * cc * LANES:(2 * cc + 1) * LANES], SWIGLU_LIMIT)
            l = jnp.clip(hu[:, (2 * cc + 1) * LANES:(2 * cc + 2) * LANES], -SWIGLU_LIMIT, SWIGLU_LIMIT)
            chunks.append(g * _sigmoid(SWIGLU_ALPHA * g) * (l + 1.0))
        act = jnp.concatenate(chunks, axis=1)
        y = jnp.dot(act.astype(BF16), wd_ref[0], preferred_element_type=F32) + bd_ref[0]
        packed = _pack_rows(y)

        @pl.when(lo == 0)
        def _():
            o_ref[...] = packed

        @pl.when(lo > 0)
        def _():
            row = lax.broadcasted_iota(I32, packed.shape, 0)
            o_ref[...] = jnp.where((row >= lo) & (row < hi), packed, o_ref[...])


def _wup_prep_kernel(w_ref, perm_ref, o_ref):
    for cc in range(2 * D_FF // (2 * LANES)):
        cols = slice(cc * 2 * LANES, (cc + 1) * 2 * LANES)
        o_ref[0, :, cols] = jnp.dot(w_ref[0, :, cols].astype(BF16), perm_ref[...],
                                    preferred_element_type=F32).astype(BF16)


def _wup_prep(w_up_l, perm):
    E = w_up_l.shape[0]
    spec = pl.BlockSpec((1, D_MODEL, 2 * D_FF), lambda e: (e, 0, 0))
    return pl.pallas_call(
        _wup_prep_kernel,
        grid=(E,),
        in_specs=[spec, pl.BlockSpec((2 * LANES, 2 * LANES), lambda e: (0, 0))],
        out_specs=spec,
        out_shape=jax.ShapeDtypeStruct(w_up_l.shape, BF16),
        compiler_params=_cparams(("parallel",)),
        name="w_up_regroup",
    )(w_up_l, perm)


def _pair_perm():
    p = np.zeros((2 * LANES, 2 * LANES), np.float32)
    j = np.arange(LANES)
    p[2 * j, j] = 1.0
    p[2 * j + 1, LANES + j] = 1.0
    return jnp.asarray(p, BF16)


def _experts(meta, xs, wu, bu, wd, bd):
    M = xs.shape[0]
    W = meta[0].shape[0]
    by_expert = lambda *shape: pl.BlockSpec((1,) + shape, lambda w, blk, ex, lo, hi: (ex[w], 0, 0))
    rows = pl.BlockSpec((BM_EXP, HALF_D), lambda w, blk, ex, lo, hi: (blk[w], 0))
    return pl.pallas_call(
        _expert_kernel,
        grid_spec=pltpu.PrefetchScalarGridSpec(
            num_scalar_prefetch=4,
            grid=(W,),
            in_specs=[rows, by_expert(D_MODEL, 2 * D_FF), by_expert(1, 2 * D_FF),
                      by_expert(D_FF, D_MODEL), by_expert(1, D_MODEL)],
            out_specs=rows,
        ),
        out_shape=jax.ShapeDtypeStruct((M, HALF_D), U32),
        compiler_params=_cparams(("arbitrary",)),
        name="moe_experts",
    )(*meta, xs, wu, bu, wd, bd)


def _combine_kernel(tab_ref, x_ref, gate_ref, slot_ref, g_ref, b_ref, yb_hbm, o_ref, buf, sem):
    tm = x_ref.shape[0]

    @pl.when(pl.program_id(0) == 0)
    def _():
        buf[...] = jnp.zeros_like(buf)

    issue, drain = _group_copies(
        tab_ref, lambda local, remote: pltpu.make_async_copy(
            yb_hbm.at[pl.ds(remote, SUBLANES)], buf.at[pl.ds(local, SUBLANES)], sem))
    issue()
    r = lax.broadcasted_iota(I32, (tm, ROWS_TILE), 1)
    slots = slot_ref[...]
    gates = gate_ref[...]
    pw = jnp.zeros((tm, ROWS_TILE), F32)
    for kk in range(TOP_K):
        pw = jnp.where(r == slots[:, kk:kk + 1], gates[:, kk:kk + 1], pw)
    pw_hi = pw.astype(BF16)
    pw_lo = (pw - pw_hi.astype(F32)).astype(BF16)
    drain()
    yh, yl = _unpack_rows(buf[...])
    ys = jnp.concatenate([yh.astype(BF16), yl.astype(BF16)], axis=1)
    y = jnp.dot(pw_hi, ys, preferred_element_type=F32) + jnp.dot(pw_lo, ys, preferred_element_type=F32)
    o_ref[...] = _layer_norm(DEEPNORM_ALPHA * x_ref[...] + y, g_ref[...], b_ref[...])


def _combine(tab, x1, gate, slot, g, b, yb):
    N = x1.shape[0]
    return pl.pallas_call(
        _combine_kernel,
        grid=(N // TM_ROW,),
        in_specs=[pl.BlockSpec((1, 1, TAB_W), lambda i: (i, 0, 0), memory_space=pltpu.SMEM),
                  pl.BlockSpec((TM_ROW, D_MODEL), lambda i: (i, 0)),
                  pl.BlockSpec((TM_ROW, LANES), lambda i: (i, 0)),
                  pl.BlockSpec((TM_ROW, LANES), lambda i: (i, 0)),
                  pl.BlockSpec((1, D_MODEL), lambda i: (0, 0)),
                  pl.BlockSpec((1, D_MODEL), lambda i: (0, 0)),
                  pl.BlockSpec(memory_space=pl.ANY)],
        out_specs=pl.BlockSpec((TM_ROW, D_MODEL), lambda i: (i, 0)),
        out_shape=jax.ShapeDtypeStruct((N, D_MODEL), F32),
        scratch_shapes=[pltpu.VMEM((ROWS_TILE, HALF_D), U32), pltpu.SemaphoreType.DMA],
        compiler_params=_cparams(("arbitrary",)),
        name="moe_combine_ln2",
    )(tab, x1, gate, slot, g, b, yb)


def _moe_tables(cnt, m_pad):
    n_pad = cnt[:, 0, :N_EXPERTS]
    meta, region = _work_items(n_pad.sum(axis=0), m_pad)
    run_end = jnp.cumsum(n_pad, axis=1)
    run_start = run_end - n_pad
    hbm_start = region[None, :] + jnp.cumsum(n_pad, axis=0) - n_pad
    first_row = SUBLANES * jnp.arange(GROUPS_TILE, dtype=I32)
    owner = jnp.minimum((run_end[:, None, :] <= first_row[None, :, None]).sum(axis=-1), N_EXPERTS - 1)
    sel = owner[:, :, None] == jnp.arange(N_EXPERTS, dtype=I32)[None, None, :]
    shift = jnp.sum(jnp.where(sel, (hbm_start - run_start)[:, None, :], 0), axis=-1)
    n_groups = run_end[:, -1:] // SUBLANES
    pad = jnp.zeros((n_pad.shape[0], TAB_W - 2 - GROUPS_TILE), I32)
    used = jnp.broadcast_to(n_pad.sum(), n_groups.shape)
    tab = jnp.concatenate([n_groups, shift + first_row[None, :], pad, used], axis=1).astype(I32)
    return tab[:, None, :], meta


def _work_items(counts, M):
    nblk = M // BM_EXP
    W = nblk + N_EXPERTS
    ends = jnp.cumsum(counts)
    starts = ends - counts
    first = starts // BM_EXP
    last = jnp.maximum(ends - 1, 0) // BM_EXP
    n_items = jnp.where(counts > 0, last - first + 1, 0)
    item_end = jnp.cumsum(n_items)
    item_start = item_end - n_items
    total = item_end[-1]
    w = jnp.arange(W, dtype=I32)
    w_eff = jnp.minimum(w, total - 1)
    owner = (item_end[None, :] <= w_eff[:, None]).sum(axis=1).astype(I32)
    sel = owner[:, None] == jnp.arange(N_EXPERTS, dtype=I32)[None, :]
    pick = lambda table: jnp.sum(jnp.where(sel, table[None, :], 0), axis=1)
    blk = pick(first) + (w_eff - pick(item_start))
    lo = jnp.clip(pick(starts) - blk * BM_EXP, 0, BM_EXP)
    hi = jnp.clip(pick(ends) - blk * BM_EXP, 0, BM_EXP)
    valid = w < total
    used_blocks = (ends[-1] + BM_EXP - 1) // BM_EXP
    spare_blk = used_blocks + (w - total)
    fill = (~valid) & (spare_blk < nblk)
    blk = jnp.where(valid, blk, jnp.minimum(spare_blk, nblk - 1))
    lo = jnp.where(valid, lo, jnp.where(fill, 1, 0))
    hi = jnp.where(valid, hi, 0)
    return (blk.astype(I32), owner, lo.astype(I32), hi.astype(I32)), starts


def _block_diag(blocks):
    G, a, b = blocks.shape
    eye = jnp.eye(G, dtype=blocks.dtype)
    return jnp.einsum("gab,gh->gahb", blocks, eye).reshape(G * a, G * b)


def kernel(x, w_in, b_forget, w_pool, pool_scale, ssm_lambda_re, ssm_lambda_im, ssm_log_dt, ssm_b_re, ssm_b_im, ssm_c_re, ssm_c_im, ssm_d, w_glu, b_glu, w_branch_a, w_branch_b, w_branch_c, w_out, ln1_g, ln1_b, w_router, b_router, w_up, b_up, w_down, b_down, ln2_g, ln2_b):
    B, S, D = x.shape
    assert D == D_MODEL and S % T_ATT == 0 and S % TM_PROJ == 0
    N = B * S
    m_pad = (N // TM_ROW) * ROWS_TILE
    assert m_pad % BM_EXP == 0
    perm = _pair_perm()
    x2 = x.reshape(N, D)
    scale = ATT_HEAD_DIM ** -0.5
    for l in range(DEPTH):
        wl = w_in[l]
        wq = wl[:, :ATT_WIDTH] * scale
        wqkv = jnp.concatenate([wq, wl[:, ATT_WIDTH:QKV_W]], axis=1).astype(BF16)
        c0 = QKV_W
        wf = wl[:, c0:c0 + ATT_HEADS]
        wsmall = jnp.concatenate(
            [wl[:, c0 + ATT_HEADS:c0 + ATT_HEADS + POOL_WIDTH + SSM_WIDTH], wf,
             jnp.zeros((D, F_PAD - ATT_HEADS), F32)], axis=1).astype(BF16)
        wgates = wl[:, c0 + ATT_HEADS + POOL_WIDTH + SSM_WIDTH:].astype(BF16)

        q, k, v, u_pool, u_ssm_tm, f_pad = _proj(x2, wqkv, wsmall, B, S)

        f_rows = f_pad[:, :ATT_HEADS].reshape(B, S, ATT_HEADS).transpose(0, 2, 1).reshape(B * ATT_HEADS, S)
        b_rows = jnp.tile(b_forget[l], B).reshape(B * ATT_HEADS, 1)
        c = _fcum(f_rows, b_rows).reshape(B, ATT_HEADS, 1, S)
        y_a = _attention(q, k, v, c, B, S).reshape(N, ATT_WIDTH)

        w_pool_bd = _block_diag(w_pool[l]).astype(BF16)
        y_b = _pool(u_pool.reshape(B, S, POOL_WIDTH), w_pool_bd, pool_scale[l].reshape(1, POOL_WIDTH)).reshape(N, POOL_WIDTH)

        ar, ai, bbrT, bbiT = _ssm_prep(ssm_lambda_re[l], ssm_lambda_im[l], ssm_log_dt[l], ssm_b_re[l], ssm_b_im[l])
        bblk = jnp.concatenate([_block_diag(bbrT), _block_diag(bbiT)], axis=1).astype(BF16)
        cblk = jnp.concatenate([_block_diag(ssm_c_re[l].transpose(0, 2, 1)),
                                -_block_diag(ssm_c_im[l].transpose(0, 2, 1))], axis=0).astype(BF16)
        ar_b = jnp.broadcast_to(ar.reshape(1, SSM_STATES), (B, SSM_STATES))
        ai_b = jnp.broadcast_to(ai.reshape(1, SSM_STATES), (B, SSM_STATES))
        y_c_tm = _ssm(u_ssm_tm.reshape(S * B, SSM_WIDTH), bblk, ar_b, ai_b, cblk,
                      ssm_d[l].reshape(1, SSM_WIDTH), w_glu[l].astype(BF16), b_glu[l].reshape(1, SSM_WIDTH), B, S)

        x1 = _merge(x2, y_a, y_b, y_c_tm.reshape(S, B * SSM_WIDTH), wgates,
                    w_branch_a[l].astype(BF16), w_branch_b[l].astype(BF16), w_branch_c[l].astype(BF16),
                    w_out[l].astype(BF16), ln1_g[l].reshape(1, D), ln1_b[l].reshape(1, D), B, S)

        wr = jnp.concatenate([w_router[l], jnp.zeros((D, LANES - N_EXPERTS), F32)], axis=1)
        br = jnp.concatenate([b_router[l], jnp.full((LANES - N_EXPERTS,), _NEG_BIG, F32)]).reshape(1, LANES)
        gate, slot, slott, cnt = _route(x1, wr, br)
        tab, meta = _moe_tables(cnt, m_pad)
        xs = _dispatch(tab, x1, slott, m_pad)
        bu = b_up[l].reshape(N_EXPERTS, D_FF // LANES, LANES, 2).transpose(0, 1, 3, 2).reshape(N_EXPERTS, 1, 2 * D_FF)
        yb = _experts(meta, xs, _wup_prep(w_up[l], perm), bu, w_down[l].astype(BF16), b_down[l][:, None, :])
        x2 = _combine(tab, x1, gate, slot, ln2_g[l].reshape(1, D), ln2_b[l].reshape(1, D), yb)
    return x2.reshape(B, S, D)
```

```python
import functools
import math

import jax
import jax.numpy as jnp
import numpy as np
from jax import lax
from jax.experimental import pallas as pl
from jax.experimental.pallas import tpu as pltpu

F32 = jnp.float32
BF16 = jnp.bfloat16
I32 = jnp.int32
U32 = jnp.uint32

D_MODEL = 1024
DEPTH = 4
ATT_HEADS = 8
ATT_HEAD_DIM = 64
ATT_WIDTH = ATT_HEADS * ATT_HEAD_DIM
POOL_WINDOWS = (2, 4, 8, 16)
POOL_GROUPS = 4
POOL_WIDTH = 256
POOL_GROUP_DIM = 64
MAX_WINDOW = max(POOL_WINDOWS)
SSM_WIDTH = 256
SSM_GROUP_DIM = 16
SSM_GROUPS = 16
SSM_STATE = 64
SSM_STATES = SSM_GROUPS * SSM_STATE
N_BRANCH = 3
N_EXPERTS = 32
TOP_K = 4
D_FF = D_MODEL
SWIGLU_LIMIT = 7.0
SWIGLU_ALPHA = 1.702
LN_EPS = 1e-5
DEEPNORM_ALPHA = (2.0 * DEPTH) ** 0.25
GELU_C = math.sqrt(2.0 / math.pi)

LANES = 128
HALF_D = D_MODEL // 2
F_PAD = LANES
V_PAD = LANES
SMALL_W = POOL_WIDTH + SSM_WIDTH + F_PAD
QKV_W = 3 * ATT_WIDTH
TM_PROJ = 512
TQ_ATT = 1024
TK_ATT = 1024
assert TQ_ATT % TK_ATT == 0 or TK_ATT % TQ_ATT == 0
TM_POOL = 256
TT_SSM = 64
TM_MERGE = 256
SUBLANES = 8
TM_ROW = 256
ROWS_TILE = TM_ROW * TOP_K + N_EXPERTS * SUBLANES
GROUPS_TILE = ROWS_TILE // SUBLANES
TAB_W = 2 * LANES
assert TAB_W >= GROUPS_TILE + 2
BM_EXP = 256
VMEM_LIMIT = 52 * 1024 * 1024

_NEG_BIG = -1e30


def _cparams(sem, **kw):
    return pltpu.CompilerParams(dimension_semantics=sem, vmem_limit_bytes=VMEM_LIMIT, **kw)


def _sigmoid(x):
    return 1.0 / (1.0 + jnp.exp(-x))


def _layer_norm(z, g, b):
    mu = jnp.mean(z, axis=-1, keepdims=True)
    zc = z - mu
    var = jnp.mean(zc * zc, axis=-1, keepdims=True)
    return zc * lax.rsqrt(var + LN_EPS) * g + b


def _pack_rows(y):
    u = pltpu.bitcast(y.astype(BF16).astype(F32), U32)
    return u[:, :HALF_D] | (u[:, HALF_D:] >> 16)


def _unpack_rows(p):
    hi = pltpu.bitcast(p & jnp.uint32(0xFFFF0000), F32)
    lo = pltpu.bitcast(p << 16, F32)
    return hi, lo


def _proj_kernel(x_ref, wqkv_ref, ws_ref, q_ref, k_ref, v_ref, up_ref, us_ref, f_ref):
    xb = x_ref[...].astype(BF16)
    for part, ref in enumerate((q_ref, k_ref)):
        h = jnp.dot(xb, wqkv_ref[:, part * ATT_WIDTH:(part + 1) * ATT_WIDTH], preferred_element_type=F32)
        for hh in range(ATT_HEADS):
            ref[0, hh] = h[:, hh * ATT_HEAD_DIM:(hh + 1) * ATT_HEAD_DIM].astype(BF16)
    h = jnp.dot(xb, wqkv_ref[:, 2 * ATT_WIDTH:], preferred_element_type=F32)
    lane = lax.broadcasted_iota(I32, (h.shape[0], V_PAD), 1)
    tail = jnp.where(lane == ATT_HEAD_DIM, 1.0, 0.0)
    for pair in range(ATT_HEADS // 2):
        slab = h[:, pair * V_PAD:(pair + 1) * V_PAD]
        v_ref[0, 2 * pair] = jnp.where(lane < ATT_HEAD_DIM, slab, tail).astype(BF16)
        v_ref[0, 2 * pair + 1] = jnp.where(lane < ATT_HEAD_DIM, pltpu.roll(slab, ATT_HEAD_DIM, 1), tail).astype(BF16)
    hs = jnp.dot(xb, ws_ref[...], preferred_element_type=F32)
    up_ref[...] = hs[:, :POOL_WIDTH]
    us_ref[...] = hs[:, POOL_WIDTH:POOL_WIDTH + SSM_WIDTH]
    f_ref[...] = hs[:, POOL_WIDTH + SSM_WIDTH:]


def _proj(x2, wqkv, wsmall, B, S):
    N = B * S
    nS = S // TM_PROJ
    hm = jax.ShapeDtypeStruct((B, ATT_HEADS, S, ATT_HEAD_DIM), BF16)
    hm_spec = pl.BlockSpec((1, ATT_HEADS, TM_PROJ, ATT_HEAD_DIM), lambda b, s: (b, 0, s, 0))
    hv = jax.ShapeDtypeStruct((B, ATT_HEADS, S, V_PAD), BF16)
    hv_spec = pl.BlockSpec((1, ATT_HEADS, TM_PROJ, V_PAD), lambda b, s: (b, 0, s, 0))
    return pl.pallas_call(
        _proj_kernel,
        grid=(B, nS),
        in_specs=[
            pl.BlockSpec((TM_PROJ, D_MODEL), lambda b, s: (b * nS + s, 0)),
            pl.BlockSpec((D_MODEL, QKV_W), lambda b, s: (0, 0)),
            pl.BlockSpec((D_MODEL, SMALL_W), lambda b, s: (0, 0)),
        ],
        out_specs=[
            hm_spec, hm_spec, hv_spec,
            pl.BlockSpec((TM_PROJ, POOL_WIDTH), lambda b, s: (b * nS + s, 0)),
            pl.BlockSpec((TM_PROJ, SSM_WIDTH), lambda b, s: (s, b)),
            pl.BlockSpec((TM_PROJ, F_PAD), lambda b, s: (b * nS + s, 0)),
        ],
        out_shape=[
            hm, hm, hv,
            jax.ShapeDtypeStruct((N, POOL_WIDTH), F32),
            jax.ShapeDtypeStruct((S, B * SSM_WIDTH), F32),
            jax.ShapeDtypeStruct((N, F_PAD), F32),
        ],
        compiler_params=_cparams(("parallel", "parallel")),
        name="in_proj",
    )(x2, wqkv, wsmall)


def _fcum_kernel(f_ref, b_ref, c_ref):
    rows, S = f_ref.shape
    lane = lax.broadcasted_iota(I32, (rows, LANES), 1)
    carry = jnp.zeros((rows, 1), F32)
    for ch in range(S // LANES):
        z = f_ref[:, ch * LANES:(ch + 1) * LANES] + b_ref[...]
        lf = jnp.minimum(z, 0.0) - jnp.log1p(jnp.exp(-jnp.abs(z)))
        sh = 1
        while sh < LANES:
            lf = lf + jnp.where(lane >= sh, pltpu.roll(lf, sh, 1), 0.0)
            sh *= 2
        lf = lf + carry
        c_ref[:, ch * LANES:(ch + 1) * LANES] = lf
        carry = lf[:, LANES - 1:LANES]


def _fcum(f_rows, b_rows):
    rows, S = f_rows.shape
    return pl.pallas_call(
        _fcum_kernel,
        out_shape=jax.ShapeDtypeStruct((rows, S), F32),
        compiler_params=pltpu.CompilerParams(vmem_limit_bytes=VMEM_LIMIT),
        name="forget_cumsum",
    )(f_rows, b_rows)


def _attn_kernel(q_ref, k_ref, v_ref, c_ref, o_ref):
    TQ, TK = TQ_ATT, TK_ATT
    qi = pl.program_id(2)
    delta = lax.broadcasted_iota(I32, (TQ, TK), 1) - lax.broadcasted_iota(I32, (TQ, TK), 0)
    qs = [q_ref[0, hh] for hh in range(2)]

    def block(j, carry, masked):
        start = pl.multiple_of(j * TK, TK)
        out = []
        for hh in range(2):
            m, acc = carry[2 * hh], carry[2 * hh + 1]
            k = k_ref[0, hh, pl.ds(start, TK), :]
            v = v_ref[0, hh, pl.ds(start, TK), :]
            s = lax.dot_general(qs[hh], k, (((1,), (1,)), ((), ())), preferred_element_type=F32)
            s = s - c_ref[0, hh, :, pl.ds(start, TK)]
            if masked:
                s = jnp.where(delta <= qi * TQ - start, s, -jnp.inf)
            m_new = jnp.maximum(m, jnp.max(s, axis=-1, keepdims=True))
            p = jnp.exp(s - m_new)
            acc = jnp.exp(m - m_new) * acc + jnp.dot(p.astype(BF16), v, preferred_element_type=F32)
            out += [m_new, acc]
        return tuple(out)

    m0 = jnp.full((TQ, 1), -jnp.inf, F32)
    acc0 = jnp.zeros((TQ, V_PAD), F32)
    n_full = (qi * TQ) // TK
    carry = lax.fori_loop(0, n_full, functools.partial(block, masked=False), (m0, acc0, m0, acc0))
    for d in range(max(1, TQ // TK)):
        carry = block(n_full + d, carry, True)
    for hh in range(2):
        acc = carry[2 * hh + 1]
        out = acc[:, :ATT_HEAD_DIM] / acc[:, ATT_HEAD_DIM:ATT_HEAD_DIM + 1]
        o_ref[0, :, hh * ATT_HEAD_DIM:(hh + 1) * ATT_HEAD_DIM] = out.astype(BF16)


def _attention(q, k, v, c, B, S):
    nq = S // TQ_ATT
    qspec = pl.BlockSpec((1, 2, TQ_ATT, ATT_HEAD_DIM), lambda b, hp, i: (b, hp, i, 0))
    kspec = pl.BlockSpec((1, 2, S, ATT_HEAD_DIM), lambda b, hp, i: (b, hp, 0, 0))
    vspec = pl.BlockSpec((1, 2, S, V_PAD), lambda b, hp, i: (b, hp, 0, 0))
    return pl.pallas_call(
        _attn_kernel,
        grid=(B, ATT_HEADS // 2, nq),
        in_specs=[qspec, kspec, vspec,
                  pl.BlockSpec((1, 2, 1, S), lambda b, hp, i: (b, hp, 0, 0))],
        out_specs=pl.BlockSpec((1, TQ_ATT, 2 * ATT_HEAD_DIM), lambda b, hp, i: (b, i, hp)),
        out_shape=jax.ShapeDtypeStruct((B, S, ATT_WIDTH), BF16),
        compiler_params=_cparams(("parallel", "parallel", "arbitrary")),
        name="fox_attention",
    )(q, k, v, c)


def _pool_kernel(u_ref, w_ref, sc_ref, o_ref, pad_ref):
    S = u_ref.shape[1]
    R = TM_POOL
    pad_ref[0:MAX_WINDOW, :] = jnp.zeros((MAX_WINDOW, POOL_WIDTH), F32)
    pad_ref[MAX_WINDOW:, :] = u_ref[0]
    lane = lax.broadcasted_iota(I32, (R, POOL_WIDTH), 1)
    trow = lax.broadcasted_iota(I32, (R, POOL_WIDTH), 0)
    grp = lane // POOL_GROUP_DIM
    win = jnp.where(grp == 0, POOL_WINDOWS[0],
                    jnp.where(grp == 1, POOL_WINDOWS[1], jnp.where(grp == 2, POOL_WINDOWS[2], POOL_WINDOWS[3])))
    for i in range(S // R):
        base = MAX_WINDOW + i * R
        u0 = pad_ref[base:base + R, :]
        acc = u0
        sums = {}
        for kk in range(1, MAX_WINDOW):
            acc = acc + pad_ref[base - kk:base - kk + R, :]
            if kk + 1 in POOL_WINDOWS:
                sums[kk + 1] = acc
        total = jnp.where(grp == 0, sums[POOL_WINDOWS[0]],
                          jnp.where(grp == 1, sums[POOL_WINDOWS[1]],
                                    jnp.where(grp == 2, sums[POOL_WINDOWS[2]], sums[POOL_WINDOWS[3]])))
        cnt = jnp.minimum(trow + (i * R + 1), win).astype(F32)
        mixed = total / cnt - u0
        y = jnp.dot(mixed.astype(BF16), w_ref[...], preferred_element_type=F32) * sc_ref[...]
        o_ref[0, i * R:(i + 1) * R, :] = y.astype(BF16)


def _pool(u3, w_bd, scale):
    B, S, _ = u3.shape
    return pl.pallas_call(
        _pool_kernel,
        grid=(B,),
        in_specs=[pl.BlockSpec((1, S, POOL_WIDTH), lambda b: (b, 0, 0)),
                  pl.BlockSpec((POOL_WIDTH, POOL_WIDTH), lambda b: (0, 0)),
                  pl.BlockSpec((1, POOL_WIDTH), lambda b: (0, 0))],
        out_specs=pl.BlockSpec((1, S, POOL_WIDTH), lambda b: (b, 0, 0)),
        out_shape=jax.ShapeDtypeStruct((B, S, POOL_WIDTH), BF16),
        scratch_shapes=[pltpu.VMEM((S + MAX_WINDOW, POOL_WIDTH), F32)],
        compiler_params=_cparams(("parallel",)),
        name="multiscale_pool",
    )(u3, w_bd, scale)


def _ssm_prep_kernel(lr_ref, li_ref, ldt_ref, brT_ref, biT_ref, ar_ref, ai_ref, bbr_ref, bbi_ref):
    lr = lr_ref[...]
    li = li_ref[...]
    dt = jnp.exp(ldt_ref[...])
    mag = jnp.exp(lr * dt)
    ar = mag * jnp.cos(li * dt)
    ai = mag * jnp.sin(li * dt)
    den = lr * lr + li * li
    nr = ar - 1.0
    zr = (nr * lr + ai * li) / den
    zi = (ai * lr - nr * li) / den
    ar_ref[...] = ar
    ai_ref[...] = ai
    br = brT_ref[...]
    bi = biT_ref[...]
    bbr_ref[...] = zr[:, None, :] * br - zi[:, None, :] * bi
    bbi_ref[...] = zr[:, None, :] * bi + zi[:, None, :] * br


def _ssm_prep(lr, li, log_dt, b_re, b_im):
    G, P, H = b_re.shape
    gp = jax.ShapeDtypeStruct((G, P), F32)
    ghp = jax.ShapeDtypeStruct((G, H, P), F32)
    return pl.pallas_call(
        _ssm_prep_kernel,
        out_shape=[gp, gp, ghp, ghp],
        name="s5_discretise",
    )(lr, li, log_dt.reshape(G, 1), b_re.transpose(0, 2, 1), b_im.transpose(0, 2, 1))


def _gelu_tanh(y):
    return 0.5 * y * (1.0 + jnp.tanh(GELU_C * (y + 0.044715 * (y * y * y))))


def _ssm_kernel(u_ref, bb_ref, ar_ref, ai_ref, cc_ref, d_ref, wg_ref, bg_ref, o_ref, x_scr, hr_scr, hi_scr):
    nb = hr_scr.shape[0]
    tt = u_ref.shape[0] // nb

    @pl.when(pl.program_id(0) == 0)
    def _():
        hr_scr[...] = jnp.zeros_like(hr_scr)
        hi_scr[...] = jnp.zeros_like(hi_scr)

    u = u_ref[...]
    x_scr[...] = jnp.dot(u.astype(BF16), bb_ref[...], preferred_element_type=F32)

    def step(t, carry):
        hr, hi = carry
        r0 = pl.multiple_of(t * nb, nb)
        xr = x_scr[pl.ds(r0, nb), 0:SSM_STATES]
        xi = x_scr[pl.ds(r0, nb), SSM_STATES:2 * SSM_STATES]
        ar = ar_ref[...]
        ai = ai_ref[...]
        nr = ar * hr - ai * hi + xr
        ni = ar * hi + ai * hr + xi
        x_scr[pl.ds(r0, nb), 0:SSM_STATES] = nr
        x_scr[pl.ds(r0, nb), SSM_STATES:2 * SSM_STATES] = ni
        return nr, ni

    hr, hi = lax.fori_loop(0, tt, step, (hr_scr[...], hi_scr[...]))
    hr_scr[...] = hr
    hi_scr[...] = hi
    y = jnp.dot(x_scr[...].astype(BF16), cc_ref[...], preferred_element_type=F32) + d_ref[...] * u
    y = _gelu_tanh(y)
    gl = jnp.dot(y.astype(BF16), wg_ref[...], preferred_element_type=F32) + bg_ref[...]
    o_ref[...] = (y * _sigmoid(gl)).astype(BF16)


def _ssm(u_tm, bblk, ar_b, ai_b, cblk, dvec, wglu, bglu, B, S):
    rows = TT_SSM * B
    const = lambda shape: pl.BlockSpec(shape, lambda i: (0, 0))
    return pl.pallas_call(
        _ssm_kernel,
        grid=(S // TT_SSM,),
        in_specs=[pl.BlockSpec((rows, SSM_WIDTH), lambda i: (i, 0)),
                  const((SSM_WIDTH, 2 * SSM_STATES)),
                  const((B, SSM_STATES)), const((B, SSM_STATES)),
                  const((2 * SSM_STATES, SSM_WIDTH)),
                  const((1, SSM_WIDTH)), const((SSM_WIDTH, SSM_WIDTH)), const((1, SSM_WIDTH))],
        out_specs=pl.BlockSpec((rows, SSM_WIDTH), lambda i: (i, 0)),
        out_shape=jax.ShapeDtypeStruct((S * B, SSM_WIDTH), BF16),
        scratch_shapes=[pltpu.VMEM((rows, 2 * SSM_STATES), F32),
                        pltpu.VMEM((B, SSM_STATES), F32), pltpu.VMEM((B, SSM_STATES), F32)],
        compiler_params=_cparams(("arbitrary",)),
        name="s5_scan",
    )(u_tm, bblk, ar_b, ai_b, cblk, dvec, wglu, bglu)


def _merge_kernel(x_ref, ya_ref, yb_ref, yc_ref, wg_ref, wa_ref, wb_ref, wc_ref, wo_ref, g_ref, b_ref, o_ref):
    x = x_ref[...]
    xb = x.astype(BF16)
    merged = None
    for i, (y_ref, w_ref) in enumerate(((ya_ref, wa_ref), (yb_ref, wb_ref), (yc_ref, wc_ref))):
        gate = _sigmoid(jnp.dot(xb, wg_ref[:, i * D_MODEL:(i + 1) * D_MODEL], preferred_element_type=F32))
        term = gate * jnp.dot(y_ref[...], w_ref[...], preferred_element_type=F32)
        merged = term if merged is None else merged + term
    mix = jnp.dot(merged.astype(BF16), wo_ref[...], preferred_element_type=F32)
    o_ref[...] = _layer_norm(DEEPNORM_ALPHA * x + mix, g_ref[...], b_ref[...])


def _merge(x2, ya, yb, yc_tm, wg, wa, wb, wc, wo, g, b, B, S):
    N = B * S
    nS = S // TM_MERGE
    const = lambda shape: pl.BlockSpec(shape, lambda bb, s: (0, 0))
    row = lambda w: pl.BlockSpec((TM_MERGE, w), lambda bb, s: (bb * nS + s, 0))
    return pl.pallas_call(
        _merge_kernel,
        grid=(B, nS),
        in_specs=[row(D_MODEL), row(ATT_WIDTH), row(POOL_WIDTH),
                  pl.BlockSpec((TM_MERGE, SSM_WIDTH), lambda bb, s: (s, bb)),
                  const((D_MODEL, N_BRANCH * D_MODEL)), const((ATT_WIDTH, D_MODEL)),
                  const((POOL_WIDTH, D_MODEL)), const((SSM_WIDTH, D_MODEL)), const((D_MODEL, D_MODEL)),
                  const((1, D_MODEL)), const((1, D_MODEL))],
        out_specs=row(D_MODEL),
        out_shape=jax.ShapeDtypeStruct((N, D_MODEL), F32),
        compiler_params=_cparams(("parallel", "parallel")),
        name="merge_ln1",
    )(x2, ya, yb, yc_tm, wg, wa, wb, wc, wo, g, b)


def _route_kernel(x_ref, wr_ref, br_ref, gate_ref, slot_ref, slott_ref, cnt_ref):
    tm = x_ref.shape[0]
    x = x_ref[...]
    xh = x.astype(BF16)
    xl = (x - xh.astype(F32)).astype(BF16)
    w = wr_ref[...]
    wh = w.astype(BF16)
    wl = (w - wh.astype(F32)).astype(BF16)
    logits = (jnp.dot(xh, wh, preferred_element_type=F32) + jnp.dot(xl, wh, preferred_element_type=F32)
              + jnp.dot(xh, wl, preferred_element_type=F32)) + br_ref[...]
    lane = lax.broadcasted_iota(I32, (tm, LANES), 1)
    lane_f = lane.astype(F32)
    work = logits
    vals, idxs = [], []
    multihot = jnp.zeros((tm, LANES), F32)
    for _ in range(TOP_K):
        mx = jnp.max(work, axis=-1, keepdims=True)
        ix = jnp.min(jnp.where(work == mx, lane_f, float(LANES)), axis=-1, keepdims=True)
        sel = lane_f == ix
        work = jnp.where(sel, -jnp.inf, work)
        multihot = multihot + sel.astype(F32)
        vals.append(mx)
        idxs.append(ix)
    exps = [jnp.exp(v - vals[0]) for v in vals]
    denom = exps[0] + exps[1] + exps[2] + exps[3]
    r = lax.broadcasted_iota(I32, (tm, tm), 0)
    c = lax.broadcasted_iota(I32, (tm, tm), 1)
    lower = jnp.where(c < r, 1.0, 0.0).astype(BF16)
    before = jnp.dot(lower, multihot.astype(BF16), preferred_element_type=F32)
    n = jnp.sum(multihot, axis=0, keepdims=True)
    n_pad = jnp.floor((n + float(SUBLANES - 1)) * (1.0 / SUBLANES)) * float(SUBLANES)
    lane8 = lax.broadcasted_iota(I32, (SUBLANES, LANES), 1)
    incl = jnp.broadcast_to(n_pad, (SUBLANES, LANES))
    sh = 1
    while sh < LANES:
        incl = incl + jnp.where(lane8 >= sh, pltpu.roll(incl, sh, 1), 0.0)
        sh *= 2
    run_start = incl[0:1, :] - n_pad
    where_in_tile = before + run_start
    gate_out = jnp.zeros((tm, LANES), F32)
    slot_out = jnp.zeros((tm, LANES), F32)
    for kk in range(TOP_K):
        sl = jnp.sum(jnp.where(lane_f == idxs[kk], where_in_tile, 0.0), axis=-1, keepdims=True)
        here = lane == kk
        gate_out = jnp.where(here, exps[kk] / denom, gate_out)
        slot_out = jnp.where(here, sl, slot_out)
    gate_ref[...] = gate_out
    slot_ref[...] = slot_out.astype(I32)
    slott_ref[0] = slot_out.T[0:SUBLANES, :].astype(I32)
    cnt_ref[0] = n_pad.astype(I32)


def _route(x1, wr, br):
    N = x1.shape[0]
    nt = N // TM_ROW
    row = pl.BlockSpec((TM_ROW, LANES), lambda i: (i, 0))
    return pl.pallas_call(
        _route_kernel,
        grid=(nt,),
        in_specs=[pl.BlockSpec((TM_ROW, D_MODEL), lambda i: (i, 0)),
                  pl.BlockSpec((D_MODEL, LANES), lambda i: (0, 0)),
                  pl.BlockSpec((1, LANES), lambda i: (0, 0))],
        out_specs=[row, row,
                   pl.BlockSpec((1, SUBLANES, TM_ROW), lambda i: (i, 0, 0)),
                   pl.BlockSpec((1, 1, LANES), lambda i: (i, 0, 0))],
        out_shape=[jax.ShapeDtypeStruct((N, LANES), F32), jax.ShapeDtypeStruct((N, LANES), I32),
                   jax.ShapeDtypeStruct((nt, SUBLANES, TM_ROW), I32),
                   jax.ShapeDtypeStruct((nt, 1, LANES), I32)],
        compiler_params=_cparams(("parallel",)),
        name="router_top4",
    )(x1, wr, br)


def _group_copies(tab_ref, make_copy):
    n_groups = tab_ref[0, 0, 0]

    def copy(u):
        local = pl.multiple_of(u * SUBLANES, SUBLANES)
        remote = pl.multiple_of(tab_ref[0, 0, 1 + u], SUBLANES)
        return make_copy(local, remote)

    def issue(u, _):
        copy(u).start()
        return 0

    def drain(u, _):
        copy(u).wait()
        return 0

    return (lambda: lax.fori_loop(0, n_groups, issue, 0)), (lambda: lax.fori_loop(0, n_groups, drain, 0))


def _dispatch_kernel(tab_ref, x_ref, slott_ref, xs_hbm, srt_scr, sem):
    tm = x_ref.shape[0]
    r = lax.broadcasted_iota(I32, (ROWS_TILE, tm), 0)
    hit = r == slott_ref[0, 0:1, :]
    for kk in range(1, TOP_K):
        hit = hit | (r == slott_ref[0, kk:kk + 1, :])
    sel = jnp.where(hit, 1.0, 0.0).astype(BF16)
    srt = jnp.dot(sel, x_ref[...].astype(BF16), preferred_element_type=F32)
    srt_scr[...] = _pack_rows(srt)
    issue, drain = _group_copies(
        tab_ref, lambda local, remote: pltpu.make_async_copy(
            srt_scr.at[pl.ds(local, SUBLANES)], xs_hbm.at[pl.ds(remote, SUBLANES)], sem))
    issue()
    drain()

    @pl.when(pl.program_id(0) == pl.num_programs(0) - 1)
    def _():
        used = pl.multiple_of(tab_ref[0, 0, TAB_W - 1], SUBLANES)
        free = xs_hbm.shape[0] - used
        n_big = free // BM_EXP
        n_small = (free - n_big * BM_EXP) // SUBLANES
        srt_scr[0:BM_EXP, :] = jnp.zeros((BM_EXP, HALF_D), U32)

        def big(i):
            return pltpu.make_async_copy(
                srt_scr.at[pl.ds(0, BM_EXP)],
                xs_hbm.at[pl.ds(pl.multiple_of(used + i * BM_EXP, SUBLANES), BM_EXP)], sem)

        def small(i):
            return pltpu.make_async_copy(
                srt_scr.at[pl.ds(0, SUBLANES)],
                xs_hbm.at[pl.ds(pl.multiple_of(used + n_big * BM_EXP + i * SUBLANES, SUBLANES), SUBLANES)], sem)

        def for_each(make, count, op):
            def body(i, c):
                op(make(i))
                return c
            lax.fori_loop(0, count, body, 0)

        for op in (lambda cp: cp.start(), lambda cp: cp.wait()):
            for_each(big, n_big, op)
            for_each(small, n_small, op)


def _dispatch(tab, x1, slott, m_pad):
    N = x1.shape[0]
    return pl.pallas_call(
        _dispatch_kernel,
        grid=(N // TM_ROW,),
        in_specs=[pl.BlockSpec((1, 1, TAB_W), lambda i: (i, 0, 0), memory_space=pltpu.SMEM),
                  pl.BlockSpec((TM_ROW, D_MODEL), lambda i: (i, 0)),
                  pl.BlockSpec((1, SUBLANES, TM_ROW), lambda i: (i, 0, 0))],
        out_specs=pl.BlockSpec(memory_space=pl.ANY),
        out_shape=jax.ShapeDtypeStruct((m_pad, HALF_D), U32),
        scratch_shapes=[pltpu.VMEM((ROWS_TILE, HALF_D), U32), pltpu.SemaphoreType.DMA],
        compiler_params=_cparams(("arbitrary",), has_side_effects=True),
        name="moe_dispatch",
    )(tab, x1, slott)


def _expert_kernel(blk_ref, exp_ref, lo_ref, hi_ref, xs_ref, wu_ref, bu_ref, wd_ref, bd_ref, o_ref):
    w = pl.program_id(0)
    lo = lo_ref[w]
    hi = hi_ref[w]

    @pl.when(lo > hi)
    def _():
        o_ref[...] = jnp.zeros_like(o_ref)

    @pl.when(hi > lo)
    def _():
        xh, xl = _unpack_rows(xs_ref[...])
        x = jnp.concatenate([xh.astype(BF16), xl.astype(BF16)], axis=1)
        hu = jnp.dot(x, wu_ref[0], preferred_element_type=F32) + bu_ref[0]
        chunks = []
        for cc in range(D_FF // LANES):
            g = jnp.minimum(hu[:, 2 * cc * LANES:(2 * cc + 1) * LANES], SWIGLU_LIMIT)
            l = jnp.clip(hu[:, (2 * cc + 1) * LANES:(2 * cc + 2) * LANES], -SWIGLU_LIMIT, SWIGLU_LIMIT)
            chunks.append(g * _sigmoid(SWIGLU_ALPHA * g) * (l + 1.0))
        act = jnp.concatenate(chunks, axis=1)
        y = jnp.dot(act.astype(BF16), wd_ref[0], preferred_element_type=F32) + bd_ref[0]
        packed = _pack_rows(y)

        @pl.when(lo == 0)
        def _():
            o_ref[...] = packed

        @pl.when(lo > 0)
        def _():
            row = lax.broadcasted_iota(I32, packed.shape, 0)
            o_ref[...] = jnp.where((row >= lo) & (row < hi), packed, o_ref[...])


def _expert_weight_prep_kernel(wu_ref, wd_ref, perm_ref, ou_ref, od_ref):
    for cc in range(2 * D_FF // (2 * LANES)):
        cols = slice(cc * 2 * LANES, (cc + 1) * 2 * LANES)
        ou_ref[0, :, cols] = jnp.dot(wu_ref[0, 0, :, cols].astype(BF16), perm_ref[...],
                                     preferred_element_type=F32).astype(BF16)
    od_ref[0] = wd_ref[0, 0].astype(BF16)


def _expert_weight_prep(w_up, w_down, layer, perm):
    E = w_up.shape[1]
    return pl.pallas_call(
        _expert_weight_prep_kernel,
        grid=(E,),
        in_specs=[pl.BlockSpec((1, 1, D_MODEL, 2 * D_FF), lambda e: (layer, e, 0, 0)),
                  pl.BlockSpec((1, 1, D_FF, D_MODEL), lambda e: (layer, e, 0, 0)),
                  pl.BlockSpec((2 * LANES, 2 * LANES), lambda e: (0, 0))],
        out_specs=[pl.BlockSpec((1, D_MODEL, 2 * D_FF), lambda e: (e, 0, 0)),
                   pl.BlockSpec((1, D_FF, D_MODEL), lambda e: (e, 0, 0))],
        out_shape=[jax.ShapeDtypeStruct((E, D_MODEL, 2 * D_FF), BF16),
                   jax.ShapeDtypeStruct((E, D_FF, D_MODEL), BF16)],
        compiler_params=_cparams(("parallel",)),
        name="expert_weight_prep",
    )(w_up, w_down, perm)


def _pair_perm():
    p = np.zeros((2 * LANES, 2 * LANES), np.float32)
    j = np.arange(LANES)
    p[2 * j, j] = 1.0
    p[2 * j + 1, LANES + j] = 1.0
    return jnp.asarray(p, BF16)


def _experts(meta, xs, wu, bu, wd, bd):
    M = xs.shape[0]
    W = meta[0].shape[0]
    by_expert = lambda *shape: pl.BlockSpec((1,) + shape, lambda w, blk, ex, lo, hi: (ex[w], 0, 0))
    rows = pl.BlockSpec((BM_EXP, HALF_D), lambda w, blk, ex, lo, hi: (blk[w], 0))
    return pl.pallas_call(
        _expert_kernel,
        grid_spec=pltpu.PrefetchScalarGridSpec(
            num_scalar_prefetch=4,
            grid=(W,),
            in_specs=[rows, by_expert(D_MODEL, 2 * D_FF), by_expert(1, 2 * D_FF),
                      by_expert(D_FF, D_MODEL), by_expert(1, D_MODEL)],
            out_specs=rows,
        ),
        out_shape=jax.ShapeDtypeStruct((M, HALF_D), U32),
        compiler_params=_cparams(("arbitrary",)),
        name="moe_experts",
    )(*meta, xs, wu, bu, wd, bd)


def _combine_kernel(tab_ref, x_ref, gate_ref, slot_ref, g_ref, b_ref, yb_hbm, o_ref, buf, sem):
    tm = x_ref.shape[0]

    @pl.when(pl.program_id(0) == 0)
    def _():
        buf[...] = jnp.zeros_like(buf)

    issue, drain = _group_copies(
        tab_ref, lambda local, remote: pltpu.make_async_copy(
            yb_hbm.at[pl.ds(remote, SUBLANES)], buf.at[pl.ds(local, SUBLANES)], sem))
    issue()
    r = lax.broadcasted_iota(I32, (tm, ROWS_TILE), 1)
    slots = slot_ref[...]
    gates = gate_ref[...]
    pw = jnp.zeros((tm, ROWS_TILE), F32)
    for kk in range(TOP_K):
        pw = jnp.where(r == slots[:, kk:kk + 1], gates[:, kk:kk + 1], pw)
    pw_hi = pw.astype(BF16)
    pw_lo = (pw - pw_hi.astype(F32)).astype(BF16)
    drain()
    yh, yl = _unpack_rows(buf[...])
    ys = jnp.concatenate([yh.astype(BF16), yl.astype(BF16)], axis=1)
    y = jnp.dot(pw_hi, ys, preferred_element_type=F32) + jnp.dot(pw_lo, ys, preferred_element_type=F32)
    o_ref[...] = _layer_norm(DEEPNORM_ALPHA * x_ref[...] + y, g_ref[...], b_ref[...])


def _combine(tab, x1, gate, slot, g, b, yb):
    N = x1.shape[0]
    return pl.pallas_call(
        _combine_kernel,
        grid=(N // TM_ROW,),
        in_specs=[pl.BlockSpec((1, 1, TAB_W), lambda i: (i, 0, 0), memory_space=pltpu.SMEM),
                  pl.BlockSpec((TM_ROW, D_MODEL), lambda i: (i, 0)),
                  pl.BlockSpec((TM_ROW, LANES), lambda i: (i, 0)),
                  pl.BlockSpec((TM_ROW, LANES), lambda i: (i, 0)),
                  pl.BlockSpec((1, D_MODEL), lambda i: (0, 0)),
                  pl.BlockSpec((1, D_MODEL), lambda i: (0, 0)),
                  pl.BlockSpec(memory_space=pl.ANY)],
        out_specs=pl.BlockSpec((TM_ROW, D_MODEL), lambda i: (i, 0)),
        out_shape=jax.ShapeDtypeStruct((N, D_MODEL), F32),
        scratch_shapes=[pltpu.VMEM((ROWS_TILE, HALF_D), U32), pltpu.SemaphoreType.DMA],
        compiler_params=_cparams(("arbitrary",)),
        name="moe_combine_ln2",
    )(tab, x1, gate, slot, g, b, yb)


def _moe_tables(cnt, m_pad):
    n_pad = cnt[:, 0, :N_EXPERTS]
    meta, region = _work_items(n_pad.sum(axis=0), m_pad)
    run_end = jnp.cumsum(n_pad, axis=1)
    run_start = run_end - n_pad
    hbm_start = region[None, :] + jnp.cumsum(n_pad, axis=0) - n_pad
    first_row = SUBLANES * jnp.arange(GROUPS_TILE, dtype=I32)
    owner = jnp.minimum((run_end[:, None, :] <= first_row[None, :, None]).sum(axis=-1), N_EXPERTS - 1)
    sel = owner[:, :, None] == jnp.arange(N_EXPERTS, dtype=I32)[None, None, :]
    shift = jnp.sum(jnp.where(sel, (hbm_start - run_start)[:, None, :], 0), axis=-1)
    n_groups = run_end[:, -1:] // SUBLANES
    pad = jnp.zeros((n_pad.shape[0], TAB_W - 2 - GROUPS_TILE), I32)
    used = jnp.broadcast_to(n_pad.sum(), n_groups.shape)
    tab = jnp.concatenate([n_groups, shift + first_row[None, :], pad, used], axis=1).astype(I32)
    return tab[:, None, :], meta


def _work_items(counts, M):
    nblk = M // BM_EXP
    W = nblk + N_EXPERTS
    ends = jnp.cumsum(counts)
    starts = ends - counts
    first = starts // BM_EXP
    last = jnp.maximum(ends - 1, 0) // BM_EXP
    n_items = jnp.where(counts > 0, last - first + 1, 0)
    item_end = jnp.cumsum(n_items)
    item_start = item_end - n_items
    total = item_end[-1]
    w = jnp.arange(W, dtype=I32)
    w_eff = jnp.minimum(w, total - 1)
    owner = (item_end[None, :] <= w_eff[:, None]).sum(axis=1).astype(I32)
    sel = owner[:, None] == jnp.arange(N_EXPERTS, dtype=I32)[None, :]
    pick = lambda table: jnp.sum(jnp.where(sel, table[None, :], 0), axis=1)
    blk = pick(first) + (w_eff - pick(item_start))
    lo = jnp.clip(pick(starts) - blk * BM_EXP, 0, BM_EXP)
    hi = jnp.clip(pick(ends) - blk * BM_EXP, 0, BM_EXP)
    valid = w < total
    used_blocks = (ends[-1] + BM_EXP - 1) // BM_EXP
    spare_blk = used_blocks + (w - total)
    fill = (~valid) & (spare_blk < nblk)
    blk = jnp.where(valid, blk, jnp.minimum(spare_blk, nblk - 1))
    lo = jnp.where(valid, lo, jnp.where(fill, 1, 0))
    hi = jnp.where(valid, hi, 0)
    return (blk.astype(I32), owner, lo.astype(I32), hi.astype(I32)), starts


def _block_diag(blocks):
    G, a, b = blocks.shape
    eye = jnp.eye(G, dtype=blocks.dtype)
    return jnp.einsum("gab,gh->gahb", blocks, eye).reshape(G * a, G * b)


def kernel(x, w_in, b_forget, w_pool, pool_scale, ssm_lambda_re, ssm_lambda_im, ssm_log_dt, ssm_b_re, ssm_b_im, ssm_c_re, ssm_c_im, ssm_d, w_glu, b_glu, w_branch_a, w_branch_b, w_branch_c, w_out, ln1_g, ln1_b, w_router, b_router, w_up, b_up, w_down, b_down, ln2_g, ln2_b):
    B, S, D = x.shape
    assert D == D_MODEL and S % TQ_ATT == 0 and S % TK_ATT == 0 and S % TM_PROJ == 0
    N = B * S
    m_pad = (N // TM_ROW) * ROWS_TILE
    assert m_pad % BM_EXP == 0
    perm = _pair_perm()
    x2 = x.reshape(N, D)
    scale = ATT_HEAD_DIM ** -0.5
    for l in range(DEPTH):
        wl = w_in[l]
        wq = wl[:, :ATT_WIDTH] * scale
        wqkv = jnp.concatenate([wq, wl[:, ATT_WIDTH:QKV_W]], axis=1).astype(BF16)
        c0 = QKV_W
        wf = wl[:, c0:c0 + ATT_HEADS]
        wsmall = jnp.concatenate(
            [wl[:, c0 + ATT_HEADS:c0 + ATT_HEADS + POOL_WIDTH + SSM_WIDTH], wf,
             jnp.zeros((D, F_PAD - ATT_HEADS), F32)], axis=1).astype(BF16)
        wgates = wl[:, c0 + ATT_HEADS + POOL_WIDTH + SSM_WIDTH:].astype(BF16)

        q, k, v, u_pool, u_ssm_tm, f_pad = _proj(x2, wqkv, wsmall, B, S)

        f_rows = f_pad[:, :ATT_HEADS].reshape(B, S, ATT_HEADS).transpose(0, 2, 1).reshape(B * ATT_HEADS, S)
        b_rows = jnp.tile(b_forget[l], B).reshape(B * ATT_HEADS, 1)
        c = _fcum(f_rows, b_rows).reshape(B, ATT_HEADS, 1, S)
        y_a = _attention(q, k, v, c, B, S).reshape(N, ATT_WIDTH)

        w_pool_bd = _block_diag(w_pool[l]).astype(BF16)
        y_b = _pool(u_pool.reshape(B, S, POOL_WIDTH), w_pool_bd, pool_scale[l].reshape(1, POOL_WIDTH)).reshape(N, POOL_WIDTH)

        ar, ai, bbrT, bbiT = _ssm_prep(ssm_lambda_re[l], ssm_lambda_im[l], ssm_log_dt[l], ssm_b_re[l], ssm_b_im[l])
        bblk = jnp.concatenate([_block_diag(bbrT), _block_diag(bbiT)], axis=1).astype(BF16)
        cblk = jnp.concatenate([_block_diag(ssm_c_re[l].transpose(0, 2, 1)),
                                -_block_diag(ssm_c_im[l].transpose(0, 2, 1))], axis=0).astype(BF16)
        ar_b = jnp.broadcast_to(ar.reshape(1, SSM_STATES), (B, SSM_STATES))
        ai_b = jnp.broadcast_to(ai.reshape(1, SSM_STATES), (B, SSM_STATES))
        y_c_tm = _ssm(u_ssm_tm.reshape(S * B, SSM_WIDTH), bblk, ar_b, ai_b, cblk,
                      ssm_d[l].reshape(1, SSM_WIDTH), w_glu[l].astype(BF16), b_glu[l].reshape(1, SSM_WIDTH), B, S)

        x1 = _merge(x2, y_a, y_b, y_c_tm.reshape(S, B * SSM_WIDTH), wgates,
                    w_branch_a[l].astype(BF16), w_branch_b[l].astype(BF16), w_branch_c[l].astype(BF16),
                    w_out[l].astype(BF16), ln1_g[l].reshape(1, D), ln1_b[l].reshape(1, D), B, S)

        wr = jnp.concatenate([w_router[l], jnp.zeros((D, LANES - N_EXPERTS), F32)], axis=1)
        br = jnp.concatenate([b_router[l], jnp.full((LANES - N_EXPERTS,), _NEG_BIG, F32)]).reshape(1, LANES)
        gate, slot, slott, cnt = _route(x1, wr, br)
        tab, meta = _moe_tables(cnt, m_pad)
        xs = _dispatch(tab, x1, slott, m_pad)
        bu = b_up[l].reshape(N_EXPERTS, D_FF // LANES, LANES, 2).transpose(0, 1, 3, 2).reshape(N_EXPERTS, 1, 2 * D_FF)
        wu, wd = _expert_weight_prep(w_up, w_down, l, perm)
        yb = _experts(meta, xs, wu, bu, wd, b_down[l][:, None, :])
        x2 = _combine(tab, x1, gate, slot, ln2_g[l].reshape(1, D), ln2_b[l].reshape(1, D), yb)
    return x2.reshape(B, S, D)
```

```python
import functools
import math

import jax
import jax.numpy as jnp
import numpy as np
from jax import lax
from jax.experimental import pallas as pl
from jax.experimental.pallas import tpu as pltpu

F32 = jnp.float32
BF16 = jnp.bfloat16
I32 = jnp.int32
U32 = jnp.uint32

D_MODEL = 1024
DEPTH = 4
ATT_HEADS = 8
ATT_HEAD_DIM = 64
ATT_WIDTH = ATT_HEADS * ATT_HEAD_DIM
POOL_WINDOWS = (2, 4, 8, 16)
POOL_GROUPS = 4
POOL_WIDTH = 256
POOL_GROUP_DIM = 64
MAX_WINDOW = max(POOL_WINDOWS)
SSM_WIDTH = 256
SSM_GROUP_DIM = 16
SSM_GROUPS = 16
SSM_STATE = 64
SSM_STATES = SSM_GROUPS * SSM_STATE
N_BRANCH = 3
N_EXPERTS = 32
TOP_K = 4
D_FF = D_MODEL
SWIGLU_LIMIT = 7.0
SWIGLU_ALPHA = 1.702
LN_EPS = 1e-5
DEEPNORM_ALPHA = (2.0 * DEPTH) ** 0.25
GELU_C = math.sqrt(2.0 / math.pi)

LANES = 128
HALF_D = D_MODEL // 2
F_PAD = LANES
V_PAD = LANES
SMALL_W = POOL_WIDTH + SSM_WIDTH + F_PAD
QKV_W = 3 * ATT_WIDTH
TM_PROJ = 512
TQ_ATT = 1024
TK_ATT = 1024
assert TQ_ATT % TK_ATT == 0 or TK_ATT % TQ_ATT == 0
TM_POOL = 256
TT_SSM = 64
TM_MERGE = 256
SUBLANES = 8
TM_ROW = 256
ROWS_TILE = TM_ROW * TOP_K + N_EXPERTS * SUBLANES
GROUPS_TILE = ROWS_TILE // SUBLANES
TAB_W = 2 * LANES
assert TAB_W >= GROUPS_TILE + 2
BM_EXP = 512
VMEM_LIMIT = 52 * 1024 * 1024

_NEG_BIG = -1e30


def _cparams(sem, **kw):
    return pltpu.CompilerParams(dimension_semantics=sem, vmem_limit_bytes=VMEM_LIMIT, **kw)


def _sigmoid(x):
    return 1.0 / (1.0 + jnp.exp(-x))


def _layer_norm(z, g, b):
    mu = jnp.mean(z, axis=-1, keepdims=True)
    zc = z - mu
    var = jnp.mean(zc * zc, axis=-1, keepdims=True)
    return zc * lax.rsqrt(var + LN_EPS) * g + b


def _pack_rows(y):
    u = pltpu.bitcast(y.astype(BF16).astype(F32), U32)
    return u[:, :HALF_D] | (u[:, HALF_D:] >> 16)


def _unpack_rows(p):
    hi = pltpu.bitcast(p & jnp.uint32(0xFFFF0000), F32)
    lo = pltpu.bitcast(p << 16, F32)
    return hi, lo


def _proj_kernel(x_ref, wqkv_ref, ws_ref, q_ref, k_ref, v_ref, up_ref, us_ref, f_ref):
    xb = x_ref[...].astype(BF16)
    for part, ref in enumerate((q_ref, k_ref)):
        h = jnp.dot(xb, wqkv_ref[:, part * ATT_WIDTH:(part + 1) * ATT_WIDTH], preferred_element_type=F32)
        for hh in range(ATT_HEADS):
            ref[0, hh] = h[:, hh * ATT_HEAD_DIM:(hh + 1) * ATT_HEAD_DIM].astype(BF16)
    h = jnp.dot(xb, wqkv_ref[:, 2 * ATT_WIDTH:], preferred_element_type=F32)
    lane = lax.broadcasted_iota(I32, (h.shape[0], V_PAD), 1)
    tail = jnp.where(lane == ATT_HEAD_DIM, 1.0, 0.0)
    for pair in range(ATT_HEADS // 2):
        slab = h[:, pair * V_PAD:(pair + 1) * V_PAD]
        v_ref[0, 2 * pair] = jnp.where(lane < ATT_HEAD_DIM, slab, tail).astype(BF16)
        v_ref[0, 2 * pair + 1] = jnp.where(lane < ATT_HEAD_DIM, pltpu.roll(slab, ATT_HEAD_DIM, 1), tail).astype(BF16)
    hs = jnp.dot(xb, ws_ref[...], preferred_element_type=F32)
    up_ref[...] = hs[:, :POOL_WIDTH]
    us_ref[...] = hs[:, POOL_WIDTH:POOL_WIDTH + SSM_WIDTH]
    f_ref[...] = hs[:, POOL_WIDTH + SSM_WIDTH:]


def _proj(x2, wqkv, wsmall, B, S):
    N = B * S
    nS = S // TM_PROJ
    hm = jax.ShapeDtypeStruct((B, ATT_HEADS, S, ATT_HEAD_DIM), BF16)
    hm_spec = pl.BlockSpec((1, ATT_HEADS, TM_PROJ, ATT_HEAD_DIM), lambda b, s: (b, 0, s, 0))
    hv = jax.ShapeDtypeStruct((B, ATT_HEADS, S, V_PAD), BF16)
    hv_spec = pl.BlockSpec((1, ATT_HEADS, TM_PROJ, V_PAD), lambda b, s: (b, 0, s, 0))
    return pl.pallas_call(
        _proj_kernel,
        grid=(B, nS),
        in_specs=[
            pl.BlockSpec((TM_PROJ, D_MODEL), lambda b, s: (b * nS + s, 0)),
            pl.BlockSpec((D_MODEL, QKV_W), lambda b, s: (0, 0)),
            pl.BlockSpec((D_MODEL, SMALL_W), lambda b, s: (0, 0)),
        ],
        out_specs=[
            hm_spec, hm_spec, hv_spec,
            pl.BlockSpec((TM_PROJ, POOL_WIDTH), lambda b, s: (b * nS + s, 0)),
            pl.BlockSpec((TM_PROJ, SSM_WIDTH), lambda b, s: (s, b)),
            pl.BlockSpec((TM_PROJ, F_PAD), lambda b, s: (b * nS + s, 0)),
        ],
        out_shape=[
            hm, hm, hv,
            jax.ShapeDtypeStruct((N, POOL_WIDTH), F32),
            jax.ShapeDtypeStruct((S, B * SSM_WIDTH), F32),
            jax.ShapeDtypeStruct((N, F_PAD), F32),
        ],
        compiler_params=_cparams(("parallel", "parallel")),
        name="in_proj",
    )(x2, wqkv, wsmall)


def _fcum_kernel(f_ref, b_ref, c_ref):
    rows, S = f_ref.shape
    lane = lax.broadcasted_iota(I32, (rows, LANES), 1)
    carry = jnp.zeros((rows, 1), F32)
    for ch in range(S // LANES):
        z = f_ref[:, ch * LANES:(ch + 1) * LANES] + b_ref[...]
        lf = jnp.minimum(z, 0.0) - jnp.log1p(jnp.exp(-jnp.abs(z)))
        sh = 1
        while sh < LANES:
            lf = lf + jnp.where(lane >= sh, pltpu.roll(lf, sh, 1), 0.0)
            sh *= 2
        lf = lf + carry
        c_ref[:, ch * LANES:(ch + 1) * LANES] = lf
        carry = lf[:, LANES - 1:LANES]


def _fcum(f_rows, b_rows):
    rows, S = f_rows.shape
    return pl.pallas_call(
        _fcum_kernel,
        out_shape=jax.ShapeDtypeStruct((rows, S), F32),
        compiler_params=pltpu.CompilerParams(vmem_limit_bytes=VMEM_LIMIT),
        name="forget_cumsum",
    )(f_rows, b_rows)


def _attn_kernel(q_ref, k_ref, v_ref, c_ref, o_ref):
    TQ, TK = TQ_ATT, TK_ATT
    qi = pl.program_id(2)
    delta = lax.broadcasted_iota(I32, (TQ, TK), 1) - lax.broadcasted_iota(I32, (TQ, TK), 0)
    qs = [q_ref[0, hh] for hh in range(2)]

    def block(j, carry, masked):
        start = pl.multiple_of(j * TK, TK)
        out = []
        for hh in range(2):
            m, acc = carry[2 * hh], carry[2 * hh + 1]
            k = k_ref[0, hh, pl.ds(start, TK), :]
            v = v_ref[0, hh, pl.ds(start, TK), :]
            s = lax.dot_general(qs[hh], k, (((1,), (1,)), ((), ())), preferred_element_type=F32)
            s = s - c_ref[0, hh, :, pl.ds(start, TK)]
            if masked:
                s = jnp.where(delta <= qi * TQ - start, s, -jnp.inf)
            m_new = jnp.maximum(m, jnp.max(s, axis=-1, keepdims=True))
            p = jnp.exp(s - m_new)
            acc = jnp.exp(m - m_new) * acc + jnp.dot(p.astype(BF16), v, preferred_element_type=F32)
            out += [m_new, acc]
        return tuple(out)

    m0 = jnp.full((TQ, 1), -jnp.inf, F32)
    acc0 = jnp.zeros((TQ, V_PAD), F32)
    n_full = (qi * TQ) // TK
    carry = lax.fori_loop(0, n_full, functools.partial(block, masked=False), (m0, acc0, m0, acc0))
    for d in range(max(1, TQ // TK)):
        carry = block(n_full + d, carry, True)
    for hh in range(2):
        acc = carry[2 * hh + 1]
        out = acc[:, :ATT_HEAD_DIM] / acc[:, ATT_HEAD_DIM:ATT_HEAD_DIM + 1]
        o_ref[0, :, hh * ATT_HEAD_DIM:(hh + 1) * ATT_HEAD_DIM] = out.astype(BF16)


def _attention(q, k, v, c, B, S):
    nq = S // TQ_ATT
    qspec = pl.BlockSpec((1, 2, TQ_ATT, ATT_HEAD_DIM), lambda b, hp, i: (b, hp, i, 0))
    kspec = pl.BlockSpec((1, 2, S, ATT_HEAD_DIM), lambda b, hp, i: (b, hp, 0, 0))
    vspec = pl.BlockSpec((1, 2, S, V_PAD), lambda b, hp, i: (b, hp, 0, 0))
    return pl.pallas_call(
        _attn_kernel,
        grid=(B, ATT_HEADS // 2, nq),
        in_specs=[qspec, kspec, vspec,
                  pl.BlockSpec((1, 2, 1, S), lambda b, hp, i: (b, hp, 0, 0))],
        out_specs=pl.BlockSpec((1, TQ_ATT, 2 * ATT_HEAD_DIM), lambda b, hp, i: (b, i, hp)),
        out_shape=jax.ShapeDtypeStruct((B, S, ATT_WIDTH), BF16),
        compiler_params=_cparams(("parallel", "parallel", "arbitrary")),
        name="fox_attention",
    )(q, k, v, c)


def _pool_kernel(u_ref, w_ref, sc_ref, o_ref, pad_ref):
    S = u_ref.shape[1]
    R = TM_POOL
    pad_ref[0:MAX_WINDOW, :] = jnp.zeros((MAX_WINDOW, POOL_WIDTH), F32)
    pad_ref[MAX_WINDOW:, :] = u_ref[0]
    lane = lax.broadcasted_iota(I32, (R, POOL_WIDTH), 1)
    trow = lax.broadcasted_iota(I32, (R, POOL_WIDTH), 0)
    grp = lane // POOL_GROUP_DIM
    win = jnp.where(grp == 0, POOL_WINDOWS[0],
                    jnp.where(grp == 1, POOL_WINDOWS[1], jnp.where(grp == 2, POOL_WINDOWS[2], POOL_WINDOWS[3])))
    for i in range(S // R):
        base = MAX_WINDOW + i * R
        u0 = pad_ref[base:base + R, :]
        acc = u0
        sums = {}
        for kk in range(1, MAX_WINDOW):
            acc = acc + pad_ref[base - kk:base - kk + R, :]
            if kk + 1 in POOL_WINDOWS:
                sums[kk + 1] = acc
        total = jnp.where(grp == 0, sums[POOL_WINDOWS[0]],
                          jnp.where(grp == 1, sums[POOL_WINDOWS[1]],
                                    jnp.where(grp == 2, sums[POOL_WINDOWS[2]], sums[POOL_WINDOWS[3]])))
        cnt = jnp.minimum(trow + (i * R + 1), win).astype(F32)
        mixed = total / cnt - u0
        y = jnp.dot(mixed.astype(BF16), w_ref[...], preferred_element_type=F32) * sc_ref[...]
        o_ref[0, i * R:(i + 1) * R, :] = y.astype(BF16)


def _pool(u3, w_bd, scale):
    B, S, _ = u3.shape
    return pl.pallas_call(
        _pool_kernel,
        grid=(B,),
        in_specs=[pl.BlockSpec((1, S, POOL_WIDTH), lambda b: (b, 0, 0)),
                  pl.BlockSpec((POOL_WIDTH, POOL_WIDTH), lambda b: (0, 0)),
                  pl.BlockSpec((1, POOL_WIDTH), lambda b: (0, 0))],
        out_specs=pl.BlockSpec((1, S, POOL_WIDTH), lambda b: (b, 0, 0)),
        out_shape=jax.ShapeDtypeStruct((B, S, POOL_WIDTH), BF16),
        scratch_shapes=[pltpu.VMEM((S + MAX_WINDOW, POOL_WIDTH), F32)],
        compiler_params=_cparams(("parallel",)),
        name="multiscale_pool",
    )(u3, w_bd, scale)


def _ssm_prep_kernel(lr_ref, li_ref, ldt_ref, brT_ref, biT_ref, ar_ref, ai_ref, bbr_ref, bbi_ref):
    lr = lr_ref[...]
    li = li_ref[...]
    dt = jnp.exp(ldt_ref[...])
    mag = jnp.exp(lr * dt)
    ar = mag * jnp.cos(li * dt)
    ai = mag * jnp.sin(li * dt)
    den = lr * lr + li * li
    nr = ar - 1.0
    zr = (nr * lr + ai * li) / den
    zi = (ai * lr - nr * li) / den
    ar_ref[...] = ar
    ai_ref[...] = ai
    br = brT_ref[...]
    bi = biT_ref[...]
    bbr_ref[...] = zr[:, None, :] * br - zi[:, None, :] * bi
    bbi_ref[...] = zr[:, None, :] * bi + zi[:, None, :] * br


def _ssm_prep(lr, li, log_dt, b_re, b_im):
    G, P, H = b_re.shape
    gp = jax.ShapeDtypeStruct((G, P), F32)
    ghp = jax.ShapeDtypeStruct((G, H, P), F32)
    return pl.pallas_call(
        _ssm_prep_kernel,
        out_shape=[gp, gp, ghp, ghp],
        name="s5_discretise",
    )(lr, li, log_dt.reshape(G, 1), b_re.transpose(0, 2, 1), b_im.transpose(0, 2, 1))


def _gelu_tanh(y):
    return 0.5 * y * (1.0 + jnp.tanh(GELU_C * (y + 0.044715 * (y * y * y))))


def _ssm_kernel(u_ref, bb_ref, ar_ref, ai_ref, cc_ref, d_ref, wg_ref, bg_ref, o_ref, x_scr, hr_scr, hi_scr):
    nb = hr_scr.shape[0]
    tt = u_ref.shape[0] // nb

    @pl.when(pl.program_id(0) == 0)
    def _():
        hr_scr[...] = jnp.zeros_like(hr_scr)
        hi_scr[...] = jnp.zeros_like(hi_scr)

    u = u_ref[...]
    x_scr[...] = jnp.dot(u.astype(BF16), bb_ref[...], preferred_element_type=F32)

    def step(t, carry):
        hr, hi = carry
        r0 = pl.multiple_of(t * nb, nb)
        xr = x_scr[pl.ds(r0, nb), 0:SSM_STATES]
        xi = x_scr[pl.ds(r0, nb), SSM_STATES:2 * SSM_STATES]
        ar = ar_ref[...]
        ai = ai_ref[...]
        nr = ar * hr - ai * hi + xr
        ni = ar * hi + ai * hr + xi
        x_scr[pl.ds(r0, nb), 0:SSM_STATES] = nr
        x_scr[pl.ds(r0, nb), SSM_STATES:2 * SSM_STATES] = ni
        return nr, ni

    hr, hi = lax.fori_loop(0, tt, step, (hr_scr[...], hi_scr[...]))
    hr_scr[...] = hr
    hi_scr[...] = hi
    y = jnp.dot(x_scr[...].astype(BF16), cc_ref[...], preferred_element_type=F32) + d_ref[...] * u
    y = _gelu_tanh(y)
    gl = jnp.dot(y.astype(BF16), wg_ref[...], preferred_element_type=F32) + bg_ref[...]
    o_ref[...] = (y * _sigmoid(gl)).astype(BF16)


def _ssm(u_tm, bblk, ar_b, ai_b, cblk, dvec, wglu, bglu, B, S):
    rows = TT_SSM * B
    const = lambda shape: pl.BlockSpec(shape, lambda i: (0, 0))
    return pl.pallas_call(
        _ssm_kernel,
        grid=(S // TT_SSM,),
        in_specs=[pl.BlockSpec((rows, SSM_WIDTH), lambda i: (i, 0)),
                  const((SSM_WIDTH, 2 * SSM_STATES)),
                  const((B, SSM_STATES)), const((B, SSM_STATES)),
                  const((2 * SSM_STATES, SSM_WIDTH)),
                  const((1, SSM_WIDTH)), const((SSM_WIDTH, SSM_WIDTH)), const((1, SSM_WIDTH))],
        out_specs=pl.BlockSpec((rows, SSM_WIDTH), lambda i: (i, 0)),
        out_shape=jax.ShapeDtypeStruct((S * B, SSM_WIDTH), BF16),
        scratch_shapes=[pltpu.VMEM((rows, 2 * SSM_STATES), F32),
                        pltpu.VMEM((B, SSM_STATES), F32), pltpu.VMEM((B, SSM_STATES), F32)],
        compiler_params=_cparams(("arbitrary",)),
        name="s5_scan",
    )(u_tm, bblk, ar_b, ai_b, cblk, dvec, wglu, bglu)


def _merge_kernel(x_ref, ya_ref, yb_ref, yc_ref, wg_ref, wa_ref, wb_ref, wc_ref, wo_ref, g_ref, b_ref, o_ref):
    x = x_ref[...]
    xb = x.astype(BF16)
    merged = None
    for i, (y_ref, w_ref) in enumerate(((ya_ref, wa_ref), (yb_ref, wb_ref), (yc_ref, wc_ref))):
        gate = _sigmoid(jnp.dot(xb, wg_ref[:, i * D_MODEL:(i + 1) * D_MODEL], preferred_element_type=F32))
        term = gate * jnp.dot(y_ref[...], w_ref[...], preferred_element_type=F32)
        merged = term if merged is None else merged + term
    mix = jnp.dot(merged.astype(BF16), wo_ref[...], preferred_element_type=F32)
    o_ref[...] = _layer_norm(DEEPNORM_ALPHA * x + mix, g_ref[...], b_ref[...])


def _merge(x2, ya, yb, yc_tm, wg, wa, wb, wc, wo, g, b, B, S):
    N = B * S
    nS = S // TM_MERGE
    const = lambda shape: pl.BlockSpec(shape, lambda bb, s: (0, 0))
    row = lambda w: pl.BlockSpec((TM_MERGE, w), lambda bb, s: (bb * nS + s, 0))
    return pl.pallas_call(
        _merge_kernel,
        grid=(B, nS),
        in_specs=[row(D_MODEL), row(ATT_WIDTH), row(POOL_WIDTH),
                  pl.BlockSpec((TM_MERGE, SSM_WIDTH), lambda bb, s: (s, bb)),
                  const((D_MODEL, N_BRANCH * D_MODEL)), const((ATT_WIDTH, D_MODEL)),
                  const((POOL_WIDTH, D_MODEL)), const((SSM_WIDTH, D_MODEL)), const((D_MODEL, D_MODEL)),
                  const((1, D_MODEL)), const((1, D_MODEL))],
        out_specs=row(D_MODEL),
        out_shape=jax.ShapeDtypeStruct((N, D_MODEL), F32),
        compiler_params=_cparams(("parallel", "parallel")),
        name="merge_ln1",
    )(x2, ya, yb, yc_tm, wg, wa, wb, wc, wo, g, b)


def _route_kernel(x_ref, wr_ref, br_ref, gate_ref, slot_ref, slott_ref, cnt_ref):
    tm = x_ref.shape[0]
    x = x_ref[...]
    xh = x.astype(BF16)
    xl = (x - xh.astype(F32)).astype(BF16)
    w = wr_ref[...]
    wh = w.astype(BF16)
    wl = (w - wh.astype(F32)).astype(BF16)
    logits = (jnp.dot(xh, wh, preferred_element_type=F32) + jnp.dot(xl, wh, preferred_element_type=F32)
              + jnp.dot(xh, wl, preferred_element_type=F32)) + br_ref[...]
    lane = lax.broadcasted_iota(I32, (tm, LANES), 1)
    lane_f = lane.astype(F32)
    work = logits
    vals, idxs = [], []
    multihot = jnp.zeros((tm, LANES), F32)
    for _ in range(TOP_K):
        mx = jnp.max(work, axis=-1, keepdims=True)
        ix = jnp.min(jnp.where(work == mx, lane_f, float(LANES)), axis=-1, keepdims=True)
        sel = lane_f == ix
        work = jnp.where(sel, -jnp.inf, work)
        multihot = multihot + sel.astype(F32)
        vals.append(mx)
        idxs.append(ix)
    exps = [jnp.exp(v - vals[0]) for v in vals]
    denom = exps[0] + exps[1] + exps[2] + exps[3]
    r = lax.broadcasted_iota(I32, (tm, tm), 0)
    c = lax.broadcasted_iota(I32, (tm, tm), 1)
    lower = jnp.where(c < r, 1.0, 0.0).astype(BF16)
    before = jnp.dot(lower, multihot.astype(BF16), preferred_element_type=F32)
    n = jnp.sum(multihot, axis=0, keepdims=True)
    n_pad = jnp.floor((n + float(SUBLANES - 1)) * (1.0 / SUBLANES)) * float(SUBLANES)
    lane8 = lax.broadcasted_iota(I32, (SUBLANES, LANES), 1)
    incl = jnp.broadcast_to(n_pad, (SUBLANES, LANES))
    sh = 1
    while sh < LANES:
        incl = incl + jnp.where(lane8 >= sh, pltpu.roll(incl, sh, 1), 0.0)
        sh *= 2
    run_start = incl[0:1, :] - n_pad
    where_in_tile = before + run_start
    gate_out = jnp.zeros((tm, LANES), F32)
    slot_out = jnp.zeros((tm, LANES), F32)
    for kk in range(TOP_K):
        sl = jnp.sum(jnp.where(lane_f == idxs[kk], where_in_tile, 0.0), axis=-1, keepdims=True)
        here = lane == kk
        gate_out = jnp.where(here, exps[kk] / denom, gate_out)
        slot_out = jnp.where(here, sl, slot_out)
    gate_ref[...] = gate_out
    slot_ref[...] = slot_out.astype(I32)
    slott_ref[0] = slot_out.T[0:SUBLANES, :].astype(I32)
    cnt_ref[0] = n_pad.astype(I32)


def _route(x1, wr, br):
    N = x1.shape[0]
    nt = N // TM_ROW
    row = pl.BlockSpec((TM_ROW, LANES), lambda i: (i, 0))
    return pl.pallas_call(
        _route_kernel,
        grid=(nt,),
        in_specs=[pl.BlockSpec((TM_ROW, D_MODEL), lambda i: (i, 0)),
                  pl.BlockSpec((D_MODEL, LANES), lambda i: (0, 0)),
                  pl.BlockSpec((1, LANES), lambda i: (0, 0))],
        out_specs=[row, row,
                   pl.BlockSpec((1, SUBLANES, TM_ROW), lambda i: (i, 0, 0)),
                   pl.BlockSpec((1, 1, LANES), lambda i: (i, 0, 0))],
        out_shape=[jax.ShapeDtypeStruct((N, LANES), F32), jax.ShapeDtypeStruct((N, LANES), I32),
                   jax.ShapeDtypeStruct((nt, SUBLANES, TM_ROW), I32),
                   jax.ShapeDtypeStruct((nt, 1, LANES), I32)],
        compiler_params=_cparams(("parallel",)),
        name="router_top4",
    )(x1, wr, br)


def _group_copies(tab_ref, make_copy):
    n_groups = tab_ref[0, 0, 0]

    def issue_one(u, _):
        local = pl.multiple_of(u * SUBLANES, SUBLANES)
        remote = pl.multiple_of(tab_ref[0, 0, 1 + u], SUBLANES)
        make_copy(local, remote, SUBLANES).start()
        return 0

    def drain():
        @pl.when(n_groups > 0)
        def _():
            make_copy(0, 0, n_groups * SUBLANES).wait()

    return (lambda: lax.fori_loop(0, n_groups, issue_one, 0)), drain


def _dispatch_kernel(tab_ref, x_ref, slott_ref, xs_hbm, srt_scr, sem):
    tm = x_ref.shape[0]
    r = lax.broadcasted_iota(I32, (ROWS_TILE, tm), 0)
    hit = r == slott_ref[0, 0:1, :]
    for kk in range(1, TOP_K):
        hit = hit | (r == slott_ref[0, kk:kk + 1, :])
    sel = jnp.where(hit, 1.0, 0.0).astype(BF16)
    srt = jnp.dot(sel, x_ref[...].astype(BF16), preferred_element_type=F32)
    srt_scr[...] = _pack_rows(srt)
    issue, drain = _group_copies(
        tab_ref, lambda local, remote, rows: pltpu.make_async_copy(
            srt_scr.at[pl.ds(local, rows)], xs_hbm.at[pl.ds(remote, rows)], sem))
    issue()
    drain()

    @pl.when(pl.program_id(0) == pl.num_programs(0) - 1)
    def _():
        used = pl.multiple_of(tab_ref[0, 0, TAB_W - 1], SUBLANES)
        free = xs_hbm.shape[0] - used
        n_big = free // BM_EXP
        n_small = (free - n_big * BM_EXP) // SUBLANES
        srt_scr[0:BM_EXP, :] = jnp.zeros((BM_EXP, HALF_D), U32)

        def big(i):
            return pltpu.make_async_copy(
                srt_scr.at[pl.ds(0, BM_EXP)],
                xs_hbm.at[pl.ds(pl.multiple_of(used + i * BM_EXP, SUBLANES), BM_EXP)], sem)

        def small(i):
            return pltpu.make_async_copy(
                srt_scr.at[pl.ds(0, SUBLANES)],
                xs_hbm.at[pl.ds(pl.multiple_of(used + n_big * BM_EXP + i * SUBLANES, SUBLANES), SUBLANES)], sem)

        def for_each(make, count, op):
            def body(i, c):
                op(make(i))
                return c
            lax.fori_loop(0, count, body, 0)

        for op in (lambda cp: cp.start(), lambda cp: cp.wait()):
            for_each(big, n_big, op)
            for_each(small, n_small, op)


def _dispatch(tab, x1, slott, m_pad):
    N = x1.shape[0]
    return pl.pallas_call(
        _dispatch_kernel,
        grid=(N // TM_ROW,),
        in_specs=[pl.BlockSpec((1, 1, TAB_W), lambda i: (i, 0, 0), memory_space=pltpu.SMEM),
                  pl.BlockSpec((TM_ROW, D_MODEL), lambda i: (i, 0)),
                  pl.BlockSpec((1, SUBLANES, TM_ROW), lambda i: (i, 0, 0))],
        out_specs=pl.BlockSpec(memory_space=pl.ANY),
        out_shape=jax.ShapeDtypeStruct((m_pad, HALF_D), U32),
        scratch_shapes=[pltpu.VMEM((ROWS_TILE, HALF_D), U32), pltpu.SemaphoreType.DMA],
        compiler_params=_cparams(("arbitrary",), has_side_effects=True),
        name="moe_dispatch",
    )(tab, x1, slott)


def _expert_kernel(blk_ref, exp_ref, lo_ref, hi_ref, xs_ref, wu_ref, bu_ref, wd_ref, bd_ref, o_ref):
    w = pl.program_id(0)
    lo = lo_ref[w]
    hi = hi_ref[w]

    @pl.when(lo > hi)
    def _():
        o_ref[...] = jnp.zeros_like(o_ref)

    @pl.when(hi > lo)
    def _():
        xh, xl = _unpack_rows(xs_ref[...])
        x = jnp.concatenate([xh.astype(BF16), xl.astype(BF16)], axis=1)
        hu = jnp.dot(x, wu_ref[0], preferred_element_type=F32) + bu_ref[0]
        chunks = []
        for cc in range(D_FF // LANES):
            g = jnp.minimum(hu[:, 2 * cc * LANES:(2 * cc + 1) * LANES], SWIGLU_LIMIT)
            l = jnp.clip(hu[:, (2 * cc + 1) * LANES:(2 * cc + 2) * LANES], -SWIGLU_LIMIT, SWIGLU_LIMIT)
            chunks.append(g * _sigmoid(SWIGLU_ALPHA * g) * (l + 1.0))
        act = jnp.concatenate(chunks, axis=1)
        y = jnp.dot(act.astype(BF16), wd_ref[0], preferred_element_type=F32) + bd_ref[0]
        packed = _pack_rows(y)

        @pl.when(lo == 0)
        def _():
            o_ref[...] = packed

        @pl.when(lo > 0)
        def _():
            row = lax.broadcasted_iota(I32, packed.shape, 0)
            o_ref[...] = jnp.where((row >= lo) & (row < hi), packed, o_ref[...])


def _expert_weight_prep_kernel(wu_ref, wd_ref, perm_ref, ou_ref, od_ref):
    for cc in range(2 * D_FF // (2 * LANES)):
        cols = slice(cc * 2 * LANES, (cc + 1) * 2 * LANES)
        ou_ref[0, :, cols] = jnp.dot(wu_ref[0, 0, :, cols].astype(BF16), perm_ref[...],
                                     preferred_element_type=F32).astype(BF16)
    od_ref[0] = wd_ref[0, 0].astype(BF16)


def _expert_weight_prep(w_up, w_down, layer, perm):
    E = w_up.shape[1]
    return pl.pallas_call(
        _expert_weight_prep_kernel,
        grid=(E,),
        in_specs=[pl.BlockSpec((1, 1, D_MODEL, 2 * D_FF), lambda e: (layer, e, 0, 0)),
                  pl.BlockSpec((1, 1, D_FF, D_MODEL), lambda e: (layer, e, 0, 0)),
                  pl.BlockSpec((2 * LANES, 2 * LANES), lambda e: (0, 0))],
        out_specs=[pl.BlockSpec((1, D_MODEL, 2 * D_FF), lambda e: (e, 0, 0)),
                   pl.BlockSpec((1, D_FF, D_MODEL), lambda e: (e, 0, 0))],
        out_shape=[jax.ShapeDtypeStruct((E, D_MODEL, 2 * D_FF), BF16),
                   jax.ShapeDtypeStruct((E, D_FF, D_MODEL), BF16)],
        compiler_params=_cparams(("parallel",)),
        name="expert_weight_prep",
    )(w_up, w_down, perm)


def _pair_perm():
    p = np.zeros((2 * LANES, 2 * LANES), np.float32)
    j = np.arange(LANES)
    p[2 * j, j] = 1.0
    p[2 * j + 1, LANES + j] = 1.0
    return jnp.asarray(p, BF16)


def _experts(meta, xs, wu, bu, wd, bd):
    M = xs.shape[0]
    W = meta[0].shape[0]
    by_expert = lambda *shape: pl.BlockSpec((1,) + shape, lambda w, blk, ex, lo, hi: (ex[w], 0, 0))
    rows = pl.BlockSpec((BM_EXP, HALF_D), lambda w, blk, ex, lo, hi: (blk[w], 0))
    return pl.pallas_call(
        _expert_kernel,
        grid_spec=pltpu.PrefetchScalarGridSpec(
            num_scalar_prefetch=4,
            grid=(W,),
            in_specs=[rows, by_expert(D_MODEL, 2 * D_FF), by_expert(1, 2 * D_FF),
                      by_expert(D_FF, D_MODEL), by_expert(1, D_MODEL)],
            out_specs=rows,
        ),
        out_shape=jax.ShapeDtypeStruct((M, HALF_D), U32),
        compiler_params=_cparams(("arbitrary",)),
        name="moe_experts",
    )(*meta, xs, wu, bu, wd, bd)


def _combine_kernel(tab_ref, x_ref, gate_ref, slot_ref, g_ref, b_ref, yb_hbm, o_ref, buf, sem):
    tm = x_ref.shape[0]

    @pl.when(pl.program_id(0) == 0)
    def _():
        buf[...] = jnp.zeros_like(buf)

    issue, drain = _group_copies(
        tab_ref, lambda local, remote, rows: pltpu.make_async_copy(
            yb_hbm.at[pl.ds(remote, rows)], buf.at[pl.ds(local, rows)], sem))
    issue()
    r = lax.broadcasted_iota(I32, (tm, ROWS_TILE), 1)
    slots = slot_ref[...]
    gates = gate_ref[...]
    pw = jnp.zeros((tm, ROWS_TILE), F32)
    for kk in range(TOP_K):
        pw = jnp.where(r == slots[:, kk:kk + 1], gates[:, kk:kk + 1], pw)
    pw_hi = pw.astype(BF16)
    pw_lo = (pw - pw_hi.astype(F32)).astype(BF16)
    drain()
    yh, yl = _unpack_rows(buf[...])
    ys = jnp.concatenate([yh.astype(BF16), yl.astype(BF16)], axis=1)
    y = jnp.dot(pw_hi, ys, preferred_element_type=F32) + jnp.dot(pw_lo, ys, preferred_element_type=F32)
    o_ref[...] = _layer_norm(DEEPNORM_ALPHA * x_ref[...] + y, g_ref[...], b_ref[...])


def _combine(tab, x1, gate, slot, g, b, yb):
    N = x1.shape[0]
    return pl.pallas_call(
        _combine_kernel,
        grid=(N // TM_ROW,),
        in_specs=[pl.BlockSpec((1, 1, TAB_W), lambda i: (i, 0, 0), memory_space=pltpu.SMEM),
                  pl.BlockSpec((TM_ROW, D_MODEL), lambda i: (i, 0)),
                  pl.BlockSpec((TM_ROW, LANES), lambda i: (i, 0)),
                  pl.BlockSpec((TM_ROW, LANES), lambda i: (i, 0)),
                  pl.BlockSpec((1, D_MODEL), lambda i: (0, 0)),
                  pl.BlockSpec((1, D_MODEL), lambda i: (0, 0)),
                  pl.BlockSpec(memory_space=pl.ANY)],
        out_specs=pl.BlockSpec((TM_ROW, D_MODEL), lambda i: (i, 0)),
        out_shape=jax.ShapeDtypeStruct((N, D_MODEL), F32),
        scratch_shapes=[pltpu.VMEM((ROWS_TILE, HALF_D), U32), pltpu.SemaphoreType.DMA],
        compiler_params=_cparams(("arbitrary",)),
        name="moe_combine_ln2",
    )(tab, x1, gate, slot, g, b, yb)


def _moe_tables(cnt, m_pad):
    n_pad = cnt[:, 0, :N_EXPERTS]
    meta, region = _work_items(n_pad.sum(axis=0), m_pad)
    run_end = jnp.cumsum(n_pad, axis=1)
    run_start = run_end - n_pad
    hbm_start = region[None, :] + jnp.cumsum(n_pad, axis=0) - n_pad
    first_row = SUBLANES * jnp.arange(GROUPS_TILE, dtype=I32)
    owner = jnp.minimum((run_end[:, None, :] <= first_row[None, :, None]).sum(axis=-1), N_EXPERTS - 1)
    sel = owner[:, :, None] == jnp.arange(N_EXPERTS, dtype=I32)[None, None, :]
    shift = jnp.sum(jnp.where(sel, (hbm_start - run_start)[:, None, :], 0), axis=-1)
    n_groups = run_end[:, -1:] // SUBLANES
    pad = jnp.zeros((n_pad.shape[0], TAB_W - 2 - GROUPS_TILE), I32)
    used = jnp.broadcast_to(n_pad.sum(), n_groups.shape)
    tab = jnp.concatenate([n_groups, shift + first_row[None, :], pad, used], axis=1).astype(I32)
    return tab[:, None, :], meta


def _work_items(counts, M):
    nblk = M // BM_EXP
    W = nblk + N_EXPERTS
    ends = jnp.cumsum(counts)
    starts = ends - counts
    first = starts // BM_EXP
    last = jnp.maximum(ends - 1, 0) // BM_EXP
    n_items = jnp.where(counts > 0, last - first + 1, 0)
    item_end = jnp.cumsum(n_items)
    item_start = item_end - n_items
    total = item_end[-1]
    w = jnp.arange(W, dtype=I32)
    w_eff = jnp.minimum(w, total - 1)
    owner = (item_end[None, :] <= w_eff[:, None]).sum(axis=1).astype(I32)
    sel = owner[:, None] == jnp.arange(N_EXPERTS, dtype=I32)[None, :]
    pick = lambda table: jnp.sum(jnp.where(sel, table[None, :], 0), axis=1)
    blk = pick(first) + (w_eff - pick(item_start))
    lo = jnp.clip(pick(starts) - blk * BM_EXP, 0, BM_EXP)
    hi = jnp.clip(pick(ends) - blk * BM_EXP, 0, BM_EXP)
    valid = w < total
    used_blocks = (ends[-1] + BM_EXP - 1) // BM_EXP
    spare_blk = used_blocks + (w - total)
    fill = (~valid) & (spare_blk < nblk)
    blk = jnp.where(valid, blk, jnp.minimum(spare_blk, nblk - 1))
    lo = jnp.where(valid, lo, jnp.where(fill, 1, 0))
    hi = jnp.where(valid, hi, 0)
    return (blk.astype(I32), owner, lo.astype(I32), hi.astype(I32)), starts


def _block_diag(blocks):
    G, a, b = blocks.shape
    eye = jnp.eye(G, dtype=blocks.dtype)
    return jnp.einsum("gab,gh->gahb", blocks, eye).reshape(G * a, G * b)


def kernel(x, w_in, b_forget, w_pool, pool_scale, ssm_lambda_re, ssm_lambda_im, ssm_log_dt, ssm_b_re, ssm_b_im, ssm_c_re, ssm_c_im, ssm_d, w_glu, b_glu, w_branch_a, w_branch_b, w_branch_c, w_out, ln1_g, ln1_b, w_router, b_router, w_up, b_up, w_down, b_down, ln2_g, ln2_b):
    B, S, D = x.shape
    assert D == D_MODEL and S % TQ_ATT == 0 and S % TK_ATT == 0 and S % TM_PROJ == 0
    N = B * S
    m_pad = (N // TM_ROW) * ROWS_TILE
    assert m_pad % BM_EXP == 0
    perm = _pair_perm()
    x2 = x.reshape(N, D)
    scale = ATT_HEAD_DIM ** -0.5
    for l in range(DEPTH):
        wl = w_in[l]
        wq = wl[:, :ATT_WIDTH] * scale
        wqkv = jnp.concatenate([wq, wl[:, ATT_WIDTH:QKV_W]], axis=1).astype(BF16)
        c0 = QKV_W
        wf = wl[:, c0:c0 + ATT_HEADS]
        wsmall = jnp.concatenate(
            [wl[:, c0 + ATT_HEADS:c0 + ATT_HEADS + POOL_WIDTH + SSM_WIDTH], wf,
             jnp.zeros((D, F_PAD - ATT_HEADS), F32)], axis=1).astype(BF16)
        wgates = wl[:, c0 + ATT_HEADS + POOL_WIDTH + SSM_WIDTH:].astype(BF16)

        q, k, v, u_pool, u_ssm_tm, f_pad = _proj(x2, wqkv, wsmall, B, S)

        f_rows = f_pad[:, :ATT_HEADS].reshape(B, S, ATT_HEADS).transpose(0, 2, 1).reshape(B * ATT_HEADS, S)
        b_rows = jnp.tile(b_forget[l], B).reshape(B * ATT_HEADS, 1)
        c = _fcum(f_rows, b_rows).reshape(B, ATT_HEADS, 1, S)
        y_a = _attention(q, k, v, c, B, S).reshape(N, ATT_WIDTH)

        w_pool_bd = _block_diag(w_pool[l]).astype(BF16)
        y_b = _pool(u_pool.reshape(B, S, POOL_WIDTH), w_pool_bd, pool_scale[l].reshape(1, POOL_WIDTH)).reshape(N, POOL_WIDTH)

        ar, ai, bbrT, bbiT = _ssm_prep(ssm_lambda_re[l], ssm_lambda_im[l], ssm_log_dt[l], ssm_b_re[l], ssm_b_im[l])
        bblk = jnp.concatenate([_block_diag(bbrT), _block_diag(bbiT)], axis=1).astype(BF16)
        cblk = jnp.concatenate([_block_diag(ssm_c_re[l].transpose(0, 2, 1)),
                                -_block_diag(ssm_c_im[l].transpose(0, 2, 1))], axis=0).astype(BF16)
        ar_b = jnp.broadcast_to(ar.reshape(1, SSM_STATES), (B, SSM_STATES))
        ai_b = jnp.broadcast_to(ai.reshape(1, SSM_STATES), (B, SSM_STATES))
        y_c_tm = _ssm(u_ssm_tm.reshape(S * B, SSM_WIDTH), bblk, ar_b, ai_b, cblk,
                      ssm_d[l].reshape(1, SSM_WIDTH), w_glu[l].astype(BF16), b_glu[l].reshape(1, SSM_WIDTH), B, S)

        x1 = _merge(x2, y_a, y_b, y_c_tm.reshape(S, B * SSM_WIDTH), wgates,
                    w_branch_a[l].astype(BF16), w_branch_b[l].astype(BF16), w_branch_c[l].astype(BF16),
                    w_out[l].astype(BF16), ln1_g[l].reshape(1, D), ln1_b[l].reshape(1, D), B, S)

        wr = jnp.concatenate([w_router[l], jnp.zeros((D, LANES - N_EXPERTS), F32)], axis=1)
        br = jnp.concatenate([b_router[l], jnp.full((LANES - N_EXPERTS,), _NEG_BIG, F32)]).reshape(1, LANES)
        gate, slot, slott, cnt = _route(x1, wr, br)
        tab, meta = _moe_tables(cnt, m_pad)
        xs = _dispatch(tab, x1, slott, m_pad)
        bu = b_up[l].reshape(N_EXPERTS, D_FF // LANES, LANES, 2).transpose(0, 1, 3, 2).reshape(N_EXPERTS, 1, 2 * D_FF)
        wu, wd = _expert_weight_prep(w_up, w_down, l, perm)
        yb = _experts(meta, xs, wu, bu, wd, b_down[l][:, None, :])
        x2 = _combine(tab, x1, gate, slot, ln2_g[l].reshape(1, D), ln2_b[l].reshape(1, D), yb)
    return x2.reshape(B, S, D)
```

```python
import functools
import math

import jax
import jax.numpy as jnp
import numpy as np
from jax import lax
from jax.experimental import pallas as pl
from jax.experimental.pallas import tpu as pltpu

F32 = jnp.float32
BF16 = jnp.bfloat16
I32 = jnp.int32
U32 = jnp.uint32

D_MODEL = 1024
DEPTH = 4
ATT_HEADS = 8
ATT_HEAD_DIM = 64
ATT_WIDTH = ATT_HEADS * ATT_HEAD_DIM
POOL_WINDOWS = (2, 4, 8, 16)
POOL_GROUPS = 4
POOL_WIDTH = 256
POOL_GROUP_DIM = 64
MAX_WINDOW = max(POOL_WINDOWS)
SSM_WIDTH = 256
SSM_GROUP_DIM = 16
SSM_GROUPS = 16
SSM_STATE = 64
SSM_STATES = SSM_GROUPS * SSM_STATE
N_BRANCH = 3
N_EXPERTS = 32
TOP_K = 4
D_FF = D_MODEL
SWIGLU_LIMIT = 7.0
SWIGLU_ALPHA = 1.702
LN_EPS = 1e-5
DEEPNORM_ALPHA = (2.0 * DEPTH) ** 0.25
GELU_C = math.sqrt(2.0 / math.pi)

LANES = 128
HALF_D = D_MODEL // 2
F_PAD = LANES
V_PAD = LANES
SMALL_W = POOL_WIDTH + SSM_WIDTH + F_PAD
QKV_W = 3 * ATT_WIDTH
TM_PROJ = 512
TQ_ATT = 1024
TK_ATT = 1024
assert TQ_ATT % TK_ATT == 0 or TK_ATT % TQ_ATT == 0
TM_POOL = 256
TT_SSM = 64
TM_MERGE = 256
SUBLANES = 8
TM_ROW = 256
ROWS_TILE = TM_ROW * TOP_K + N_EXPERTS * SUBLANES
GROUPS_TILE = ROWS_TILE // SUBLANES
TAB_W = 2 * LANES
assert TAB_W >= GROUPS_TILE + 2
BM_EXP = 512
VMEM_LIMIT = 52 * 1024 * 1024

_NEG_BIG = -1e30


def _cparams(sem, **kw):
    return pltpu.CompilerParams(dimension_semantics=sem, vmem_limit_bytes=VMEM_LIMIT, **kw)


def _sigmoid(x):
    return 1.0 / (1.0 + jnp.exp(-x))


def _layer_norm(z, g, b):
    mu = jnp.mean(z, axis=-1, keepdims=True)
    zc = z - mu
    var = jnp.mean(zc * zc, axis=-1, keepdims=True)
    return zc * lax.rsqrt(var + LN_EPS) * g + b


def _pack_rows(y):
    u = pltpu.bitcast(y.astype(BF16).astype(F32), U32)
    return u[:, :HALF_D] | (u[:, HALF_D:] >> 16)


def _unpack_rows(p):
    hi = pltpu.bitcast(p & jnp.uint32(0xFFFF0000), F32)
    lo = pltpu.bitcast(p << 16, F32)
    return hi, lo


def _proj_kernel(x_ref, wqkv_ref, ws_ref, q_ref, k_ref, v_ref, up_ref, us_ref, f_ref):
    xb = x_ref[...].astype(BF16)
    for part, ref in enumerate((q_ref, k_ref)):
        h = jnp.dot(xb, wqkv_ref[:, part * ATT_WIDTH:(part + 1) * ATT_WIDTH], preferred_element_type=F32)
        for hh in range(ATT_HEADS):
            ref[0, hh] = h[:, hh * ATT_HEAD_DIM:(hh + 1) * ATT_HEAD_DIM].astype(BF16)
    h = jnp.dot(xb, wqkv_ref[:, 2 * ATT_WIDTH:], preferred_element_type=F32)
    lane = lax.broadcasted_iota(I32, (h.shape[0], V_PAD), 1)
    tail = jnp.where(lane == ATT_HEAD_DIM, 1.0, 0.0)
    for pair in range(ATT_HEADS // 2):
        slab = h[:, pair * V_PAD:(pair + 1) * V_PAD]
        v_ref[0, 2 * pair] = jnp.where(lane < ATT_HEAD_DIM, slab, tail).astype(BF16)
        v_ref[0, 2 * pair + 1] = jnp.where(lane < ATT_HEAD_DIM, pltpu.roll(slab, ATT_HEAD_DIM, 1), tail).astype(BF16)
    hs = jnp.dot(xb, ws_ref[...], preferred_element_type=F32)
    up_ref[...] = hs[:, :POOL_WIDTH]
    us_ref[...] = hs[:, POOL_WIDTH:POOL_WIDTH + SSM_WIDTH]
    f_ref[...] = hs[:, POOL_WIDTH + SSM_WIDTH:]


def _proj(x2, wqkv, wsmall, B, S):
    N = B * S
    nS = S // TM_PROJ
    hm = jax.ShapeDtypeStruct((B, ATT_HEADS, S, ATT_HEAD_DIM), BF16)
    hm_spec = pl.BlockSpec((1, ATT_HEADS, TM_PROJ, ATT_HEAD_DIM), lambda b, s: (b, 0, s, 0))
    hv = jax.ShapeDtypeStruct((B, ATT_HEADS, S, V_PAD), BF16)
    hv_spec = pl.BlockSpec((1, ATT_HEADS, TM_PROJ, V_PAD), lambda b, s: (b, 0, s, 0))
    return pl.pallas_call(
        _proj_kernel,
        grid=(B, nS),
        in_specs=[
            pl.BlockSpec((TM_PROJ, D_MODEL), lambda b, s: (b * nS + s, 0)),
            pl.BlockSpec((D_MODEL, QKV_W), lambda b, s: (0, 0)),
            pl.BlockSpec((D_MODEL, SMALL_W), lambda b, s: (0, 0)),
        ],
        out_specs=[
            hm_spec, hm_spec, hv_spec,
            pl.BlockSpec((TM_PROJ, POOL_WIDTH), lambda b, s: (b * nS + s, 0)),
            pl.BlockSpec((TM_PROJ, SSM_WIDTH), lambda b, s: (s, b)),
            pl.BlockSpec((TM_PROJ, F_PAD), lambda b, s: (b * nS + s, 0)),
        ],
        out_shape=[
            hm, hm, hv,
            jax.ShapeDtypeStruct((N, POOL_WIDTH), F32),
            jax.ShapeDtypeStruct((S, B * SSM_WIDTH), F32),
            jax.ShapeDtypeStruct((N, F_PAD), F32),
        ],
        compiler_params=_cparams(("parallel", "parallel")),
        name="in_proj",
    )(x2, wqkv, wsmall)


def _fcum_kernel(f_ref, b_ref, c_ref):
    rows, S = f_ref.shape
    lane = lax.broadcasted_iota(I32, (rows, LANES), 1)
    carry = jnp.zeros((rows, 1), F32)
    for ch in range(S // LANES):
        z = f_ref[:, ch * LANES:(ch + 1) * LANES] + b_ref[...]
        lf = jnp.minimum(z, 0.0) - jnp.log1p(jnp.exp(-jnp.abs(z)))
        sh = 1
        while sh < LANES:
            lf = lf + jnp.where(lane >= sh, pltpu.roll(lf, sh, 1), 0.0)
            sh *= 2
        lf = lf + carry
        c_ref[:, ch * LANES:(ch + 1) * LANES] = lf
        carry = lf[:, LANES - 1:LANES]


def _fcum(f_rows, b_rows):
    rows, S = f_rows.shape
    return pl.pallas_call(
        _fcum_kernel,
        out_shape=jax.ShapeDtypeStruct((rows, S), F32),
        compiler_params=pltpu.CompilerParams(vmem_limit_bytes=VMEM_LIMIT),
        name="forget_cumsum",
    )(f_rows, b_rows)


def _attn_kernel(q_ref, k_ref, v_ref, c_ref, o_ref):
    TQ, TK = TQ_ATT, TK_ATT
    qi = pl.program_id(2)
    delta = lax.broadcasted_iota(I32, (TQ, TK), 1) - lax.broadcasted_iota(I32, (TQ, TK), 0)
    qs = [q_ref[0, hh] for hh in range(2)]

    def block(j, carry, masked):
        start = pl.multiple_of(j * TK, TK)
        out = []
        for hh in range(2):
            m, acc = carry[2 * hh], carry[2 * hh + 1]
            k = k_ref[0, hh, pl.ds(start, TK), :]
            v = v_ref[0, hh, pl.ds(start, TK), :]
            s = lax.dot_general(qs[hh], k, (((1,), (1,)), ((), ())), preferred_element_type=F32)
            s = s - c_ref[0, hh, :, pl.ds(start, TK)]
            if masked:
                s = jnp.where(delta <= qi * TQ - start, s, -jnp.inf)
            m_new = jnp.maximum(m, jnp.max(s, axis=-1, keepdims=True))
            p = jnp.exp(s - m_new)
            acc = jnp.exp(m - m_new) * acc + jnp.dot(p.astype(BF16), v, preferred_element_type=F32)
            out += [m_new, acc]
        return tuple(out)

    m0 = jnp.full((TQ, 1), -jnp.inf, F32)
    acc0 = jnp.zeros((TQ, V_PAD), F32)
    n_full = (qi * TQ) // TK
    carry = lax.fori_loop(0, n_full, functools.partial(block, masked=False), (m0, acc0, m0, acc0))
    for d in range(max(1, TQ // TK)):
        carry = block(n_full + d, carry, True)
    for hh in range(2):
        acc = carry[2 * hh + 1]
        out = acc[:, :ATT_HEAD_DIM] / acc[:, ATT_HEAD_DIM:ATT_HEAD_DIM + 1]
        o_ref[0, :, hh * ATT_HEAD_DIM:(hh + 1) * ATT_HEAD_DIM] = out.astype(BF16)


def _attention(q, k, v, c, B, S):
    nq = S // TQ_ATT
    qspec = pl.BlockSpec((1, 2, TQ_ATT, ATT_HEAD_DIM), lambda b, hp, i: (b, hp, i, 0))
    kspec = pl.BlockSpec((1, 2, S, ATT_HEAD_DIM), lambda b, hp, i: (b, hp, 0, 0))
    vspec = pl.BlockSpec((1, 2, S, V_PAD), lambda b, hp, i: (b, hp, 0, 0))
    return pl.pallas_call(
        _attn_kernel,
        grid=(B, ATT_HEADS // 2, nq),
        in_specs=[qspec, kspec, vspec,
                  pl.BlockSpec((1, 2, 1, S), lambda b, hp, i: (b, hp, 0, 0))],
        out_specs=pl.BlockSpec((1, TQ_ATT, 2 * ATT_HEAD_DIM), lambda b, hp, i: (b, i, hp)),
        out_shape=jax.ShapeDtypeStruct((B, S, ATT_WIDTH), BF16),
        compiler_params=_cparams(("parallel", "parallel", "arbitrary")),
        name="fox_attention",
    )(q, k, v, c)


def _pool_kernel(u_ref, w_ref, sc_ref, o_ref, pad_ref):
    S = u_ref.shape[1]
    R = TM_POOL
    pad_ref[0:MAX_WINDOW, :] = jnp.zeros((MAX_WINDOW, POOL_WIDTH), F32)
    pad_ref[MAX_WINDOW:, :] = u_ref[0]
    lane = lax.broadcasted_iota(I32, (R, POOL_WIDTH), 1)
    trow = lax.broadcasted_iota(I32, (R, POOL_WIDTH), 0)
    grp = lane // POOL_GROUP_DIM
    win = jnp.where(grp == 0, POOL_WINDOWS[0],
                    jnp.where(grp == 1, POOL_WINDOWS[1], jnp.where(grp == 2, POOL_WINDOWS[2], POOL_WINDOWS[3])))
    for i in range(S // R):
        base = MAX_WINDOW + i * R
        u0 = pad_ref[base:base + R, :]
        acc = u0
        sums = {}
        for kk in range(1, MAX_WINDOW):
            acc = acc + pad_ref[base - kk:base - kk + R, :]
            if kk + 1 in POOL_WINDOWS:
                sums[kk + 1] = acc
        total = jnp.where(grp == 0, sums[POOL_WINDOWS[0]],
                          jnp.where(grp == 1, sums[POOL_WINDOWS[1]],
                                    jnp.where(grp == 2, sums[POOL_WINDOWS[2]], sums[POOL_WINDOWS[3]])))
        cnt = jnp.minimum(trow + (i * R + 1), win).astype(F32)
        mixed = total / cnt - u0
        y = jnp.dot(mixed.astype(BF16), w_ref[...], preferred_element_type=F32) * sc_ref[...]
        o_ref[0, i * R:(i + 1) * R, :] = y.astype(BF16)


def _pool(u3, w_bd, scale):
    B, S, _ = u3.shape
    return pl.pallas_call(
        _pool_kernel,
        grid=(B,),
        in_specs=[pl.BlockSpec((1, S, POOL_WIDTH), lambda b: (b, 0, 0)),
                  pl.BlockSpec((POOL_WIDTH, POOL_WIDTH), lambda b: (0, 0)),
                  pl.BlockSpec((1, POOL_WIDTH), lambda b: (0, 0))],
        out_specs=pl.BlockSpec((1, S, POOL_WIDTH), lambda b: (b, 0, 0)),
        out_shape=jax.ShapeDtypeStruct((B, S, POOL_WIDTH), BF16),
        scratch_shapes=[pltpu.VMEM((S + MAX_WINDOW, POOL_WIDTH), F32)],
        compiler_params=_cparams(("parallel",)),
        name="multiscale_pool",
    )(u3, w_bd, scale)


def _ssm_prep_kernel(lr_ref, li_ref, ldt_ref, brT_ref, biT_ref, ar_ref, ai_ref, bbr_ref, bbi_ref):
    lr = lr_ref[...]
    li = li_ref[...]
    dt = jnp.exp(ldt_ref[...])
    mag = jnp.exp(lr * dt)
    ar = mag * jnp.cos(li * dt)
    ai = mag * jnp.sin(li * dt)
    den = lr * lr + li * li
    nr = ar - 1.0
    zr = (nr * lr + ai * li) / den
    zi = (ai * lr - nr * li) / den
    ar_ref[...] = ar
    ai_ref[...] = ai
    br = brT_ref[...]
    bi = biT_ref[...]
    bbr_ref[...] = zr[:, None, :] * br - zi[:, None, :] * bi
    bbi_ref[...] = zr[:, None, :] * bi + zi[:, None, :] * br


def _ssm_prep(lr, li, log_dt, b_re, b_im):
    G, P, H = b_re.shape
    gp = jax.ShapeDtypeStruct((G, P), F32)
    ghp = jax.ShapeDtypeStruct((G, H, P), F32)
    return pl.pallas_call(
        _ssm_prep_kernel,
        out_shape=[gp, gp, ghp, ghp],
        name="s5_discretise",
    )(lr, li, log_dt.reshape(G, 1), b_re.transpose(0, 2, 1), b_im.transpose(0, 2, 1))


def _gelu_tanh(y):
    return 0.5 * y * (1.0 + jnp.tanh(GELU_C * (y + 0.044715 * (y * y * y))))


def _ssm_kernel(u_ref, bb_ref, ar_ref, ai_ref, cc_ref, d_ref, wg_ref, bg_ref, o_ref, x_scr, hr_scr, hi_scr):
    nb = hr_scr.shape[0]
    tt = u_ref.shape[0] // nb

    @pl.when(pl.program_id(0) == 0)
    def _():
        hr_scr[...] = jnp.zeros_like(hr_scr)
        hi_scr[...] = jnp.zeros_like(hi_scr)

    u = u_ref[...]
    x_scr[...] = jnp.dot(u.astype(BF16), bb_ref[...], preferred_element_type=F32)

    def step(t, carry):
        hr, hi = carry
        r0 = pl.multiple_of(t * nb, nb)
        xr = x_scr[pl.ds(r0, nb), 0:SSM_STATES]
        xi = x_scr[pl.ds(r0, nb), SSM_STATES:2 * SSM_STATES]
        ar = ar_ref[...]
        ai = ai_ref[...]
        nr = ar * hr - ai * hi + xr
        ni = ar * hi + ai * hr + xi
        x_scr[pl.ds(r0, nb), 0:SSM_STATES] = nr
        x_scr[pl.ds(r0, nb), SSM_STATES:2 * SSM_STATES] = ni
        return nr, ni

    hr, hi = lax.fori_loop(0, tt, step, (hr_scr[...], hi_scr[...]))
    hr_scr[...] = hr
    hi_scr[...] = hi
    y = jnp.dot(x_scr[...].astype(BF16), cc_ref[...], preferred_element_type=F32) + d_ref[...] * u
    y = _gelu_tanh(y)
    gl = jnp.dot(y.astype(BF16), wg_ref[...], preferred_element_type=F32) + bg_ref[...]
    o_ref[...] = (y * _sigmoid(gl)).astype(BF16)


def _ssm(u_tm, bblk, ar_b, ai_b, cblk, dvec, wglu, bglu, B, S):
    rows = TT_SSM * B
    const = lambda shape: pl.BlockSpec(shape, lambda i: (0, 0))
    return pl.pallas_call(
        _ssm_kernel,
        grid=(S // TT_SSM,),
        in_specs=[pl.BlockSpec((rows, SSM_WIDTH), lambda i: (i, 0)),
                  const((SSM_WIDTH, 2 * SSM_STATES)),
                  const((B, SSM_STATES)), const((B, SSM_STATES)),
                  const((2 * SSM_STATES, SSM_WIDTH)),
                  const((1, SSM_WIDTH)), const((SSM_WIDTH, SSM_WIDTH)), const((1, SSM_WIDTH))],
        out_specs=pl.BlockSpec((rows, SSM_WIDTH), lambda i: (i, 0)),
        out_shape=jax.ShapeDtypeStruct((S * B, SSM_WIDTH), BF16),
        scratch_shapes=[pltpu.VMEM((rows, 2 * SSM_STATES), F32),
                        pltpu.VMEM((B, SSM_STATES), F32), pltpu.VMEM((B, SSM_STATES), F32)],
        compiler_params=_cparams(("arbitrary",)),
        name="s5_scan",
    )(u_tm, bblk, ar_b, ai_b, cblk, dvec, wglu, bglu)


def _merge_kernel(x_ref, ya_ref, yb_ref, yc_ref, wg_ref, wa_ref, wb_ref, wc_ref, wo_ref, g_ref, b_ref, o_ref):
    x = x_ref[...]
    xb = x.astype(BF16)
    merged = None
    for i, (y_ref, w_ref) in enumerate(((ya_ref, wa_ref), (yb_ref, wb_ref), (yc_ref, wc_ref))):
        gate = _sigmoid(jnp.dot(xb, wg_ref[:, i * D_MODEL:(i + 1) * D_MODEL], preferred_element_type=F32))
        term = gate * jnp.dot(y_ref[...], w_ref[...], preferred_element_type=F32)
        merged = term if merged is None else merged + term
    mix = jnp.dot(merged.astype(BF16), wo_ref[...], preferred_element_type=F32)
    o_ref[...] = _layer_norm(DEEPNORM_ALPHA * x + mix, g_ref[...], b_ref[...])


def _merge(x2, ya, yb, yc_tm, wg, wa, wb, wc, wo, g, b, B, S):
    N = B * S
    nS = S // TM_MERGE
    const = lambda shape: pl.BlockSpec(shape, lambda bb, s: (0, 0))
    row = lambda w: pl.BlockSpec((TM_MERGE, w), lambda bb, s: (bb * nS + s, 0))
    return pl.pallas_call(
        _merge_kernel,
        grid=(B, nS),
        in_specs=[row(D_MODEL), row(ATT_WIDTH), row(POOL_WIDTH),
                  pl.BlockSpec((TM_MERGE, SSM_WIDTH), lambda bb, s: (s, bb)),
                  const((D_MODEL, N_BRANCH * D_MODEL)), const((ATT_WIDTH, D_MODEL)),
                  const((POOL_WIDTH, D_MODEL)), const((SSM_WIDTH, D_MODEL)), const((D_MODEL, D_MODEL)),
                  const((1, D_MODEL)), const((1, D_MODEL))],
        out_specs=row(D_MODEL),
        out_shape=jax.ShapeDtypeStruct((N, D_MODEL), F32),
        compiler_params=_cparams(("parallel", "parallel")),
        name="merge_ln1",
    )(x2, ya, yb, yc_tm, wg, wa, wb, wc, wo, g, b)


def _route_kernel(x_ref, wr_ref, br_ref, gate_ref, slot_ref, slott_ref, cnt_ref):
    tm = x_ref.shape[0]
    x = x_ref[...]
    xh = x.astype(BF16)
    xl = (x - xh.astype(F32)).astype(BF16)
    w = wr_ref[...]
    wh = w.astype(BF16)
    wl = (w - wh.astype(F32)).astype(BF16)
    logits = (jnp.dot(xh, wh, preferred_element_type=F32) + jnp.dot(xl, wh, preferred_element_type=F32)
              + jnp.dot(xh, wl, preferred_element_type=F32)) + br_ref[...]
    lane = lax.broadcasted_iota(I32, (tm, LANES), 1)
    lane_f = lane.astype(F32)
    work = logits
    vals, idxs = [], []
    multihot = jnp.zeros((tm, LANES), F32)
    for _ in range(TOP_K):
        mx = jnp.max(work, axis=-1, keepdims=True)
        ix = jnp.min(jnp.where(work == mx, lane_f, float(LANES)), axis=-1, keepdims=True)
        sel = lane_f == ix
        work = jnp.where(sel, -jnp.inf, work)
        multihot = multihot + sel.astype(F32)
        vals.append(mx)
        idxs.append(ix)
    exps = [jnp.exp(v - vals[0]) for v in vals]
    denom = exps[0] + exps[1] + exps[2] + exps[3]
    r = lax.broadcasted_iota(I32, (tm, tm), 0)
    c = lax.broadcasted_iota(I32, (tm, tm), 1)
    lower = jnp.where(c < r, 1.0, 0.0).astype(BF16)
    before = jnp.dot(lower, multihot.astype(BF16), preferred_element_type=F32)
    n = jnp.sum(multihot, axis=0, keepdims=True)
    n_pad = jnp.floor((n + float(SUBLANES - 1)) * (1.0 / SUBLANES)) * float(SUBLANES)
    lane8 = lax.broadcasted_iota(I32, (SUBLANES, LANES), 1)
    incl = jnp.broadcast_to(n_pad, (SUBLANES, LANES))
    sh = 1
    while sh < LANES:
        incl = incl + jnp.where(lane8 >= sh, pltpu.roll(incl, sh, 1), 0.0)
        sh *= 2
    run_start = incl[0:1, :] - n_pad
    where_in_tile = before + run_start
    gate_out = jnp.zeros((tm, LANES), F32)
    slot_out = jnp.zeros((tm, LANES), F32)
    for kk in range(TOP_K):
        sl = jnp.sum(jnp.where(lane_f == idxs[kk], where_in_tile, 0.0), axis=-1, keepdims=True)
        here = lane == kk
        gate_out = jnp.where(here, exps[kk] / denom, gate_out)
        slot_out = jnp.where(here, sl, slot_out)
    gate_ref[...] = gate_out
    slot_ref[...] = slot_out.astype(I32)
    slott_ref[0] = slot_out.T[0:SUBLANES, :].astype(I32)
    cnt_ref[0] = n_pad.astype(I32)


def _route(x1, wr, br):
    N = x1.shape[0]
    nt = N // TM_ROW
    row = pl.BlockSpec((TM_ROW, LANES), lambda i: (i, 0))
    return pl.pallas_call(
        _route_kernel,
        grid=(nt,),
        in_specs=[pl.BlockSpec((TM_ROW, D_MODEL), lambda i: (i, 0)),
                  pl.BlockSpec((D_MODEL, LANES), lambda i: (0, 0)),
                  pl.BlockSpec((1, LANES), lambda i: (0, 0))],
        out_specs=[row, row,
                   pl.BlockSpec((1, SUBLANES, TM_ROW), lambda i: (i, 0, 0)),
                   pl.BlockSpec((1, 1, LANES), lambda i: (i, 0, 0))],
        out_shape=[jax.ShapeDtypeStruct((N, LANES), F32), jax.ShapeDtypeStruct((N, LANES), I32),
                   jax.ShapeDtypeStruct((nt, SUBLANES, TM_ROW), I32),
                   jax.ShapeDtypeStruct((nt, 1, LANES), I32)],
        compiler_params=_cparams(("parallel",)),
        name="router_top4",
    )(x1, wr, br)


def _issue_groups(tab_ref, make_copy):
    def issue_one(u, _):
        local = pl.multiple_of(u * SUBLANES, SUBLANES)
        remote = pl.multiple_of(tab_ref[0, 0, 1 + u], SUBLANES)
        make_copy(local, remote, SUBLANES).start()
        return 0

    lax.fori_loop(0, tab_ref[0, 0, 0], issue_one, 0)


def _wait_groups(tab_ref, make_copy):
    n_groups = tab_ref[0, 0, 0]

    @pl.when(n_groups > 0)
    def _():
        make_copy(0, 0, n_groups * SUBLANES).wait()


def _dispatch_kernel(tab_ref, tab_prev_ref, x_ref, slott_ref, xs_hbm, srt_scr, sem):
    tm = x_ref.shape[0]
    i = pl.program_id(0)
    last = pl.num_programs(0) - 1
    cur = lax.rem(i, 2)
    r = lax.broadcasted_iota(I32, (ROWS_TILE, tm), 0)
    hit = r == slott_ref[0, 0:1, :]
    for kk in range(1, TOP_K):
        hit = hit | (r == slott_ref[0, kk:kk + 1, :])
    sel = jnp.where(hit, 1.0, 0.0).astype(BF16)
    srt = jnp.dot(sel, x_ref[...].astype(BF16), preferred_element_type=F32)
    srt_scr[cur] = _pack_rows(srt)

    def out_copy(buf):
        return lambda local, remote, rows: pltpu.make_async_copy(
            srt_scr.at[buf, pl.ds(local, rows)], xs_hbm.at[pl.ds(remote, rows)], sem.at[buf])

    _issue_groups(tab_ref, out_copy(cur))

    @pl.when(i > 0)
    def _():
        _wait_groups(tab_prev_ref, out_copy(1 - cur))

    @pl.when(i == last)
    def _():
        _wait_groups(tab_ref, out_copy(cur))
        used = pl.multiple_of(tab_ref[0, 0, TAB_W - 1], SUBLANES)
        free = xs_hbm.shape[0] - used
        n_big = free // BM_EXP
        n_small = (free - n_big * BM_EXP) // SUBLANES
        srt_scr[0, 0:BM_EXP, :] = jnp.zeros((BM_EXP, HALF_D), U32)

        def big(i):
            return pltpu.make_async_copy(
                srt_scr.at[0, pl.ds(0, BM_EXP)],
                xs_hbm.at[pl.ds(pl.multiple_of(used + i * BM_EXP, SUBLANES), BM_EXP)], sem.at[0])

        def small(i):
            return pltpu.make_async_copy(
                srt_scr.at[0, pl.ds(0, SUBLANES)],
                xs_hbm.at[pl.ds(pl.multiple_of(used + n_big * BM_EXP + i * SUBLANES, SUBLANES), SUBLANES)],
                sem.at[0])

        def for_each(make, count, op):
            def body(i, c):
                op(make(i))
                return c
            lax.fori_loop(0, count, body, 0)

        for op in (lambda cp: cp.start(), lambda cp: cp.wait()):
            for_each(big, n_big, op)
            for_each(small, n_small, op)


def _dispatch(tab, x1, slott, m_pad):
    N = x1.shape[0]
    return pl.pallas_call(
        _dispatch_kernel,
        grid=(N // TM_ROW,),
        in_specs=[pl.BlockSpec((1, 1, TAB_W), lambda i: (i, 0, 0), memory_space=pltpu.SMEM),
                  pl.BlockSpec((1, 1, TAB_W), lambda i: (jnp.maximum(i - 1, 0), 0, 0), memory_space=pltpu.SMEM),
                  pl.BlockSpec((TM_ROW, D_MODEL), lambda i: (i, 0)),
                  pl.BlockSpec((1, SUBLANES, TM_ROW), lambda i: (i, 0, 0))],
        out_specs=pl.BlockSpec(memory_space=pl.ANY),
        out_shape=jax.ShapeDtypeStruct((m_pad, HALF_D), U32),
        scratch_shapes=[pltpu.VMEM((2, ROWS_TILE, HALF_D), U32), pltpu.SemaphoreType.DMA((2,))],
        compiler_params=_cparams(("arbitrary",), has_side_effects=True),
        name="moe_dispatch",
    )(tab, tab, x1, slott)


def _expert_kernel(blk_ref, exp_ref, lo_ref, hi_ref, xs_ref, wu_ref, bu_ref, wd_ref, bd_ref, o_ref):
    w = pl.program_id(0)
    lo = lo_ref[w]
    hi = hi_ref[w]

    @pl.when(lo > hi)
    def _():
        o_ref[...] = jnp.zeros_like(o_ref)

    @pl.when(hi > lo)
    def _():
        xh, xl = _unpack_rows(xs_ref[...])
        x = jnp.concatenate([xh.astype(BF16), xl.astype(BF16)], axis=1)
        hu = jnp.dot(x, wu_ref[0], preferred_element_type=F32) + bu_ref[0]
        chunks = []
        for cc in range(D_FF // LANES):
            g = jnp.minimum(hu[:, 2 * cc * LANES:(2 * cc + 1) * LANES], SWIGLU_LIMIT)
            l = jnp.clip(hu[:, (2 * cc + 1) * LANES:(2 * cc + 2) * LANES], -SWIGLU_LIMIT, SWIGLU_LIMIT)
            chunks.append(g * _sigmoid(SWIGLU_ALPHA * g) * (l + 1.0))
        act = jnp.concatenate(chunks, axis=1)
        y = jnp.dot(act.astype(BF16), wd_ref[0], preferred_element_type=F32) + bd_ref[0]
        packed = _pack_rows(y)

        @pl.when(lo == 0)
        def _():
            o_ref[...] = packed

        @pl.when(lo > 0)
        def _():
            row = lax.broadcasted_iota(I32, packed.shape, 0)
            o_ref[...] = jnp.where((row >= lo) & (row < hi), packed, o_ref[...])


def _expert_weight_prep_kernel(wu_ref, wd_ref, perm_ref, ou_ref, od_ref):
    for cc in range(2 * D_FF // (2 * LANES)):
        cols = slice(cc * 2 * LANES, (cc + 1) * 2 * LANES)
        ou_ref[0, :, cols] = jnp.dot(wu_ref[0, 0, :, cols].astype(BF16), perm_ref[...],
                                     preferred_element_type=F32).astype(BF16)
    od_ref[0] = wd_ref[0, 0].astype(BF16)


def _expert_weight_prep(w_up, w_down, layer, perm):
    E = w_up.shape[1]
    return pl.pallas_call(
        _expert_weight_prep_kernel,
        grid=(E,),
        in_specs=[pl.BlockSpec((1, 1, D_MODEL, 2 * D_FF), lambda e: (layer, e, 0, 0)),
                  pl.BlockSpec((1, 1, D_FF, D_MODEL), lambda e: (layer, e, 0, 0)),
                  pl.BlockSpec((2 * LANES, 2 * LANES), lambda e: (0, 0))],
        out_specs=[pl.BlockSpec((1, D_MODEL, 2 * D_FF), lambda e: (e, 0, 0)),
                   pl.BlockSpec((1, D_FF, D_MODEL), lambda e: (e, 0, 0))],
        out_shape=[jax.ShapeDtypeStruct((E, D_MODEL, 2 * D_FF), BF16),
                   jax.ShapeDtypeStruct((E, D_FF, D_MODEL), BF16)],
        compiler_params=_cparams(("parallel",)),
        name="expert_weight_prep",
    )(w_up, w_down, perm)


def _pair_perm():
    p = np.zeros((2 * LANES, 2 * LANES), np.float32)
    j = np.arange(LANES)
    p[2 * j, j] = 1.0
    p[2 * j + 1, LANES + j] = 1.0
    return jnp.asarray(p, BF16)


def _experts(meta, xs, wu, bu, wd, bd):
    M = xs.shape[0]
    W = meta[0].shape[0]
    by_expert = lambda *shape: pl.BlockSpec((1,) + shape, lambda w, blk, ex, lo, hi: (ex[w], 0, 0))
    rows = pl.BlockSpec((BM_EXP, HALF_D), lambda w, blk, ex, lo, hi: (blk[w], 0))
    return pl.pallas_call(
        _expert_kernel,
        grid_spec=pltpu.PrefetchScalarGridSpec(
            num_scalar_prefetch=4,
            grid=(W,),
            in_specs=[rows, by_expert(D_MODEL, 2 * D_FF), by_expert(1, 2 * D_FF),
                      by_expert(D_FF, D_MODEL), by_expert(1, D_MODEL)],
            out_specs=rows,
        ),
        out_shape=jax.ShapeDtypeStruct((M, HALF_D), U32),
        compiler_params=_cparams(("arbitrary",)),
        name="moe_experts",
    )(*meta, xs, wu, bu, wd, bd)


def _combine_kernel(tab_ref, tab_next_ref, x_ref, gate_ref, slot_ref, g_ref, b_ref, yb_hbm, o_ref, buf, sem):
    tm = x_ref.shape[0]
    i = pl.program_id(0)
    cur = lax.rem(i, 2)

    def in_copy(b):
        return lambda local, remote, rows: pltpu.make_async_copy(
            yb_hbm.at[pl.ds(remote, rows)], buf.at[b, pl.ds(local, rows)], sem.at[b])

    @pl.when(i == 0)
    def _():
        buf[...] = jnp.zeros_like(buf)
        _issue_groups(tab_ref, in_copy(cur))

    @pl.when(i + 1 < pl.num_programs(0))
    def _():
        _issue_groups(tab_next_ref, in_copy(1 - cur))

    r = lax.broadcasted_iota(I32, (tm, ROWS_TILE), 1)
    slots = slot_ref[...]
    gates = gate_ref[...]
    pw = jnp.zeros((tm, ROWS_TILE), F32)
    for kk in range(TOP_K):
        pw = jnp.where(r == slots[:, kk:kk + 1], gates[:, kk:kk + 1], pw)
    pw_hi = pw.astype(BF16)
    pw_lo = (pw - pw_hi.astype(F32)).astype(BF16)
    _wait_groups(tab_ref, in_copy(cur))
    yh, yl = _unpack_rows(buf[cur])
    ys = jnp.concatenate([yh.astype(BF16), yl.astype(BF16)], axis=1)
    y = jnp.dot(pw_hi, ys, preferred_element_type=F32) + jnp.dot(pw_lo, ys, preferred_element_type=F32)
    o_ref[...] = _layer_norm(DEEPNORM_ALPHA * x_ref[...] + y, g_ref[...], b_ref[...])


def _combine(tab, x1, gate, slot, g, b, yb):
    N = x1.shape[0]
    nt = N // TM_ROW
    return pl.pallas_call(
        _combine_kernel,
        grid=(N // TM_ROW,),
        in_specs=[pl.BlockSpec((1, 1, TAB_W), lambda i: (i, 0, 0), memory_space=pltpu.SMEM),
                  pl.BlockSpec((1, 1, TAB_W), lambda i: (jnp.minimum(i + 1, nt - 1), 0, 0), memory_space=pltpu.SMEM),
                  pl.BlockSpec((TM_ROW, D_MODEL), lambda i: (i, 0)),
                  pl.BlockSpec((TM_ROW, LANES), lambda i: (i, 0)),
                  pl.BlockSpec((TM_ROW, LANES), lambda i: (i, 0)),
                  pl.BlockSpec((1, D_MODEL), lambda i: (0, 0)),
                  pl.BlockSpec((1, D_MODEL), lambda i: (0, 0)),
                  pl.BlockSpec(memory_space=pl.ANY)],
        out_specs=pl.BlockSpec((TM_ROW, D_MODEL), lambda i: (i, 0)),
        out_shape=jax.ShapeDtypeStruct((N, D_MODEL), F32),
        scratch_shapes=[pltpu.VMEM((2, ROWS_TILE, HALF_D), U32), pltpu.SemaphoreType.DMA((2,))],
        compiler_params=_cparams(("arbitrary",)),
        name="moe_combine_ln2",
    )(tab, tab, x1, gate, slot, g, b, yb)


def _moe_tables(cnt, m_pad):
    n_pad = cnt[:, 0, :N_EXPERTS]
    meta, region = _work_items(n_pad.sum(axis=0), m_pad)
    run_end = jnp.cumsum(n_pad, axis=1)
    run_start = run_end - n_pad
    hbm_start = region[None, :] + jnp.cumsum(n_pad, axis=0) - n_pad
    first_row = SUBLANES * jnp.arange(GROUPS_TILE, dtype=I32)
    owner = jnp.minimum((run_end[:, None, :] <= first_row[None, :, None]).sum(axis=-1), N_EXPERTS - 1)
    sel = owner[:, :, None] == jnp.arange(N_EXPERTS, dtype=I32)[None, None, :]
    shift = jnp.sum(jnp.where(sel, (hbm_start - run_start)[:, None, :], 0), axis=-1)
    n_groups = run_end[:, -1:] // SUBLANES
    pad = jnp.zeros((n_pad.shape[0], TAB_W - 2 - GROUPS_TILE), I32)
    used = jnp.broadcast_to(n_pad.sum(), n_groups.shape)
    tab = jnp.concatenate([n_groups, shift + first_row[None, :], pad, used], axis=1).astype(I32)
    return tab[:, None, :], meta


def _work_items(counts, M):
    nblk = M // BM_EXP
    W = nblk + N_EXPERTS
    ends = jnp.cumsum(counts)
    starts = ends - counts
    first = starts // BM_EXP
    last = jnp.maximum(ends - 1, 0) // BM_EXP
    n_items = jnp.where(counts > 0, last - first + 1, 0)
    item_end = jnp.cumsum(n_items)
    item_start = item_end - n_items
    total = item_end[-1]
    w = jnp.arange(W, dtype=I32)
    w_eff = jnp.minimum(w, total - 1)
    owner = (item_end[None, :] <= w_eff[:, None]).sum(axis=1).astype(I32)
    sel = owner[:, None] == jnp.arange(N_EXPERTS, dtype=I32)[None, :]
    pick = lambda table: jnp.sum(jnp.where(sel, table[None, :], 0), axis=1)
    blk = pick(first) + (w_eff - pick(item_start))
    lo = jnp.clip(pick(starts) - blk * BM_EXP, 0, BM_EXP)
    hi = jnp.clip(pick(ends) - blk * BM_EXP, 0, BM_EXP)
    valid = w < total
    used_blocks = (ends[-1] + BM_EXP - 1) // BM_EXP
    spare_blk = used_blocks + (w - total)
    fill = (~valid) & (spare_blk < nblk)
    blk = jnp.where(valid, blk, jnp.minimum(spare_blk, nblk - 1))
    lo = jnp.where(valid, lo, jnp.where(fill, 1, 0))
    hi = jnp.where(valid, hi, 0)
    return (blk.astype(I32), owner, lo.astype(I32), hi.astype(I32)), starts


def _block_diag(blocks):
    G, a, b = blocks.shape
    eye = jnp.eye(G, dtype=blocks.dtype)
    return jnp.einsum("gab,gh->gahb", blocks, eye).reshape(G * a, G * b)


def kernel(x, w_in, b_forget, w_pool, pool_scale, ssm_lambda_re, ssm_lambda_im, ssm_log_dt, ssm_b_re, ssm_b_im, ssm_c_re, ssm_c_im, ssm_d, w_glu, b_glu, w_branch_a, w_branch_b, w_branch_c, w_out, ln1_g, ln1_b, w_router, b_router, w_up, b_up, w_down, b_down, ln2_g, ln2_b):
    B, S, D = x.shape
    assert D == D_MODEL and S % TQ_ATT == 0 and S % TK_ATT == 0 and S % TM_PROJ == 0
    N = B * S
    m_pad = (N // TM_ROW) * ROWS_TILE
    assert m_pad % BM_EXP == 0
    perm = _pair_perm()
    x2 = x.reshape(N, D)
    scale = ATT_HEAD_DIM ** -0.5
    for l in range(DEPTH):
        wl = w_in[l]
        wq = wl[:, :ATT_WIDTH] * scale
        wqkv = jnp.concatenate([wq, wl[:, ATT_WIDTH:QKV_W]], axis=1).astype(BF16)
        c0 = QKV_W
        wf = wl[:, c0:c0 + ATT_HEADS]
        wsmall = jnp.concatenate(
            [wl[:, c0 + ATT_HEADS:c0 + ATT_HEADS + POOL_WIDTH + SSM_WIDTH], wf,
             jnp.zeros((D, F_PAD - ATT_HEADS), F32)], axis=1).astype(BF16)
        wgates = wl[:, c0 + ATT_HEADS + POOL_WIDTH + SSM_WIDTH:].astype(BF16)

        q, k, v, u_pool, u_ssm_tm, f_pad = _proj(x2, wqkv, wsmall, B, S)

        f_rows = f_pad[:, :ATT_HEADS].reshape(B, S, ATT_HEADS).transpose(0, 2, 1).reshape(B * ATT_HEADS, S)
        b_rows = jnp.tile(b_forget[l], B).reshape(B * ATT_HEADS, 1)
        c = _fcum(f_rows, b_rows).reshape(B, ATT_HEADS, 1, S)
        y_a = _attention(q, k, v, c, B, S).reshape(N, ATT_WIDTH)

        w_pool_bd = _block_diag(w_pool[l]).astype(BF16)
        y_b = _pool(u_pool.reshape(B, S, POOL_WIDTH), w_pool_bd, pool_scale[l].reshape(1, POOL_WIDTH)).reshape(N, POOL_WIDTH)

        ar, ai, bbrT, bbiT = _ssm_prep(ssm_lambda_re[l], ssm_lambda_im[l], ssm_log_dt[l], ssm_b_re[l], ssm_b_im[l])
        bblk = jnp.concatenate([_block_diag(bbrT), _block_diag(bbiT)], axis=1).astype(BF16)
        cblk = jnp.concatenate([_block_diag(ssm_c_re[l].transpose(0, 2, 1)),
                                -_block_diag(ssm_c_im[l].transpose(0, 2, 1))], axis=0).astype(BF16)
        ar_b = jnp.broadcast_to(ar.reshape(1, SSM_STATES), (B, SSM_STATES))
        ai_b = jnp.broadcast_to(ai.reshape(1, SSM_STATES), (B, SSM_STATES))
        y_c_tm = _ssm(u_ssm_tm.reshape(S * B, SSM_WIDTH), bblk, ar_b, ai_b, cblk,
                      ssm_d[l].reshape(1, SSM_WIDTH), w_glu[l].astype(BF16), b_glu[l].reshape(1, SSM_WIDTH), B, S)

        x1 = _merge(x2, y_a, y_b, y_c_tm.reshape(S, B * SSM_WIDTH), wgates,
                    w_branch_a[l].astype(BF16), w_branch_b[l].astype(BF16), w_branch_c[l].astype(BF16),
                    w_out[l].astype(BF16), ln1_g[l].reshape(1, D), ln1_b[l].reshape(1, D), B, S)

        wr = jnp.concatenate([w_router[l], jnp.zeros((D, LANES - N_EXPERTS), F32)], axis=1)
        br = jnp.concatenate([b_router[l], jnp.full((LANES - N_EXPERTS,), _NEG_BIG, F32)]).reshape(1, LANES)
        gate, slot, slott, cnt = _route(x1, wr, br)
        tab, meta = _moe_tables(cnt, m_pad)
        xs = _dispatch(tab, x1, slott, m_pad)
        bu = b_up[l].reshape(N_EXPERTS, D_FF // LANES, LANES, 2).transpose(0, 1, 3, 2).reshape(N_EXPERTS, 1, 2 * D_FF)
        wu, wd = _expert_weight_prep(w_up, w_down, l, perm)
        yb = _experts(meta, xs, wu, bu, wd, b_down[l][:, None, :])
        x2 = _combine(tab, x1, gate, slot, ln2_g[l].reshape(1, D), ln2_b[l].reshape(1, D), yb)
    return x2.reshape(B, S, D)
```

```python
import math

import jax
import jax.numpy as jnp
import numpy as np
from jax import lax
from jax.experimental import pallas as pl
from jax.experimental.pallas import tpu as pltpu

F32 = jnp.float32
BF16 = jnp.bfloat16
I32 = jnp.int32
U32 = jnp.uint32

D_MODEL = 1024
DEPTH = 4
ATT_HEADS = 8
ATT_HEAD_DIM = 64
ATT_WIDTH = ATT_HEADS * ATT_HEAD_DIM
POOL_WINDOWS = (2, 4, 8, 16)
POOL_GROUPS = 4
POOL_WIDTH = 256
POOL_GROUP_DIM = 64
MAX_WINDOW = max(POOL_WINDOWS)
SSM_WIDTH = 256
SSM_GROUP_DIM = 16
SSM_GROUPS = 16
SSM_STATE = 64
SSM_STATES = SSM_GROUPS * SSM_STATE
N_BRANCH = 3
N_EXPERTS = 32
TOP_K = 4
D_FF = D_MODEL
SWIGLU_LIMIT = 7.0
SWIGLU_ALPHA = 1.702
LN_EPS = 1e-5
DEEPNORM_ALPHA = (2.0 * DEPTH) ** 0.25
GELU_C = math.sqrt(2.0 / math.pi)

LANES = 128
HALF_D = D_MODEL // 2
F_PAD = LANES
V_PAD = LANES
SMALL_W = POOL_WIDTH + SSM_WIDTH + F_PAD
QKV_W = 3 * ATT_WIDTH
TM_PROJ = 512
T_ATT = 1024
TM_POOL = 256
TT_SSM = 64
TM_MERGE = 256
SUBLANES = 8
TM_ROW = 256
ROWS_TILE = TM_ROW * TOP_K + N_EXPERTS * SUBLANES
GROUPS_TILE = ROWS_TILE // SUBLANES
TAB_W = 2 * LANES
assert TAB_W >= GROUPS_TILE + 2
BM_EXP = 512
VMEM_LIMIT = 52 * 1024 * 1024

_NEG_BIG = -1e30


def _cparams(sem, **kw):
    return pltpu.CompilerParams(dimension_semantics=sem, vmem_limit_bytes=VMEM_LIMIT, **kw)


def _sigmoid(x):
    return 1.0 / (1.0 + jnp.exp(-x))


def _layer_norm(z, g, b):
    mu = jnp.mean(z, axis=-1, keepdims=True)
    zc = z - mu
    var = jnp.mean(zc * zc, axis=-1, keepdims=True)
    return zc * lax.rsqrt(var + LN_EPS) * g + b


def _pack_rows(y):
    u = pltpu.bitcast(y.astype(BF16).astype(F32), U32)
    return u[:, :HALF_D] | (u[:, HALF_D:] >> 16)


def _unpack_rows(p):
    hi = pltpu.bitcast(p & jnp.uint32(0xFFFF0000), F32)
    lo = pltpu.bitcast(p << 16, F32)
    return hi, lo


def _proj_kernel(x_ref, wqkv_ref, ws_ref, q_ref, k_ref, v_ref, up_ref, us_ref, f_ref):
    xb = x_ref[...].astype(BF16)
    for part, ref in enumerate((q_ref, k_ref)):
        h = jnp.dot(xb, wqkv_ref[:, part * ATT_WIDTH:(part + 1) * ATT_WIDTH], preferred_element_type=F32)
        for hh in range(ATT_HEADS):
            ref[0, hh] = h[:, hh * ATT_HEAD_DIM:(hh + 1) * ATT_HEAD_DIM].astype(BF16)
    h = jnp.dot(xb, wqkv_ref[:, 2 * ATT_WIDTH:], preferred_element_type=F32)
    lane = lax.broadcasted_iota(I32, (h.shape[0], V_PAD), 1)
    tail = jnp.where(lane == ATT_HEAD_DIM, 1.0, 0.0)
    for pair in range(ATT_HEADS // 2):
        slab = h[:, pair * V_PAD:(pair + 1) * V_PAD]
        v_ref[0, 2 * pair] = jnp.where(lane < ATT_HEAD_DIM, slab, tail).astype(BF16)
        v_ref[0, 2 * pair + 1] = jnp.where(lane < ATT_HEAD_DIM, pltpu.roll(slab, ATT_HEAD_DIM, 1), tail).astype(BF16)
    hs = jnp.dot(xb, ws_ref[...], preferred_element_type=F32)
    up_ref[...] = hs[:, :POOL_WIDTH]
    us_ref[...] = hs[:, POOL_WIDTH:POOL_WIDTH + SSM_WIDTH]
    f_ref[...] = hs[:, POOL_WIDTH + SSM_WIDTH:]


def _proj(x2, wqkv, wsmall, B, S):
    N = B * S
    nS = S // TM_PROJ
    hm = jax.ShapeDtypeStruct((B, ATT_HEADS, S, ATT_HEAD_DIM), BF16)
    hm_spec = pl.BlockSpec((1, ATT_HEADS, TM_PROJ, ATT_HEAD_DIM), lambda b, s: (b, 0, s, 0))
    hv = jax.ShapeDtypeStruct((B, ATT_HEADS, S, V_PAD), BF16)
    hv_spec = pl.BlockSpec((1, ATT_HEADS, TM_PROJ, V_PAD), lambda b, s: (b, 0, s, 0))
    return pl.pallas_call(
        _proj_kernel,
        grid=(B, nS),
        in_specs=[
            pl.BlockSpec((TM_PROJ, D_MODEL), lambda b, s: (b * nS + s, 0)),
            pl.BlockSpec((D_MODEL, QKV_W), lambda b, s: (0, 0)),
            pl.BlockSpec((D_MODEL, SMALL_W), lambda b, s: (0, 0)),
        ],
        out_specs=[
            hm_spec, hm_spec, hv_spec,
            pl.BlockSpec((TM_PROJ, POOL_WIDTH), lambda b, s: (b * nS + s, 0)),
            pl.BlockSpec((TM_PROJ, SSM_WIDTH), lambda b, s: (s, b)),
            pl.BlockSpec((TM_PROJ, F_PAD), lambda b, s: (b * nS + s, 0)),
        ],
        out_shape=[
            hm, hm, hv,
            jax.ShapeDtypeStruct((N, POOL_WIDTH), F32),
            jax.ShapeDtypeStruct((S, B * SSM_WIDTH), F32),
            jax.ShapeDtypeStruct((N, F_PAD), F32),
        ],
        compiler_params=_cparams(("parallel", "parallel")),
        name="in_proj",
    )(x2, wqkv, wsmall)


def _fcum_kernel(f_ref, b_ref, c_ref):
    rows, S = f_ref.shape
    lane = lax.broadcasted_iota(I32, (rows, LANES), 1)
    carry = jnp.zeros((rows, 1), F32)
    for ch in range(S // LANES):
        z = f_ref[:, ch * LANES:(ch + 1) * LANES] + b_ref[...]
        lf = jnp.minimum(z, 0.0) - jnp.log1p(jnp.exp(-jnp.abs(z)))
        sh = 1
        while sh < LANES:
            lf = lf + jnp.where(lane >= sh, pltpu.roll(lf, sh, 1), 0.0)
            sh *= 2
        lf = lf + carry
        c_ref[:, ch * LANES:(ch + 1) * LANES] = lf
        carry = lf[:, LANES - 1:LANES]


def _fcum(f_rows, b_rows):
    rows, S = f_rows.shape
    return pl.pallas_call(
        _fcum_kernel,
        out_shape=jax.ShapeDtypeStruct((rows, S), F32),
        compiler_params=pltpu.CompilerParams(vmem_limit_bytes=VMEM_LIMIT),
        name="forget_cumsum",
    )(f_rows, b_rows)


def _attn_kernel(q_ref, k_ref, v_ref, c_ref, o_ref):
    T = T_ATT
    H = T // 2
    qi = pl.program_id(2)
    qs = [q_ref[0, hh] for hh in range(2)]

    def update(carry, q_pair, key_start, n_keys, diag):
        out = []
        for hh in range(2):
            m, acc = carry[2 * hh], carry[2 * hh + 1]
            k = k_ref[0, hh, pl.ds(key_start, n_keys), :]
            v = v_ref[0, hh, pl.ds(key_start, n_keys), :]
            s = lax.dot_general(q_pair[hh], k, (((1,), (1,)), ((), ())), preferred_element_type=F32)
            s = s - c_ref[0, hh, :, pl.ds(key_start, n_keys)]
            if diag is not None:
                delta = lax.broadcasted_iota(I32, s.shape, 1) - lax.broadcasted_iota(I32, s.shape, 0)
                s = jnp.where(delta <= diag, s, -jnp.inf)
            m_new = jnp.maximum(m, jnp.max(s, axis=-1, keepdims=True))
            p = jnp.exp(s - m_new)
            acc = jnp.exp(m - m_new) * acc + jnp.dot(p.astype(BF16), v, preferred_element_type=F32)
            out += [m_new, acc]
        return tuple(out)

    m0 = jnp.full((T, 1), -jnp.inf, F32)
    acc0 = jnp.zeros((T, V_PAD), F32)
    carry = lax.fori_loop(
        0, qi, lambda j, c: update(c, qs, pl.multiple_of(j * T, T), T, None), (m0, acc0, m0, acc0))
    diag_start = pl.multiple_of(qi * T, T)
    halves = []
    for half in range(2):
        rows = slice(half * H, (half + 1) * H)
        halves.append(update(tuple(x[rows] for x in carry), [q[rows] for q in qs],
                             diag_start, (half + 1) * H, half * H))
    for hh in range(2):
        acc = jnp.concatenate([halves[0][2 * hh + 1], halves[1][2 * hh + 1]], axis=0)
        out = acc[:, :ATT_HEAD_DIM] / acc[:, ATT_HEAD_DIM:ATT_HEAD_DIM + 1]
        o_ref[0, :, hh * ATT_HEAD_DIM:(hh + 1) * ATT_HEAD_DIM] = out.astype(BF16)


def _attention(q, k, v, c, B, S):
    nq = S // T_ATT
    qspec = pl.BlockSpec((1, 2, T_ATT, ATT_HEAD_DIM), lambda b, hp, i: (b, hp, i, 0))
    kspec = pl.BlockSpec((1, 2, S, ATT_HEAD_DIM), lambda b, hp, i: (b, hp, 0, 0))
    vspec = pl.BlockSpec((1, 2, S, V_PAD), lambda b, hp, i: (b, hp, 0, 0))
    return pl.pallas_call(
        _attn_kernel,
        grid=(B, ATT_HEADS // 2, nq),
        in_specs=[qspec, kspec, vspec,
                  pl.BlockSpec((1, 2, 1, S), lambda b, hp, i: (b, hp, 0, 0))],
        out_specs=pl.BlockSpec((1, T_ATT, 2 * ATT_HEAD_DIM), lambda b, hp, i: (b, i, hp)),
        out_shape=jax.ShapeDtypeStruct((B, S, ATT_WIDTH), BF16),
        compiler_params=_cparams(("parallel", "parallel", "arbitrary")),
        name="fox_attention",
    )(q, k, v, c)


def _pool_kernel(u_ref, w_ref, sc_ref, o_ref, pad_ref):
    S = u_ref.shape[1]
    R = TM_POOL
    pad_ref[0:MAX_WINDOW, :] = jnp.zeros((MAX_WINDOW, POOL_WIDTH), F32)
    pad_ref[MAX_WINDOW:, :] = u_ref[0]
    lane = lax.broadcasted_iota(I32, (R, POOL_WIDTH), 1)
    trow = lax.broadcasted_iota(I32, (R, POOL_WIDTH), 0)
    grp = lane // POOL_GROUP_DIM
    win = jnp.where(grp == 0, POOL_WINDOWS[0],
                    jnp.where(grp == 1, POOL_WINDOWS[1], jnp.where(grp == 2, POOL_WINDOWS[2], POOL_WINDOWS[3])))
    for i in range(S // R):
        base = MAX_WINDOW + i * R
        u0 = pad_ref[base:base + R, :]
        acc = u0
        sums = {}
        for kk in range(1, MAX_WINDOW):
            acc = acc + pad_ref[base - kk:base - kk + R, :]
            if kk + 1 in POOL_WINDOWS:
                sums[kk + 1] = acc
        total = jnp.where(grp == 0, sums[POOL_WINDOWS[0]],
                          jnp.where(grp == 1, sums[POOL_WINDOWS[1]],
                                    jnp.where(grp == 2, sums[POOL_WINDOWS[2]], sums[POOL_WINDOWS[3]])))
        cnt = jnp.minimum(trow + (i * R + 1), win).astype(F32)
        mixed = total / cnt - u0
        y = jnp.dot(mixed.astype(BF16), w_ref[...], preferred_element_type=F32) * sc_ref[...]
        o_ref[0, i * R:(i + 1) * R, :] = y.astype(BF16)


def _pool(u3, w_bd, scale):
    B, S, _ = u3.shape
    return pl.pallas_call(
        _pool_kernel,
        grid=(B,),
        in_specs=[pl.BlockSpec((1, S, POOL_WIDTH), lambda b: (b, 0, 0)),
                  pl.BlockSpec((POOL_WIDTH, POOL_WIDTH), lambda b: (0, 0)),
                  pl.BlockSpec((1, POOL_WIDTH), lambda b: (0, 0))],
        out_specs=pl.BlockSpec((1, S, POOL_WIDTH), lambda b: (b, 0, 0)),
        out_shape=jax.ShapeDtypeStruct((B, S, POOL_WIDTH), BF16),
        scratch_shapes=[pltpu.VMEM((S + MAX_WINDOW, POOL_WIDTH), F32)],
        compiler_params=_cparams(("parallel",)),
        name="multiscale_pool",
    )(u3, w_bd, scale)


def _ssm_prep_kernel(lr_ref, li_ref, ldt_ref, brT_ref, biT_ref, ar_ref, ai_ref, bbr_ref, bbi_ref):
    lr = lr_ref[...]
    li = li_ref[...]
    dt = jnp.exp(ldt_ref[...])
    mag = jnp.exp(lr * dt)
    ar = mag * jnp.cos(li * dt)
    ai = mag * jnp.sin(li * dt)
    den = lr * lr + li * li
    nr = ar - 1.0
    zr = (nr * lr + ai * li) / den
    zi = (ai * lr - nr * li) / den
    ar_ref[...] = ar
    ai_ref[...] = ai
    br = brT_ref[...]
    bi = biT_ref[...]
    bbr_ref[...] = zr[:, None, :] * br - zi[:, None, :] * bi
    bbi_ref[...] = zr[:, None, :] * bi + zi[:, None, :] * br


def _ssm_prep(lr, li, log_dt, b_re, b_im):
    G, P, H = b_re.shape
    gp = jax.ShapeDtypeStruct((G, P), F32)
    ghp = jax.ShapeDtypeStruct((G, H, P), F32)
    return pl.pallas_call(
        _ssm_prep_kernel,
        out_shape=[gp, gp, ghp, ghp],
        name="s5_discretise",
    )(lr, li, log_dt.reshape(G, 1), b_re.transpose(0, 2, 1), b_im.transpose(0, 2, 1))


def _gelu_tanh(y):
    return 0.5 * y * (1.0 + jnp.tanh(GELU_C * (y + 0.044715 * (y * y * y))))


def _ssm_kernel(u_ref, bb_ref, ar_ref, ai_ref, cc_ref, d_ref, wg_ref, bg_ref, o_ref, x_scr, hr_scr, hi_scr):
    nb = hr_scr.shape[0]
    tt = u_ref.shape[0] // nb

    @pl.when(pl.program_id(0) == 0)
    def _():
        hr_scr[...] = jnp.zeros_like(hr_scr)
        hi_scr[...] = jnp.zeros_like(hi_scr)

    u = u_ref[...]
    x_scr[...] = jnp.dot(u.astype(BF16), bb_ref[...], preferred_element_type=F32)

    def step(t, carry):
        hr, hi = carry
        r0 = pl.multiple_of(t * nb, nb)
        xr = x_scr[pl.ds(r0, nb), 0:SSM_STATES]
        xi = x_scr[pl.ds(r0, nb), SSM_STATES:2 * SSM_STATES]
        ar = ar_ref[...]
        ai = ai_ref[...]
        nr = ar * hr - ai * hi + xr
        ni = ar * hi + ai * hr + xi
        x_scr[pl.ds(r0, nb), 0:SSM_STATES] = nr
        x_scr[pl.ds(r0, nb), SSM_STATES:2 * SSM_STATES] = ni
        return nr, ni

    hr, hi = lax.fori_loop(0, tt, step, (hr_scr[...], hi_scr[...]))
    hr_scr[...] = hr
    hi_scr[...] = hi
    y = jnp.dot(x_scr[...].astype(BF16), cc_ref[...], preferred_element_type=F32) + d_ref[...] * u
    y = _gelu_tanh(y)
    gl = jnp.dot(y.astype(BF16), wg_ref[...], preferred_element_type=F32) + bg_ref[...]
    o_ref[...] = (y * _sigmoid(gl)).astype(BF16)


def _ssm(u_tm, bblk, ar_b, ai_b, cblk, dvec, wglu, bglu, B, S):
    rows = TT_SSM * B
    const = lambda shape: pl.BlockSpec(shape, lambda i: (0, 0))
    return pl.pallas_call(
        _ssm_kernel,
        grid=(S // TT_SSM,),
        in_specs=[pl.BlockSpec((rows, SSM_WIDTH), lambda i: (i, 0)),
                  const((SSM_WIDTH, 2 * SSM_STATES)),
                  const((B, SSM_STATES)), const((B, SSM_STATES)),
                  const((2 * SSM_STATES, SSM_WIDTH)),
                  const((1, SSM_WIDTH)), const((SSM_WIDTH, SSM_WIDTH)), const((1, SSM_WIDTH))],
        out_specs=pl.BlockSpec((rows, SSM_WIDTH), lambda i: (i, 0)),
        out_shape=jax.ShapeDtypeStruct((S * B, SSM_WIDTH), BF16),
        scratch_shapes=[pltpu.VMEM((rows, 2 * SSM_STATES), F32),
                        pltpu.VMEM((B, SSM_STATES), F32), pltpu.VMEM((B, SSM_STATES), F32)],
        compiler_params=_cparams(("arbitrary",)),
        name="s5_scan",
    )(u_tm, bblk, ar_b, ai_b, cblk, dvec, wglu, bglu)


def _merge_kernel(x_ref, ya_ref, yb_ref, yc_ref, wg_ref, wa_ref, wb_ref, wc_ref, wo_ref, g_ref, b_ref, o_ref):
    x = x_ref[...]
    xb = x.astype(BF16)
    merged = None
    for i, (y_ref, w_ref) in enumerate(((ya_ref, wa_ref), (yb_ref, wb_ref), (yc_ref, wc_ref))):
        gate = _sigmoid(jnp.dot(xb, wg_ref[:, i * D_MODEL:(i + 1) * D_MODEL], preferred_element_type=F32))
        term = gate * jnp.dot(y_ref[...], w_ref[...], preferred_element_type=F32)
        merged = term if merged is None else merged + term
    mix = jnp.dot(merged.astype(BF16), wo_ref[...], preferred_element_type=F32)
    o_ref[...] = _layer_norm(DEEPNORM_ALPHA * x + mix, g_ref[...], b_ref[...])


def _merge(x2, ya, yb, yc_tm, wg, wa, wb, wc, wo, g, b, B, S):
    N = B * S
    nS = S // TM_MERGE
    const = lambda shape: pl.BlockSpec(shape, lambda bb, s: (0, 0))
    row = lambda w: pl.BlockSpec((TM_MERGE, w), lambda bb, s: (bb * nS + s, 0))
    return pl.pallas_call(
        _merge_kernel,
        grid=(B, nS),
        in_specs=[row(D_MODEL), row(ATT_WIDTH), row(POOL_WIDTH),
                  pl.BlockSpec((TM_MERGE, SSM_WIDTH), lambda bb, s: (s, bb)),
                  const((D_MODEL, N_BRANCH * D_MODEL)), const((ATT_WIDTH, D_MODEL)),
                  const((POOL_WIDTH, D_MODEL)), const((SSM_WIDTH, D_MODEL)), const((D_MODEL, D_MODEL)),
                  const((1, D_MODEL)), const((1, D_MODEL))],
        out_specs=row(D_MODEL),
        out_shape=jax.ShapeDtypeStruct((N, D_MODEL), F32),
        compiler_params=_cparams(("parallel", "parallel")),
        name="merge_ln1",
    )(x2, ya, yb, yc_tm, wg, wa, wb, wc, wo, g, b)


def _route_kernel(x_ref, wr_ref, br_ref, gate_ref, slot_ref, slott_ref, cnt_ref):
    tm = x_ref.shape[0]
    x = x_ref[...]
    xh = x.astype(BF16)
    xl = (x - xh.astype(F32)).astype(BF16)
    w = wr_ref[...]
    wh = w.astype(BF16)
    wl = (w - wh.astype(F32)).astype(BF16)
    logits = (jnp.dot(xh, wh, preferred_element_type=F32) + jnp.dot(xl, wh, preferred_element_type=F32)
              + jnp.dot(xh, wl, preferred_element_type=F32)) + br_ref[...]
    work = logits.T[0:N_EXPERTS, :]
    sub = lax.broadcasted_iota(I32, (N_EXPERTS, tm), 0).astype(F32)
    vals, idxs = [], []
    multihot = jnp.zeros((N_EXPERTS, tm), F32)
    for _ in range(TOP_K):
        mx = jnp.max(work, axis=0, keepdims=True)
        ix = jnp.min(jnp.where(work == mx, sub, float(N_EXPERTS)), axis=0, keepdims=True)
        sel = sub == ix
        work = jnp.where(sel, -jnp.inf, work)
        multihot = multihot + jnp.where(sel, 1.0, 0.0)
        vals.append(mx)
        idxs.append(ix)
    exps = [jnp.exp(v - vals[0]) for v in vals]
    denom = exps[0] + exps[1] + exps[2] + exps[3]
    r = lax.broadcasted_iota(I32, (tm, tm), 0)
    c = lax.broadcasted_iota(I32, (tm, tm), 1)
    earlier = jnp.where(r < c, 1.0, 0.0).astype(BF16)
    before = jnp.dot(multihot.astype(BF16), earlier, preferred_element_type=F32)
    n = jnp.sum(multihot, axis=1, keepdims=True)
    n_pad = jnp.floor((n + float(SUBLANES - 1)) * (1.0 / SUBLANES)) * float(SUBLANES)
    n_pad_b = jnp.broadcast_to(n_pad, (N_EXPERTS, LANES))
    er = lax.broadcasted_iota(I32, (N_EXPERTS, N_EXPERTS), 0)
    ec = lax.broadcasted_iota(I32, (N_EXPERTS, N_EXPERTS), 1)
    below = jnp.where(ec < er, 1.0, 0.0).astype(BF16)
    run_start = jnp.dot(below, n_pad_b.astype(BF16), preferred_element_type=F32)[:, 0:1]
    where_in_tile = before + run_start
    slots = [jnp.sum(jnp.where(sub == idxs[kk], where_in_tile, 0.0), axis=0, keepdims=True) for kk in range(TOP_K)]
    gates = [exps[kk] / denom for kk in range(TOP_K)]
    sub8 = lax.broadcasted_iota(I32, (SUBLANES, tm), 0)
    slot8 = jnp.zeros((SUBLANES, tm), F32)
    gate8 = jnp.zeros((SUBLANES, tm), F32)
    for kk in range(TOP_K):
        slot8 = jnp.where(sub8 == kk, slots[kk], slot8)
        gate8 = jnp.where(sub8 == kk, gates[kk], gate8)
    slott_ref[0] = slot8.astype(I32)
    pad_rows = jnp.zeros((LANES - SUBLANES, tm), F32)
    slot_ref[...] = jnp.concatenate([slot8, pad_rows], axis=0).T.astype(I32)
    gate_ref[...] = jnp.concatenate([gate8, pad_rows], axis=0).T
    cnt_ref[0] = n_pad_b.astype(I32)


def _route(x1, wr, br):
    N = x1.shape[0]
    nt = N // TM_ROW
    row = pl.BlockSpec((TM_ROW, LANES), lambda i: (i, 0))
    return pl.pallas_call(
        _route_kernel,
        grid=(nt,),
        in_specs=[pl.BlockSpec((TM_ROW, D_MODEL), lambda i: (i, 0)),
                  pl.BlockSpec((D_MODEL, LANES), lambda i: (0, 0)),
                  pl.BlockSpec((1, LANES), lambda i: (0, 0))],
        out_specs=[row, row,
                   pl.BlockSpec((1, SUBLANES, TM_ROW), lambda i: (i, 0, 0)),
                   pl.BlockSpec((1, N_EXPERTS, LANES), lambda i: (i, 0, 0))],
        out_shape=[jax.ShapeDtypeStruct((N, LANES), F32), jax.ShapeDtypeStruct((N, LANES), I32),
                   jax.ShapeDtypeStruct((nt, SUBLANES, TM_ROW), I32),
                   jax.ShapeDtypeStruct((nt, N_EXPERTS, LANES), I32)],
        compiler_params=_cparams(("parallel",)),
        name="router_top4",
    )(x1, wr, br)


def _issue_groups(tab_ref, make_copy):
    def issue_one(u, _):
        local = pl.multiple_of(u * SUBLANES, SUBLANES)
        remote = pl.multiple_of(tab_ref[0, 0, 1 + u], SUBLANES)
        make_copy(local, remote, SUBLANES).start()
        return 0

    lax.fori_loop(0, tab_ref[0, 0, 0], issue_one, 0)


def _wait_groups(tab_ref, make_copy):
    n_groups = tab_ref[0, 0, 0]

    @pl.when(n_groups > 0)
    def _():
        make_copy(0, 0, n_groups * SUBLANES).wait()


def _dispatch_kernel(tab_ref, tab_prev_ref, x_ref, slott_ref, xs_hbm, srt_scr, sem):
    tm = x_ref.shape[0]
    i = pl.program_id(0)
    last = pl.num_programs(0) - 1
    cur = lax.rem(i, 2)
    r = lax.broadcasted_iota(I32, (ROWS_TILE, tm), 0)
    hit = r == slott_ref[0, 0:1, :]
    for kk in range(1, TOP_K):
        hit = hit | (r == slott_ref[0, kk:kk + 1, :])
    sel = jnp.where(hit, 1.0, 0.0).astype(BF16)
    srt = jnp.dot(sel, x_ref[...].astype(BF16), preferred_element_type=F32)
    srt_scr[cur] = _pack_rows(srt)

    def out_copy(buf):
        return lambda local, remote, rows: pltpu.make_async_copy(
            srt_scr.at[buf, pl.ds(local, rows)], xs_hbm.at[pl.ds(remote, rows)], sem.at[buf])

    _issue_groups(tab_ref, out_copy(cur))

    @pl.when(i > 0)
    def _():
        _wait_groups(tab_prev_ref, out_copy(1 - cur))

    @pl.when(i == last)
    def _():
        _wait_groups(tab_ref, out_copy(cur))
        used = pl.multiple_of(tab_ref[0, 0, TAB_W - 1], SUBLANES)
        free = xs_hbm.shape[0] - used
        n_big = free // BM_EXP
        n_small = (free - n_big * BM_EXP) // SUBLANES
        srt_scr[0, 0:BM_EXP, :] = jnp.zeros((BM_EXP, HALF_D), U32)

        def big(i):
            return pltpu.make_async_copy(
                srt_scr.at[0, pl.ds(0, BM_EXP)],
                xs_hbm.at[pl.ds(pl.multiple_of(used + i * BM_EXP, SUBLANES), BM_EXP)], sem.at[0])

        def small(i):
            return pltpu.make_async_copy(
                srt_scr.at[0, pl.ds(0, SUBLANES)],
                xs_hbm.at[pl.ds(pl.multiple_of(used + n_big * BM_EXP + i * SUBLANES, SUBLANES), SUBLANES)],
                sem.at[0])

        def for_each(make, count, op):
            def body(i, c):
                op(make(i))
                return c
            lax.fori_loop(0, count, body, 0)

        for op in (lambda cp: cp.start(), lambda cp: cp.wait()):
            for_each(big, n_big, op)
            for_each(small, n_small, op)


def _dispatch(tab, x1, slott, m_pad):
    N = x1.shape[0]
    return pl.pallas_call(
        _dispatch_kernel,
        grid=(N // TM_ROW,),
        in_specs=[pl.BlockSpec((1, 1, TAB_W), lambda i: (i, 0, 0), memory_space=pltpu.SMEM),
                  pl.BlockSpec((1, 1, TAB_W), lambda i: (jnp.maximum(i - 1, 0), 0, 0), memory_space=pltpu.SMEM),
                  pl.BlockSpec((TM_ROW, D_MODEL), lambda i: (i, 0)),
                  pl.BlockSpec((1, SUBLANES, TM_ROW), lambda i: (i, 0, 0))],
        out_specs=pl.BlockSpec(memory_space=pl.ANY),
        out_shape=jax.ShapeDtypeStruct((m_pad, HALF_D), U32),
        scratch_shapes=[pltpu.VMEM((2, ROWS_TILE, HALF_D), U32), pltpu.SemaphoreType.DMA((2,))],
        compiler_params=_cparams(("arbitrary",), has_side_effects=True),
        name="moe_dispatch",
    )(tab, tab, x1, slott)


def _expert_kernel(blk_ref, exp_ref, lo_ref, hi_ref, xs_ref, wu_ref, bu_ref, wd_ref, bd_ref, o_ref):
    w = pl.program_id(0)
    lo = lo_ref[w]
    hi = hi_ref[w]

    @pl.when(lo > hi)
    def _():
        o_ref[...] = jnp.zeros_like(o_ref)

    @pl.when(hi > lo)
    def _():
        xh, xl = _unpack_rows(xs_ref[...])
        x = jnp.concatenate([xh.astype(BF16), xl.astype(BF16)], axis=1)
        hu = jnp.dot(x, wu_ref[0], preferred_element_type=F32) + bu_ref[0]
        chunks = []
        for cc in range(D_FF // LANES):
            g = jnp.minimum(hu[:, 2 * cc * LANES:(2 * cc + 1) * LANES], SWIGLU_LIMIT)
            l = jnp.clip(hu[:, (2 * cc + 1) * LANES:(2 * cc + 2) * LANES], -SWIGLU_LIMIT, SWIGLU_LIMIT)
            chunks.append(g * _sigmoid(SWIGLU_ALPHA * g) * (l + 1.0))
        act = jnp.concatenate(chunks, axis=1)
        y = jnp.dot(act.astype(BF16), wd_ref[0], preferred_element_type=F32) + bd_ref[0]
        packed = _pack_rows(y)

        @pl.when(lo == 0)
        def _():
            o_ref[...] = packed

        @pl.when(lo > 0)
        def _():
            row = lax.broadcasted_iota(I32, packed.shape, 0)
            o_ref[...] = jnp.where((row >= lo) & (row < hi), packed, o_ref[...])


def _expert_weight_prep_kernel(wu_ref, wd_ref, perm_ref, ou_ref, od_ref):
    for cc in range(2 * D_FF // (2 * LANES)):
        cols = slice(cc * 2 * LANES, (cc + 1) * 2 * LANES)
        ou_ref[0, :, cols] = jnp.dot(wu_ref[0, 0, :, cols].astype(BF16), perm_ref[...],
                                     preferred_element_type=F32).astype(BF16)
    od_ref[0] = wd_ref[0, 0].astype(BF16)


def _expert_weight_prep(w_up, w_down, layer, perm):
    E = w_up.shape[1]
    return pl.pallas_call(
        _expert_weight_prep_kernel,
        grid=(E,),
        in_specs=[pl.BlockSpec((1, 1, D_MODEL, 2 * D_FF), lambda e: (layer, e, 0, 0)),
                  pl.BlockSpec((1, 1, D_FF, D_MODEL), lambda e: (layer, e, 0, 0)),
                  pl.BlockSpec((2 * LANES, 2 * LANES), lambda e: (0, 0))],
        out_specs=[pl.BlockSpec((1, D_MODEL, 2 * D_FF), lambda e: (e, 0, 0)),
                   pl.BlockSpec((1, D_FF, D_MODEL), lambda e: (e, 0, 0))],
        out_shape=[jax.ShapeDtypeStruct((E, D_MODEL, 2 * D_FF), BF16),
                   jax.ShapeDtypeStruct((E, D_FF, D_MODEL), BF16)],
        compiler_params=_cparams(("parallel",)),
        name="expert_weight_prep",
    )(w_up, w_down, perm)


def _pair_perm():
    p = np.zeros((2 * LANES, 2 * LANES), np.float32)
    j = np.arange(LANES)
    p[2 * j, j] = 1.0
    p[2 * j + 1, LANES + j] = 1.0
    return jnp.asarray(p, BF16)


def _experts(meta, xs, wu, bu, wd, bd):
    M = xs.shape[0]
    W = meta[0].shape[0]
    by_expert = lambda *shape: pl.BlockSpec((1,) + shape, lambda w, blk, ex, lo, hi: (ex[w], 0, 0))
    rows = pl.BlockSpec((BM_EXP, HALF_D), lambda w, blk, ex, lo, hi: (blk[w], 0))
    return pl.pallas_call(
        _expert_kernel,
        grid_spec=pltpu.PrefetchScalarGridSpec(
            num_scalar_prefetch=4,
            grid=(W,),
            in_specs=[rows, by_expert(D_MODEL, 2 * D_FF), by_expert(1, 2 * D_FF),
                      by_expert(D_FF, D_MODEL), by_expert(1, D_MODEL)],
            out_specs=rows,
        ),
        out_shape=jax.ShapeDtypeStruct((M, HALF_D), U32),
        compiler_params=_cparams(("arbitrary",)),
        name="moe_experts",
    )(*meta, xs, wu, bu, wd, bd)


def _combine_kernel(tab_ref, tab_next_ref, x_ref, gate_ref, slot_ref, g_ref, b_ref, yb_hbm, o_ref, buf, sem):
    tm = x_ref.shape[0]
    i = pl.program_id(0)
    cur = lax.rem(i, 2)

    def in_copy(b):
        return lambda local, remote, rows: pltpu.make_async_copy(
            yb_hbm.at[pl.ds(remote, rows)], buf.at[b, pl.ds(local, rows)], sem.at[b])

    @pl.when(i == 0)
    def _():
        buf[...] = jnp.zeros_like(buf)
        _issue_groups(tab_ref, in_copy(cur))

    @pl.when(i + 1 < pl.num_programs(0))
    def _():
        _issue_groups(tab_next_ref, in_copy(1 - cur))

    r = lax.broadcasted_iota(I32, (tm, ROWS_TILE), 1)
    slots = slot_ref[...]
    gates = gate_ref[...]
    pw = jnp.zeros((tm, ROWS_TILE), F32)
    for kk in range(TOP_K):
        pw = jnp.where(r == slots[:, kk:kk + 1], gates[:, kk:kk + 1], pw)
    pw_hi = pw.astype(BF16)
    pw_lo = (pw - pw_hi.astype(F32)).astype(BF16)
    _wait_groups(tab_ref, in_copy(cur))
    yh, yl = _unpack_rows(buf[cur])
    ys = jnp.concatenate([yh.astype(BF16), yl.astype(BF16)], axis=1)
    y = jnp.dot(pw_hi, ys, preferred_element_type=F32) + jnp.dot(pw_lo, ys, preferred_element_type=F32)
    o_ref[...] = _layer_norm(DEEPNORM_ALPHA * x_ref[...] + y, g_ref[...], b_ref[...])


def _combine(tab, x1, gate, slot, g, b, yb):
    N = x1.shape[0]
    nt = N // TM_ROW
    return pl.pallas_call(
        _combine_kernel,
        grid=(N // TM_ROW,),
        in_specs=[pl.BlockSpec((1, 1, TAB_W), lambda i: (i, 0, 0), memory_space=pltpu.SMEM),
                  pl.BlockSpec((1, 1, TAB_W), lambda i: (jnp.minimum(i + 1, nt - 1), 0, 0), memory_space=pltpu.SMEM),
                  pl.BlockSpec((TM_ROW, D_MODEL), lambda i: (i, 0)),
                  pl.BlockSpec((TM_ROW, LANES), lambda i: (i, 0)),
                  pl.BlockSpec((TM_ROW, LANES), lambda i: (i, 0)),
                  pl.BlockSpec((1, D_MODEL), lambda i: (0, 0)),
                  pl.BlockSpec((1, D_MODEL), lambda i: (0, 0)),
                  pl.BlockSpec(memory_space=pl.ANY)],
        out_specs=pl.BlockSpec((TM_ROW, D_MODEL), lambda i: (i, 0)),
        out_shape=jax.ShapeDtypeStruct((N, D_MODEL), F32),
        scratch_shapes=[pltpu.VMEM((2, ROWS_TILE, HALF_D), U32), pltpu.SemaphoreType.DMA((2,))],
        compiler_params=_cparams(("arbitrary",)),
        name="moe_combine_ln2",
    )(tab, tab, x1, gate, slot, g, b, yb)


def _moe_tables(cnt, m_pad):
    n_pad = cnt[:, :, 0]
    meta, region = _work_items(n_pad.sum(axis=0), m_pad)
    run_end = jnp.cumsum(n_pad, axis=1)
    run_start = run_end - n_pad
    hbm_start = region[None, :] + jnp.cumsum(n_pad, axis=0) - n_pad
    first_row = SUBLANES * jnp.arange(GROUPS_TILE, dtype=I32)
    owner = jnp.minimum((run_end[:, None, :] <= first_row[None, :, None]).sum(axis=-1), N_EXPERTS - 1)
    sel = owner[:, :, None] == jnp.arange(N_EXPERTS, dtype=I32)[None, None, :]
    shift = jnp.sum(jnp.where(sel, (hbm_start - run_start)[:, None, :], 0), axis=-1)
    n_groups = run_end[:, -1:] // SUBLANES
    pad = jnp.zeros((n_pad.shape[0], TAB_W - 2 - GROUPS_TILE), I32)
    used = jnp.broadcast_to(n_pad.sum(), n_groups.shape)
    tab = jnp.concatenate([n_groups, shift + first_row[None, :], pad, used], axis=1).astype(I32)
    return tab[:, None, :], meta


def _work_items(counts, M):
    nblk = M // BM_EXP
    W = nblk + N_EXPERTS
    ends = jnp.cumsum(counts)
    starts = ends - counts
    first = starts // BM_EXP
    last = jnp.maximum(ends - 1, 0) // BM_EXP
    n_items = jnp.where(counts > 0, last - first + 1, 0)
    item_end = jnp.cumsum(n_items)
    item_start = item_end - n_items
    total = item_end[-1]
    w = jnp.arange(W, dtype=I32)
    w_eff = jnp.minimum(w, total - 1)
    owner = (item_end[None, :] <= w_eff[:, None]).sum(axis=1).astype(I32)
    sel = owner[:, None] == jnp.arange(N_EXPERTS, dtype=I32)[None, :]
    pick = lambda table: jnp.sum(jnp.where(sel, table[None, :], 0), axis=1)
    blk = pick(first) + (w_eff - pick(item_start))
    lo = jnp.clip(pick(starts) - blk * BM_EXP, 0, BM_EXP)
    hi = jnp.clip(pick(ends) - blk * BM_EXP, 0, BM_EXP)
    valid = w < total
    used_blocks = (ends[-1] + BM_EXP - 1) // BM_EXP
    spare_blk = used_blocks + (w - total)
    fill = (~valid) & (spare_blk < nblk)
    blk = jnp.where(valid, blk, jnp.minimum(spare_blk, nblk - 1))
    lo = jnp.where(valid, lo, jnp.where(fill, 1, 0))
    hi = jnp.where(valid, hi, 0)
    return (blk.astype(I32), owner, lo.astype(I32), hi.astype(I32)), starts


def _block_diag(blocks):
    G, a, b = blocks.shape
    eye = jnp.eye(G, dtype=blocks.dtype)
    return jnp.einsum("gab,gh->gahb", blocks, eye).reshape(G * a, G * b)


def kernel(x, w_in, b_forget, w_pool, pool_scale, ssm_lambda_re, ssm_lambda_im, ssm_log_dt, ssm_b_re, ssm_b_im, ssm_c_re, ssm_c_im, ssm_d, w_glu, b_glu, w_branch_a, w_branch_b, w_branch_c, w_out, ln1_g, ln1_b, w_router, b_router, w_up, b_up, w_down, b_down, ln2_g, ln2_b):
    B, S, D = x.shape
    assert D == D_MODEL and S % T_ATT == 0 and S % TM_PROJ == 0
    N = B * S
    m_pad = (N // TM_ROW) * ROWS_TILE
    assert m_pad % BM_EXP == 0
    perm = _pair_perm()
    x2 = x.reshape(N, D)
    scale = ATT_HEAD_DIM ** -0.5
    for l in range(DEPTH):
        wl = w_in[l]
        wq = wl[:, :ATT_WIDTH] * scale
        wqkv = jnp.concatenate([wq, wl[:, ATT_WIDTH:QKV_W]], axis=1).astype(BF16)
        c0 = QKV_W
        wf = wl[:, c0:c0 + ATT_HEADS]
        wsmall = jnp.concatenate(
            [wl[:, c0 + ATT_HEADS:c0 + ATT_HEADS + POOL_WIDTH + SSM_WIDTH], wf,
             jnp.zeros((D, F_PAD - ATT_HEADS), F32)], axis=1).astype(BF16)
        wgates = wl[:, c0 + ATT_HEADS + POOL_WIDTH + SSM_WIDTH:].astype(BF16)

        q, k, v, u_pool, u_ssm_tm, f_pad = _proj(x2, wqkv, wsmall, B, S)

        f_rows = f_pad[:, :ATT_HEADS].reshape(B, S, ATT_HEADS).transpose(0, 2, 1).reshape(B * ATT_HEADS, S)
        b_rows = jnp.tile(b_forget[l], B).reshape(B * ATT_HEADS, 1)
        c = _fcum(f_rows, b_rows).reshape(B, ATT_HEADS, 1, S)
        y_a = _attention(q, k, v, c, B, S).reshape(N, ATT_WIDTH)

        w_pool_bd = _block_diag(w_pool[l]).astype(BF16)
        y_b = _pool(u_pool.reshape(B, S, POOL_WIDTH), w_pool_bd, pool_scale[l].reshape(1, POOL_WIDTH)).reshape(N, POOL_WIDTH)

        ar, ai, bbrT, bbiT = _ssm_prep(ssm_lambda_re[l], ssm_lambda_im[l], ssm_log_dt[l], ssm_b_re[l], ssm_b_im[l])
        bblk = jnp.concatenate([_block_diag(bbrT), _block_diag(bbiT)], axis=1).astype(BF16)
        cblk = jnp.concatenate([_block_diag(ssm_c_re[l].transpose(0, 2, 1)),
                                -_block_diag(ssm_c_im[l].transpose(0, 2, 1))], axis=0).astype(BF16)
        ar_b = jnp.broadcast_to(ar.reshape(1, SSM_STATES), (B, SSM_STATES))
        ai_b = jnp.broadcast_to(ai.reshape(1, SSM_STATES), (B, SSM_STATES))
        y_c_tm = _ssm(u_ssm_tm.reshape(S * B, SSM_WIDTH), bblk, ar_b, ai_b, cblk,
                      ssm_d[l].reshape(1, SSM_WIDTH), w_glu[l].astype(BF16), b_glu[l].reshape(1, SSM_WIDTH), B, S)

        x1 = _merge(x2, y_a, y_b, y_c_tm.reshape(S, B * SSM_WIDTH), wgates,
                    w_branch_a[l].astype(BF16), w_branch_b[l].astype(BF16), w_branch_c[l].astype(BF16),
                    w_out[l].astype(BF16), ln1_g[l].reshape(1, D), ln1_b[l].reshape(1, D), B, S)

        wr = jnp.concatenate([w_router[l], jnp.zeros((D, LANES - N_EXPERTS), F32)], axis=1)
        br = jnp.concatenate([b_router[l], jnp.full((LANES - N_EXPERTS,), _NEG_BIG, F32)]).reshape(1, LANES)
        gate, slot, slott, cnt = _route(x1, wr, br)
        tab, meta = _moe_tables(cnt, m_pad)
        xs = _dispatch(tab, x1, slott, m_pad)
        bu = b_up[l].reshape(N_EXPERTS, D_FF // LANES, LANES, 2).transpose(0, 1, 3, 2).reshape(N_EXPERTS, 1, 2 * D_FF)
        wu, wd = _expert_weight_prep(w_up, w_down, l, perm)
        yb = _experts(meta, xs, wu, bu, wd, b_down[l][:, None, :])
        x2 = _combine(tab, x1, gate, slot, ln2_g[l].reshape(1, D), ln2_b[l].reshape(1, D), yb)
    return x2.reshape(B, S, D)
```

```python
import math

import jax
import jax.numpy as jnp
import numpy as np
from jax import lax
from jax.experimental import pallas as pl
from jax.experimental.pallas import tpu as pltpu

F32 = jnp.float32
BF16 = jnp.bfloat16
I32 = jnp.int32
U32 = jnp.uint32

D_MODEL = 1024
DEPTH = 4
ATT_HEADS = 8
ATT_HEAD_DIM = 64
ATT_WIDTH = ATT_HEADS * ATT_HEAD_DIM
POOL_WINDOWS = (2, 4, 8, 16)
POOL_GROUPS = 4
POOL_WIDTH = 256
POOL_GROUP_DIM = 64
MAX_WINDOW = max(POOL_WINDOWS)
SSM_WIDTH = 256
SSM_GROUP_DIM = 16
SSM_GROUPS = 16
SSM_STATE = 64
SSM_STATES = SSM_GROUPS * SSM_STATE
N_BRANCH = 3
N_EXPERTS = 32
TOP_K = 4
D_FF = D_MODEL
SWIGLU_LIMIT = 7.0
SWIGLU_ALPHA = 1.702
LN_EPS = 1e-5
DEEPNORM_ALPHA = (2.0 * DEPTH) ** 0.25
GELU_C = math.sqrt(2.0 / math.pi)

LANES = 128
HALF_D = D_MODEL // 2
F_PAD = LANES
V_PAD = LANES
SMALL_W = POOL_WIDTH + SSM_WIDTH + F_PAD
QKV_W = 3 * ATT_WIDTH
TM_PROJ = 512
T_ATT = 1024
TM_POOL = 256
TT_SSM = 64
TM_MERGE = 512
SUBLANES = 8
TM_ROW = 256
ROWS_TILE = TM_ROW * TOP_K + N_EXPERTS * SUBLANES
GROUPS_TILE = ROWS_TILE // SUBLANES
TAB_W = 2 * LANES
assert TAB_W >= GROUPS_TILE + 2
BM_EXP = 512
VMEM_LIMIT = 52 * 1024 * 1024

_NEG_BIG = -1e30


def _cparams(sem, **kw):
    return pltpu.CompilerParams(dimension_semantics=sem, vmem_limit_bytes=VMEM_LIMIT, **kw)


def _sigmoid(x):
    return 1.0 / (1.0 + jnp.exp(-x))


def _layer_norm(z, g, b):
    mu = jnp.mean(z, axis=-1, keepdims=True)
    zc = z - mu
    var = jnp.mean(zc * zc, axis=-1, keepdims=True)
    return zc * lax.rsqrt(var + LN_EPS) * g + b


def _pack_rows(y):
    u = pltpu.bitcast(y.astype(BF16).astype(F32), U32)
    return u[:, :HALF_D] | (u[:, HALF_D:] >> 16)


def _unpack_rows(p):
    hi = pltpu.bitcast(p & jnp.uint32(0xFFFF0000), F32)
    lo = pltpu.bitcast(p << 16, F32)
    return hi, lo


def _proj_kernel(x_ref, wqkv_ref, ws_ref, q_ref, k_ref, v_ref, up_ref, us_ref, f_ref):
    xb = x_ref[...].astype(BF16)
    for part, ref in enumerate((q_ref, k_ref)):
        h = jnp.dot(xb, wqkv_ref[:, part * ATT_WIDTH:(part + 1) * ATT_WIDTH], preferred_element_type=F32)
        for hh in range(ATT_HEADS):
            ref[0, hh] = h[:, hh * ATT_HEAD_DIM:(hh + 1) * ATT_HEAD_DIM].astype(BF16)
    h = jnp.dot(xb, wqkv_ref[:, 2 * ATT_WIDTH:], preferred_element_type=F32)
    lane = lax.broadcasted_iota(I32, (h.shape[0], V_PAD), 1)
    tail = jnp.where(lane == ATT_HEAD_DIM, 1.0, 0.0)
    for pair in range(ATT_HEADS // 2):
        slab = h[:, pair * V_PAD:(pair + 1) * V_PAD]
        v_ref[0, 2 * pair] = jnp.where(lane < ATT_HEAD_DIM, slab, tail).astype(BF16)
        v_ref[0, 2 * pair + 1] = jnp.where(lane < ATT_HEAD_DIM, pltpu.roll(slab, ATT_HEAD_DIM, 1), tail).astype(BF16)
    hs = jnp.dot(xb, ws_ref[...], preferred_element_type=F32)
    up_ref[...] = hs[:, :POOL_WIDTH]
    us_ref[...] = hs[:, POOL_WIDTH:POOL_WIDTH + SSM_WIDTH]
    f_ref[...] = hs[:, POOL_WIDTH + SSM_WIDTH:]


def _proj(x2, wqkv, wsmall, B, S):
    N = B * S
    nS = S // TM_PROJ
    hm = jax.ShapeDtypeStruct((B, ATT_HEADS, S, ATT_HEAD_DIM), BF16)
    hm_spec = pl.BlockSpec((1, ATT_HEADS, TM_PROJ, ATT_HEAD_DIM), lambda b, s: (b, 0, s, 0))
    hv = jax.ShapeDtypeStruct((B, ATT_HEADS, S, V_PAD), BF16)
    hv_spec = pl.BlockSpec((1, ATT_HEADS, TM_PROJ, V_PAD), lambda b, s: (b, 0, s, 0))
    return pl.pallas_call(
        _proj_kernel,
        grid=(B, nS),
        in_specs=[
            pl.BlockSpec((TM_PROJ, D_MODEL), lambda b, s: (b * nS + s, 0)),
            pl.BlockSpec((D_MODEL, QKV_W), lambda b, s: (0, 0)),
            pl.BlockSpec((D_MODEL, SMALL_W), lambda b, s: (0, 0)),
        ],
        out_specs=[
            hm_spec, hm_spec, hv_spec,
            pl.BlockSpec((TM_PROJ, POOL_WIDTH), lambda b, s: (b * nS + s, 0)),
            pl.BlockSpec((TM_PROJ, SSM_WIDTH), lambda b, s: (s, b)),
            pl.BlockSpec((TM_PROJ, F_PAD), lambda b, s: (b * nS + s, 0)),
        ],
        out_shape=[
            hm, hm, hv,
            jax.ShapeDtypeStruct((N, POOL_WIDTH), F32),
            jax.ShapeDtypeStruct((S, B * SSM_WIDTH), F32),
            jax.ShapeDtypeStruct((N, F_PAD), F32),
        ],
        compiler_params=_cparams(("parallel", "parallel")),
        name="in_proj",
    )(x2, wqkv, wsmall)


def _fcum_kernel(f_ref, b_ref, c_ref):
    rows, S = f_ref.shape
    lane = lax.broadcasted_iota(I32, (rows, LANES), 1)
    carry = jnp.zeros((rows, 1), F32)
    for ch in range(S // LANES):
        z = f_ref[:, ch * LANES:(ch + 1) * LANES] + b_ref[...]
        lf = jnp.minimum(z, 0.0) - jnp.log1p(jnp.exp(-jnp.abs(z)))
        sh = 1
        while sh < LANES:
            lf = lf + jnp.where(lane >= sh, pltpu.roll(lf, sh, 1), 0.0)
            sh *= 2
        lf = lf + carry
        c_ref[:, ch * LANES:(ch + 1) * LANES] = lf
        carry = lf[:, LANES - 1:LANES]


def _fcum(f_rows, b_rows):
    rows, S = f_rows.shape
    return pl.pallas_call(
        _fcum_kernel,
        out_shape=jax.ShapeDtypeStruct((rows, S), F32),
        compiler_params=pltpu.CompilerParams(vmem_limit_bytes=VMEM_LIMIT),
        name="forget_cumsum",
    )(f_rows, b_rows)


def _attn_kernel(q_ref, k_ref, v_ref, c_ref, o_ref):
    T = T_ATT
    H = T // 2
    qi = pl.program_id(2)
    qs = [q_ref[0, hh] for hh in range(2)]

    def update(carry, q_pair, key_start, n_keys, diag):
        out = []
        for hh in range(2):
            m, acc = carry[2 * hh], carry[2 * hh + 1]
            k = k_ref[0, hh, pl.ds(key_start, n_keys), :]
            v = v_ref[0, hh, pl.ds(key_start, n_keys), :]
            s = lax.dot_general(q_pair[hh], k, (((1,), (1,)), ((), ())), preferred_element_type=F32)
            s = s - c_ref[0, hh, :, pl.ds(key_start, n_keys)]
            if diag is not None:
                delta = lax.broadcasted_iota(I32, s.shape, 1) - lax.broadcasted_iota(I32, s.shape, 0)
                s = jnp.where(delta <= diag, s, -jnp.inf)
            m_new = jnp.maximum(m, jnp.max(s, axis=-1, keepdims=True))
            p = jnp.exp(s - m_new)
            acc = jnp.exp(m - m_new) * acc + jnp.dot(p.astype(BF16), v, preferred_element_type=F32)
            out += [m_new, acc]
        return tuple(out)

    m0 = jnp.full((T, 1), -jnp.inf, F32)
    acc0 = jnp.zeros((T, V_PAD), F32)
    carry = lax.fori_loop(
        0, qi, lambda j, c: update(c, qs, pl.multiple_of(j * T, T), T, None), (m0, acc0, m0, acc0))
    diag_start = pl.multiple_of(qi * T, T)
    halves = []
    for half in range(2):
        rows = slice(half * H, (half + 1) * H)
        halves.append(update(tuple(x[rows] for x in carry), [q[rows] for q in qs],
                             diag_start, (half + 1) * H, half * H))
    for hh in range(2):
        acc = jnp.concatenate([halves[0][2 * hh + 1], halves[1][2 * hh + 1]], axis=0)
        out = acc[:, :ATT_HEAD_DIM] / acc[:, ATT_HEAD_DIM:ATT_HEAD_DIM + 1]
        o_ref[0, :, hh * ATT_HEAD_DIM:(hh + 1) * ATT_HEAD_DIM] = out.astype(BF16)


def _attention(q, k, v, c, B, S):
    nq = S // T_ATT
    qspec = pl.BlockSpec((1, 2, T_ATT, ATT_HEAD_DIM), lambda b, hp, i: (b, hp, i, 0))
    kspec = pl.BlockSpec((1, 2, S, ATT_HEAD_DIM), lambda b, hp, i: (b, hp, 0, 0))
    vspec = pl.BlockSpec((1, 2, S, V_PAD), lambda b, hp, i: (b, hp, 0, 0))
    return pl.pallas_call(
        _attn_kernel,
        grid=(B, ATT_HEADS // 2, nq),
        in_specs=[qspec, kspec, vspec,
                  pl.BlockSpec((1, 2, 1, S), lambda b, hp, i: (b, hp, 0, 0))],
        out_specs=pl.BlockSpec((1, T_ATT, 2 * ATT_HEAD_DIM), lambda b, hp, i: (b, i, hp)),
        out_shape=jax.ShapeDtypeStruct((B, S, ATT_WIDTH), BF16),
        compiler_params=_cparams(("parallel", "parallel", "arbitrary")),
        name="fox_attention",
    )(q, k, v, c)


def _pool_kernel(u_ref, w_ref, sc_ref, o_ref, pad_ref):
    S = u_ref.shape[1]
    R = TM_POOL
    pad_ref[0:MAX_WINDOW, :] = jnp.zeros((MAX_WINDOW, POOL_WIDTH), F32)
    pad_ref[MAX_WINDOW:, :] = u_ref[0]
    lane = lax.broadcasted_iota(I32, (R, POOL_WIDTH), 1)
    trow = lax.broadcasted_iota(I32, (R, POOL_WIDTH), 0)
    grp = lane // POOL_GROUP_DIM
    win = jnp.where(grp == 0, POOL_WINDOWS[0],
                    jnp.where(grp == 1, POOL_WINDOWS[1], jnp.where(grp == 2, POOL_WINDOWS[2], POOL_WINDOWS[3])))
    for i in range(S // R):
        base = MAX_WINDOW + i * R
        u0 = pad_ref[base:base + R, :]
        acc = u0
        sums = {}
        for kk in range(1, MAX_WINDOW):
            acc = acc + pad_ref[base - kk:base - kk + R, :]
            if kk + 1 in POOL_WINDOWS:
                sums[kk + 1] = acc
        total = jnp.where(grp == 0, sums[POOL_WINDOWS[0]],
                          jnp.where(grp == 1, sums[POOL_WINDOWS[1]],
                                    jnp.where(grp == 2, sums[POOL_WINDOWS[2]], sums[POOL_WINDOWS[3]])))
        cnt = jnp.minimum(trow + (i * R + 1), win).astype(F32)
        mixed = total / cnt - u0
        y = jnp.dot(mixed.astype(BF16), w_ref[...], preferred_element_type=F32) * sc_ref[...]
        o_ref[0, i * R:(i + 1) * R, :] = y.astype(BF16)


def _pool(u3, w_bd, scale):
    B, S, _ = u3.shape
    return pl.pallas_call(
        _pool_kernel,
        grid=(B,),
        in_specs=[pl.BlockSpec((1, S, POOL_WIDTH), lambda b: (b, 0, 0)),
                  pl.BlockSpec((POOL_WIDTH, POOL_WIDTH), lambda b: (0, 0)),
                  pl.BlockSpec((1, POOL_WIDTH), lambda b: (0, 0))],
        out_specs=pl.BlockSpec((1, S, POOL_WIDTH), lambda b: (b, 0, 0)),
        out_shape=jax.ShapeDtypeStruct((B, S, POOL_WIDTH), BF16),
        scratch_shapes=[pltpu.VMEM((S + MAX_WINDOW, POOL_WIDTH), F32)],
        compiler_params=_cparams(("parallel",)),
        name="multiscale_pool",
    )(u3, w_bd, scale)


def _ssm_prep_kernel(lr_ref, li_ref, ldt_ref, brT_ref, biT_ref, ar_ref, ai_ref, bbr_ref, bbi_ref):
    lr = lr_ref[...]
    li = li_ref[...]
    dt = jnp.exp(ldt_ref[...])
    mag = jnp.exp(lr * dt)
    ar = mag * jnp.cos(li * dt)
    ai = mag * jnp.sin(li * dt)
    den = lr * lr + li * li
    nr = ar - 1.0
    zr = (nr * lr + ai * li) / den
    zi = (ai * lr - nr * li) / den
    ar_ref[...] = ar
    ai_ref[...] = ai
    br = brT_ref[...]
    bi = biT_ref[...]
    bbr_ref[...] = zr[:, None, :] * br - zi[:, None, :] * bi
    bbi_ref[...] = zr[:, None, :] * bi + zi[:, None, :] * br


def _ssm_prep(lr, li, log_dt, b_re, b_im):
    G, P, H = b_re.shape
    gp = jax.ShapeDtypeStruct((G, P), F32)
    ghp = jax.ShapeDtypeStruct((G, H, P), F32)
    return pl.pallas_call(
        _ssm_prep_kernel,
        out_shape=[gp, gp, ghp, ghp],
        name="s5_discretise",
    )(lr, li, log_dt.reshape(G, 1), b_re.transpose(0, 2, 1), b_im.transpose(0, 2, 1))


def _gelu_tanh(y):
    return 0.5 * y * (1.0 + jnp.tanh(GELU_C * (y + 0.044715 * (y * y * y))))


def _ssm_kernel(u_ref, bb_ref, ar_ref, ai_ref, cc_ref, d_ref, wg_ref, bg_ref, o_ref, x_scr, hr_scr, hi_scr):
    nb = hr_scr.shape[0]
    tt = u_ref.shape[0] // nb

    @pl.when(pl.program_id(0) == 0)
    def _():
        hr_scr[...] = jnp.zeros_like(hr_scr)
        hi_scr[...] = jnp.zeros_like(hi_scr)

    u = u_ref[...]
    x_scr[...] = jnp.dot(u.astype(BF16), bb_ref[...], preferred_element_type=F32)

    def step(t, carry):
        hr, hi = carry
        r0 = pl.multiple_of(t * nb, nb)
        xr = x_scr[pl.ds(r0, nb), 0:SSM_STATES]
        xi = x_scr[pl.ds(r0, nb), SSM_STATES:2 * SSM_STATES]
        ar = ar_ref[...]
        ai = ai_ref[...]
        nr = ar * hr - ai * hi + xr
        ni = ar * hi + ai * hr + xi
        x_scr[pl.ds(r0, nb), 0:SSM_STATES] = nr
        x_scr[pl.ds(r0, nb), SSM_STATES:2 * SSM_STATES] = ni
        return nr, ni

    hr, hi = lax.fori_loop(0, tt, step, (hr_scr[...], hi_scr[...]))
    hr_scr[...] = hr
    hi_scr[...] = hi
    y = jnp.dot(x_scr[...].astype(BF16), cc_ref[...], preferred_element_type=F32) + d_ref[...] * u
    y = _gelu_tanh(y)
    gl = jnp.dot(y.astype(BF16), wg_ref[...], preferred_element_type=F32) + bg_ref[...]
    o_ref[...] = (y * _sigmoid(gl)).astype(BF16)


def _ssm(u_tm, bblk, ar_b, ai_b, cblk, dvec, wglu, bglu, B, S):
    rows = TT_SSM * B
    const = lambda shape: pl.BlockSpec(shape, lambda i: (0, 0))
    return pl.pallas_call(
        _ssm_kernel,
        grid=(S // TT_SSM,),
        in_specs=[pl.BlockSpec((rows, SSM_WIDTH), lambda i: (i, 0)),
                  const((SSM_WIDTH, 2 * SSM_STATES)),
                  const((B, SSM_STATES)), const((B, SSM_STATES)),
                  const((2 * SSM_STATES, SSM_WIDTH)),
                  const((1, SSM_WIDTH)), const((SSM_WIDTH, SSM_WIDTH)), const((1, SSM_WIDTH))],
        out_specs=pl.BlockSpec((rows, SSM_WIDTH), lambda i: (i, 0)),
        out_shape=jax.ShapeDtypeStruct((S * B, SSM_WIDTH), BF16),
        scratch_shapes=[pltpu.VMEM((rows, 2 * SSM_STATES), F32),
                        pltpu.VMEM((B, SSM_STATES), F32), pltpu.VMEM((B, SSM_STATES), F32)],
        compiler_params=_cparams(("arbitrary",)),
        name="s5_scan",
    )(u_tm, bblk, ar_b, ai_b, cblk, dvec, wglu, bglu)


def _merge_kernel(x_ref, ya_ref, yb_ref, yc_ref, wg_ref, wa_ref, wb_ref, wc_ref, wo_ref, g_ref, b_ref, o_ref):
    x = x_ref[...]
    xb = x.astype(BF16)
    merged = None
    for i, (y_ref, w_ref) in enumerate(((ya_ref, wa_ref), (yb_ref, wb_ref), (yc_ref, wc_ref))):
        gate = _sigmoid(jnp.dot(xb, wg_ref[:, i * D_MODEL:(i + 1) * D_MODEL], preferred_element_type=F32))
        term = gate * jnp.dot(y_ref[...], w_ref[...], preferred_element_type=F32)
        merged = term if merged is None else merged + term
    mix = jnp.dot(merged.astype(BF16), wo_ref[...], preferred_element_type=F32)
    o_ref[...] = _layer_norm(DEEPNORM_ALPHA * x + mix, g_ref[...], b_ref[...])


def _merge(x2, ya, yb, yc_tm, wg, wa, wb, wc, wo, g, b, B, S):
    N = B * S
    nS = S // TM_MERGE
    const = lambda shape: pl.BlockSpec(shape, lambda bb, s: (0, 0))
    row = lambda w: pl.BlockSpec((TM_MERGE, w), lambda bb, s: (bb * nS + s, 0))
    return pl.pallas_call(
        _merge_kernel,
        grid=(B, nS),
        in_specs=[row(D_MODEL), row(ATT_WIDTH), row(POOL_WIDTH),
                  pl.BlockSpec((TM_MERGE, SSM_WIDTH), lambda bb, s: (s, bb)),
                  const((D_MODEL, N_BRANCH * D_MODEL)), const((ATT_WIDTH, D_MODEL)),
                  const((POOL_WIDTH, D_MODEL)), const((SSM_WIDTH, D_MODEL)), const((D_MODEL, D_MODEL)),
                  const((1, D_MODEL)), const((1, D_MODEL))],
        out_specs=row(D_MODEL),
        out_shape=jax.ShapeDtypeStruct((N, D_MODEL), F32),
        compiler_params=_cparams(("parallel", "parallel")),
        name="merge_ln1",
    )(x2, ya, yb, yc_tm, wg, wa, wb, wc, wo, g, b)


def _route_kernel(x_ref, wr_ref, br_ref, gate_ref, slot_ref, slott_ref, cnt_ref):
    tm = x_ref.shape[0]
    x = x_ref[...]
    xh = x.astype(BF16)
    xl = (x - xh.astype(F32)).astype(BF16)
    w = wr_ref[...]
    wh = w.astype(BF16)
    wl = (w - wh.astype(F32)).astype(BF16)
    logits = (jnp.dot(xh, wh, preferred_element_type=F32) + jnp.dot(xl, wh, preferred_element_type=F32)
              + jnp.dot(xh, wl, preferred_element_type=F32)) + br_ref[...]
    work = logits.T[0:N_EXPERTS, :]
    sub = lax.broadcasted_iota(I32, (N_EXPERTS, tm), 0).astype(F32)
    vals, idxs = [], []
    multihot = jnp.zeros((N_EXPERTS, tm), F32)
    for _ in range(TOP_K):
        mx = jnp.max(work, axis=0, keepdims=True)
        ix = jnp.min(jnp.where(work == mx, sub, float(N_EXPERTS)), axis=0, keepdims=True)
        sel = sub == ix
        work = jnp.where(sel, -jnp.inf, work)
        multihot = multihot + jnp.where(sel, 1.0, 0.0)
        vals.append(mx)
        idxs.append(ix)
    exps = [jnp.exp(v - vals[0]) for v in vals]
    denom = exps[0] + exps[1] + exps[2] + exps[3]
    r = lax.broadcasted_iota(I32, (tm, tm), 0)
    c = lax.broadcasted_iota(I32, (tm, tm), 1)
    earlier = jnp.where(r < c, 1.0, 0.0).astype(BF16)
    before = jnp.dot(multihot.astype(BF16), earlier, preferred_element_type=F32)
    n = jnp.sum(multihot, axis=1, keepdims=True)
    n_pad = jnp.floor((n + float(SUBLANES - 1)) * (1.0 / SUBLANES)) * float(SUBLANES)
    n_pad_b = jnp.broadcast_to(n_pad, (N_EXPERTS, LANES))
    er = lax.broadcasted_iota(I32, (N_EXPERTS, N_EXPERTS), 0)
    ec = lax.broadcasted_iota(I32, (N_EXPERTS, N_EXPERTS), 1)
    below = jnp.where(ec < er, 1.0, 0.0).astype(BF16)
    run_start = jnp.dot(below, n_pad_b.astype(BF16), preferred_element_type=F32)[:, 0:1]
    where_in_tile = before + run_start
    slots = [jnp.sum(jnp.where(sub == idxs[kk], where_in_tile, 0.0), axis=0, keepdims=True) for kk in range(TOP_K)]
    gates = [exps[kk] / denom for kk in range(TOP_K)]
    sub8 = lax.broadcasted_iota(I32, (SUBLANES, tm), 0)
    slot8 = jnp.zeros((SUBLANES, tm), F32)
    gate8 = jnp.zeros((SUBLANES, tm), F32)
    for kk in range(TOP_K):
        slot8 = jnp.where(sub8 == kk, slots[kk], slot8)
        gate8 = jnp.where(sub8 == kk, gates[kk], gate8)
    slott_ref[0] = slot8.astype(I32)
    pad_rows = jnp.zeros((LANES - SUBLANES, tm), F32)
    slot_ref[...] = jnp.concatenate([slot8, pad_rows], axis=0).T.astype(I32)
    gate_ref[...] = jnp.concatenate([gate8, pad_rows], axis=0).T
    cnt_ref[0] = n_pad_b.astype(I32)


def _route(x1, wr, br):
    N = x1.shape[0]
    nt = N // TM_ROW
    row = pl.BlockSpec((TM_ROW, LANES), lambda i: (i, 0))
    return pl.pallas_call(
        _route_kernel,
        grid=(nt,),
        in_specs=[pl.BlockSpec((TM_ROW, D_MODEL), lambda i: (i, 0)),
                  pl.BlockSpec((D_MODEL, LANES), lambda i: (0, 0)),
                  pl.BlockSpec((1, LANES), lambda i: (0, 0))],
        out_specs=[row, row,
                   pl.BlockSpec((1, SUBLANES, TM_ROW), lambda i: (i, 0, 0)),
                   pl.BlockSpec((1, N_EXPERTS, LANES), lambda i: (i, 0, 0))],
        out_shape=[jax.ShapeDtypeStruct((N, LANES), F32), jax.ShapeDtypeStruct((N, LANES), I32),
                   jax.ShapeDtypeStruct((nt, SUBLANES, TM_ROW), I32),
                   jax.ShapeDtypeStruct((nt, N_EXPERTS, LANES), I32)],
        compiler_params=_cparams(("parallel",)),
        name="router_top4",
    )(x1, wr, br)


def _issue_groups(tab_ref, make_copy):
    def issue_one(u, _):
        local = pl.multiple_of(u * SUBLANES, SUBLANES)
        remote = pl.multiple_of(tab_ref[0, 0, 1 + u], SUBLANES)
        make_copy(local, remote, SUBLANES).start()
        return 0

    lax.fori_loop(0, tab_ref[0, 0, 0], issue_one, 0)


def _wait_groups(tab_ref, make_copy):
    n_groups = tab_ref[0, 0, 0]

    @pl.when(n_groups > 0)
    def _():
        make_copy(0, 0, n_groups * SUBLANES).wait()


def _dispatch_kernel(tab_ref, tab_prev_ref, x_ref, slott_ref, xs_hbm, srt_scr, sem):
    tm = x_ref.shape[0]
    i = pl.program_id(0)
    last = pl.num_programs(0) - 1
    cur = lax.rem(i, 2)
    r = lax.broadcasted_iota(I32, (ROWS_TILE, tm), 0)
    hit = r == slott_ref[0, 0:1, :]
    for kk in range(1, TOP_K):
        hit = hit | (r == slott_ref[0, kk:kk + 1, :])
    sel = jnp.where(hit, 1.0, 0.0).astype(BF16)
    srt = jnp.dot(sel, x_ref[...].astype(BF16), preferred_element_type=F32)
    srt_scr[cur] = _pack_rows(srt)

    def out_copy(buf):
        return lambda local, remote, rows: pltpu.make_async_copy(
            srt_scr.at[buf, pl.ds(local, rows)], xs_hbm.at[pl.ds(remote, rows)], sem.at[buf])

    _issue_groups(tab_ref, out_copy(cur))

    @pl.when(i > 0)
    def _():
        _wait_groups(tab_prev_ref, out_copy(1 - cur))

    @pl.when(i == last)
    def _():
        _wait_groups(tab_ref, out_copy(cur))
        used = pl.multiple_of(tab_ref[0, 0, TAB_W - 1], SUBLANES)
        free = xs_hbm.shape[0] - used
        n_big = free // BM_EXP
        n_small = (free - n_big * BM_EXP) // SUBLANES
        srt_scr[0, 0:BM_EXP, :] = jnp.zeros((BM_EXP, HALF_D), U32)

        def big(i):
            return pltpu.make_async_copy(
                srt_scr.at[0, pl.ds(0, BM_EXP)],
                xs_hbm.at[pl.ds(pl.multiple_of(used + i * BM_EXP, SUBLANES), BM_EXP)], sem.at[0])

        def small(i):
            return pltpu.make_async_copy(
                srt_scr.at[0, pl.ds(0, SUBLANES)],
                xs_hbm.at[pl.ds(pl.multiple_of(used + n_big * BM_EXP + i * SUBLANES, SUBLANES), SUBLANES)],
                sem.at[0])

        def for_each(make, count, op):
            def body(i, c):
                op(make(i))
                return c
            lax.fori_loop(0, count, body, 0)

        for op in (lambda cp: cp.start(), lambda cp: cp.wait()):
            for_each(big, n_big, op)
            for_each(small, n_small, op)


def _dispatch(tab, x1, slott, m_pad):
    N = x1.shape[0]
    return pl.pallas_call(
        _dispatch_kernel,
        grid=(N // TM_ROW,),
        in_specs=[pl.BlockSpec((1, 1, TAB_W), lambda i: (i, 0, 0), memory_space=pltpu.SMEM),
                  pl.BlockSpec((1, 1, TAB_W), lambda i: (jnp.maximum(i - 1, 0), 0, 0), memory_space=pltpu.SMEM),
                  pl.BlockSpec((TM_ROW, D_MODEL), lambda i: (i, 0)),
                  pl.BlockSpec((1, SUBLANES, TM_ROW), lambda i: (i, 0, 0))],
        out_specs=pl.BlockSpec(memory_space=pl.ANY),
        out_shape=jax.ShapeDtypeStruct((m_pad, HALF_D), U32),
        scratch_shapes=[pltpu.VMEM((2, ROWS_TILE, HALF_D), U32), pltpu.SemaphoreType.DMA((2,))],
        compiler_params=_cparams(("arbitrary",), has_side_effects=True),
        name="moe_dispatch",
    )(tab, tab, x1, slott)


def _expert_kernel(blk_ref, exp_ref, lo_ref, hi_ref, new_ref, xs_ref, wu_ref, bu_ref, wd_ref, bd_ref, perm_ref,
                   o_ref, wu_scr, wd_scr):
    w = pl.program_id(0)
    lo = lo_ref[w]
    hi = hi_ref[w]

    @pl.when(lo > hi)
    def _():
        o_ref[...] = jnp.zeros_like(o_ref)

    @pl.when(new_ref[w] == 1)
    def _():
        for cc in range(2 * D_FF // (2 * LANES)):
            cols = slice(cc * 2 * LANES, (cc + 1) * 2 * LANES)
            wu_scr[:, cols] = jnp.dot(wu_ref[0, 0, :, cols].astype(BF16), perm_ref[...],
                                      preferred_element_type=F32).astype(BF16)
        wd_scr[...] = wd_ref[0, 0].astype(BF16)

    @pl.when(hi > lo)
    def _():
        xh, xl = _unpack_rows(xs_ref[...])
        x = jnp.concatenate([xh.astype(BF16), xl.astype(BF16)], axis=1)
        hu = jnp.dot(x, wu_scr[...], preferred_element_type=F32) + bu_ref[0]
        chunks = []
        for cc in range(D_FF // LANES):
            g = jnp.minimum(hu[:, 2 * cc * LANES:(2 * cc + 1) * LANES], SWIGLU_LIMIT)
            l = jnp.clip(hu[:, (2 * cc + 1) * LANES:(2 * cc + 2) * LANES], -SWIGLU_LIMIT, SWIGLU_LIMIT)
            chunks.append(g * _sigmoid(SWIGLU_ALPHA * g) * (l + 1.0))
        act = jnp.concatenate(chunks, axis=1)
        y = jnp.dot(act.astype(BF16), wd_scr[...], preferred_element_type=F32) + bd_ref[0]
        packed = _pack_rows(y)

        @pl.when(lo == 0)
        def _():
            o_ref[...] = packed

        @pl.when(lo > 0)
        def _():
            row = lax.broadcasted_iota(I32, packed.shape, 0)
            o_ref[...] = jnp.where((row >= lo) & (row < hi), packed, o_ref[...])


def _pair_perm():
    p = np.zeros((2 * LANES, 2 * LANES), np.float32)
    j = np.arange(LANES)
    p[2 * j, j] = 1.0
    p[2 * j + 1, LANES + j] = 1.0
    return jnp.asarray(p, BF16)


def _experts(meta, xs, w_up, bu, w_down, bd, layer, perm):
    M = xs.shape[0]
    W = meta[0].shape[0]
    by_expert = lambda *shape: pl.BlockSpec((1,) + shape, lambda w, blk, ex, lo, hi, new: (ex[w], 0, 0))
    stacked = lambda *shape: pl.BlockSpec((1, 1) + shape, lambda w, blk, ex, lo, hi, new: (layer, ex[w], 0, 0))
    rows = pl.BlockSpec((BM_EXP, HALF_D), lambda w, blk, ex, lo, hi, new: (blk[w], 0))
    return pl.pallas_call(
        _expert_kernel,
        grid_spec=pltpu.PrefetchScalarGridSpec(
            num_scalar_prefetch=5,
            grid=(W,),
            in_specs=[rows, stacked(D_MODEL, 2 * D_FF), by_expert(1, 2 * D_FF),
                      stacked(D_FF, D_MODEL), by_expert(1, D_MODEL),
                      pl.BlockSpec((2 * LANES, 2 * LANES), lambda w, blk, ex, lo, hi, new: (0, 0))],
            out_specs=rows,
            scratch_shapes=[pltpu.VMEM((D_MODEL, 2 * D_FF), BF16), pltpu.VMEM((D_FF, D_MODEL), BF16)],
        ),
        out_shape=jax.ShapeDtypeStruct((M, HALF_D), U32),
        compiler_params=_cparams(("arbitrary",)),
        name="moe_experts",
    )(*meta, xs, w_up, bu, w_down, bd, perm)


def _combine_kernel(tab_ref, tab_next_ref, x_ref, gate_ref, slot_ref, g_ref, b_ref, yb_hbm, o_ref, buf, sem):
    tm = x_ref.shape[0]
    i = pl.program_id(0)
    cur = lax.rem(i, 2)

    def in_copy(b):
        return lambda local, remote, rows: pltpu.make_async_copy(
            yb_hbm.at[pl.ds(remote, rows)], buf.at[b, pl.ds(local, rows)], sem.at[b])

    @pl.when(i == 0)
    def _():
        buf[...] = jnp.zeros_like(buf)
        _issue_groups(tab_ref, in_copy(cur))

    @pl.when(i + 1 < pl.num_programs(0))
    def _():
        _issue_groups(tab_next_ref, in_copy(1 - cur))

    r = lax.broadcasted_iota(I32, (tm, ROWS_TILE), 1)
    slots = slot_ref[...]
    gates = gate_ref[...]
    pw = jnp.zeros((tm, ROWS_TILE), F32)
    for kk in range(TOP_K):
        pw = jnp.where(r == slots[:, kk:kk + 1], gates[:, kk:kk + 1], pw)
    pw_hi = pw.astype(BF16)
    pw_lo = (pw - pw_hi.astype(F32)).astype(BF16)
    _wait_groups(tab_ref, in_copy(cur))
    yh, yl = _unpack_rows(buf[cur])
    ys = jnp.concatenate([yh.astype(BF16), yl.astype(BF16)], axis=1)
    y = jnp.dot(pw_hi, ys, preferred_element_type=F32) + jnp.dot(pw_lo, ys, preferred_element_type=F32)
    o_ref[...] = _layer_norm(DEEPNORM_ALPHA * x_ref[...] + y, g_ref[...], b_ref[...])


def _combine(tab, x1, gate, slot, g, b, yb):
    N = x1.shape[0]
    nt = N // TM_ROW
    return pl.pallas_call(
        _combine_kernel,
        grid=(N // TM_ROW,),
        in_specs=[pl.BlockSpec((1, 1, TAB_W), lambda i: (i, 0, 0), memory_space=pltpu.SMEM),
                  pl.BlockSpec((1, 1, TAB_W), lambda i: (jnp.minimum(i + 1, nt - 1), 0, 0), memory_space=pltpu.SMEM),
                  pl.BlockSpec((TM_ROW, D_MODEL), lambda i: (i, 0)),
                  pl.BlockSpec((TM_ROW, LANES), lambda i: (i, 0)),
                  pl.BlockSpec((TM_ROW, LANES), lambda i: (i, 0)),
                  pl.BlockSpec((1, D_MODEL), lambda i: (0, 0)),
                  pl.BlockSpec((1, D_MODEL), lambda i: (0, 0)),
                  pl.BlockSpec(memory_space=pl.ANY)],
        out_specs=pl.BlockSpec((TM_ROW, D_MODEL), lambda i: (i, 0)),
        out_shape=jax.ShapeDtypeStruct((N, D_MODEL), F32),
        scratch_shapes=[pltpu.VMEM((2, ROWS_TILE, HALF_D), U32), pltpu.SemaphoreType.DMA((2,))],
        compiler_params=_cparams(("arbitrary",)),
        name="moe_combine_ln2",
    )(tab, tab, x1, gate, slot, g, b, yb)


def _moe_tables(cnt, m_pad):
    n_pad = cnt[:, :, 0]
    meta, region = _work_items(n_pad.sum(axis=0), m_pad)
    run_end = jnp.cumsum(n_pad, axis=1)
    run_start = run_end - n_pad
    hbm_start = region[None, :] + jnp.cumsum(n_pad, axis=0) - n_pad
    first_row = SUBLANES * jnp.arange(GROUPS_TILE, dtype=I32)
    owner = jnp.minimum((run_end[:, None, :] <= first_row[None, :, None]).sum(axis=-1), N_EXPERTS - 1)
    sel = owner[:, :, None] == jnp.arange(N_EXPERTS, dtype=I32)[None, None, :]
    shift = jnp.sum(jnp.where(sel, (hbm_start - run_start)[:, None, :], 0), axis=-1)
    n_groups = run_end[:, -1:] // SUBLANES
    pad = jnp.zeros((n_pad.shape[0], TAB_W - 2 - GROUPS_TILE), I32)
    used = jnp.broadcast_to(n_pad.sum(), n_groups.shape)
    tab = jnp.concatenate([n_groups, shift + first_row[None, :], pad, used], axis=1).astype(I32)
    return tab[:, None, :], meta


def _work_items(counts, M):
    nblk = M // BM_EXP
    W = nblk + N_EXPERTS
    ends = jnp.cumsum(counts)
    starts = ends - counts
    first = starts // BM_EXP
    last = jnp.maximum(ends - 1, 0) // BM_EXP
    n_items = jnp.where(counts > 0, last - first + 1, 0)
    item_end = jnp.cumsum(n_items)
    item_start = item_end - n_items
    total = item_end[-1]
    w = jnp.arange(W, dtype=I32)
    w_eff = jnp.minimum(w, total - 1)
    owner = (item_end[None, :] <= w_eff[:, None]).sum(axis=1).astype(I32)
    sel = owner[:, None] == jnp.arange(N_EXPERTS, dtype=I32)[None, :]
    pick = lambda table: jnp.sum(jnp.where(sel, table[None, :], 0), axis=1)
    blk = pick(first) + (w_eff - pick(item_start))
    lo = jnp.clip(pick(starts) - blk * BM_EXP, 0, BM_EXP)
    hi = jnp.clip(pick(ends) - blk * BM_EXP, 0, BM_EXP)
    valid = w < total
    used_blocks = (ends[-1] + BM_EXP - 1) // BM_EXP
    spare_blk = used_blocks + (w - total)
    fill = (~valid) & (spare_blk < nblk)
    blk = jnp.where(valid, blk, jnp.minimum(spare_blk, nblk - 1))
    lo = jnp.where(valid, lo, jnp.where(fill, 1, 0))
    hi = jnp.where(valid, hi, 0)
    new = valid & (w == pick(item_start))
    return (blk.astype(I32), owner, lo.astype(I32), hi.astype(I32), new.astype(I32)), starts


def _block_diag(blocks):
    G, a, b = blocks.shape
    eye = jnp.eye(G, dtype=blocks.dtype)
    return jnp.einsum("gab,gh->gahb", blocks, eye).reshape(G * a, G * b)


def kernel(x, w_in, b_forget, w_pool, pool_scale, ssm_lambda_re, ssm_lambda_im, ssm_log_dt, ssm_b_re, ssm_b_im, ssm_c_re, ssm_c_im, ssm_d, w_glu, b_glu, w_branch_a, w_branch_b, w_branch_c, w_out, ln1_g, ln1_b, w_router, b_router, w_up, b_up, w_down, b_down, ln2_g, ln2_b):
    B, S, D = x.shape
    assert D == D_MODEL and S % T_ATT == 0 and S % TM_PROJ == 0
    N = B * S
    m_pad = (N // TM_ROW) * ROWS_TILE
    assert m_pad % BM_EXP == 0
    perm = _pair_perm()
    x2 = x.reshape(N, D)
    scale = ATT_HEAD_DIM ** -0.5
    for l in range(DEPTH):
        wl = w_in[l]
        wq = wl[:, :ATT_WIDTH] * scale
        wqkv = jnp.concatenate([wq, wl[:, ATT_WIDTH:QKV_W]], axis=1).astype(BF16)
        c0 = QKV_W
        wf = wl[:, c0:c0 + ATT_HEADS]
        wsmall = jnp.concatenate(
            [wl[:, c0 + ATT_HEADS:c0 + ATT_HEADS + POOL_WIDTH + SSM_WIDTH], wf,
             jnp.zeros((D, F_PAD - ATT_HEADS), F32)], axis=1).astype(BF16)
        wgates = wl[:, c0 + ATT_HEADS + POOL_WIDTH + SSM_WIDTH:].astype(BF16)

        q, k, v, u_pool, u_ssm_tm, f_pad = _proj(x2, wqkv, wsmall, B, S)

        f_rows = f_pad[:, :ATT_HEADS].reshape(B, S, ATT_HEADS).transpose(0, 2, 1).reshape(B * ATT_HEADS, S)
        b_rows = jnp.tile(b_forget[l], B).reshape(B * ATT_HEADS, 1)
        c = _fcum(f_rows, b_rows).reshape(B, ATT_HEADS, 1, S)
        y_a = _attention(q, k, v, c, B, S).reshape(N, ATT_WIDTH)

        w_pool_bd = _block_diag(w_pool[l]).astype(BF16)
        y_b = _pool(u_pool.reshape(B, S, POOL_WIDTH), w_pool_bd, pool_scale[l].reshape(1, POOL_WIDTH)).reshape(N, POOL_WIDTH)

        ar, ai, bbrT, bbiT = _ssm_prep(ssm_lambda_re[l], ssm_lambda_im[l], ssm_log_dt[l], ssm_b_re[l], ssm_b_im[l])
        bblk = jnp.concatenate([_block_diag(bbrT), _block_diag(bbiT)], axis=1).astype(BF16)
        cblk = jnp.concatenate([_block_diag(ssm_c_re[l].transpose(0, 2, 1)),
                                -_block_diag(ssm_c_im[l].transpose(0, 2, 1))], axis=0).astype(BF16)
        ar_b = jnp.broadcast_to(ar.reshape(1, SSM_STATES), (B, SSM_STATES))
        ai_b = jnp.broadcast_to(ai.reshape(1, SSM_STATES), (B, SSM_STATES))
        y_c_tm = _ssm(u_ssm_tm.reshape(S * B, SSM_WIDTH), bblk, ar_b, ai_b, cblk,
                      ssm_d[l].reshape(1, SSM_WIDTH), w_glu[l].astype(BF16), b_glu[l].reshape(1, SSM_WIDTH), B, S)

        x1 = _merge(x2, y_a, y_b, y_c_tm.reshape(S, B * SSM_WIDTH), wgates,
                    w_branch_a[l].astype(BF16), w_branch_b[l].astype(BF16), w_branch_c[l].astype(BF16),
                    w_out[l].astype(BF16), ln1_g[l].reshape(1, D), ln1_b[l].reshape(1, D), B, S)

        wr = jnp.concatenate([w_router[l], jnp.zeros((D, LANES - N_EXPERTS), F32)], axis=1)
        br = jnp.concatenate([b_router[l], jnp.full((LANES - N_EXPERTS,), _NEG_BIG, F32)]).reshape(1, LANES)
        gate, slot, slott, cnt = _route(x1, wr, br)
        tab, meta = _moe_tables(cnt, m_pad)
        xs = _dispatch(tab, x1, slott, m_pad)
        bu = b_up[l].reshape(N_EXPERTS, D_FF // LANES, LANES, 2).transpose(0, 1, 3, 2).reshape(N_EXPERTS, 1, 2 * D_FF)
        yb = _experts(meta, xs, w_up, bu, w_down, b_down[l][:, None, :], l, perm)
        x2 = _combine(tab, x1, gate, slot, ln2_g[l].reshape(1, D), ln2_b[l].reshape(1, D), yb)
    return x2.reshape(B, S, D)
```

```python
import math

import jax
import jax.numpy as jnp
import numpy as np
from jax import lax
from jax.experimental import pallas as pl
from jax.experimental.pallas import tpu as pltpu

F32 = jnp.float32
BF16 = jnp.bfloat16
I32 = jnp.int32
U32 = jnp.uint32

D_MODEL = 1024
DEPTH = 4
ATT_HEADS = 8
ATT_HEAD_DIM = 64
ATT_WIDTH = ATT_HEADS * ATT_HEAD_DIM
POOL_WINDOWS = (2, 4, 8, 16)
POOL_GROUPS = 4
POOL_WIDTH = 256
POOL_GROUP_DIM = 64
MAX_WINDOW = max(POOL_WINDOWS)
SSM_WIDTH = 256
SSM_GROUP_DIM = 16
SSM_GROUPS = 16
SSM_STATE = 64
SSM_STATES = SSM_GROUPS * SSM_STATE
N_BRANCH = 3
N_EXPERTS = 32
TOP_K = 4
D_FF = D_MODEL
SWIGLU_LIMIT = 7.0
SWIGLU_ALPHA = 1.702
LN_EPS = 1e-5
DEEPNORM_ALPHA = (2.0 * DEPTH) ** 0.25
GELU_C = math.sqrt(2.0 / math.pi)

LANES = 128
HALF_D = D_MODEL // 2
F_PAD = LANES
V_PAD = LANES
SMALL_W = POOL_WIDTH + SSM_WIDTH + F_PAD
QKV_W = 3 * ATT_WIDTH
TM_PROJ = 512
T_ATT = 1024
TM_POOL = 256
TT_SSM = 128
TM_MERGE = 512
SUBLANES = 8
TM_ROW = 256
ROWS_TILE = TM_ROW * TOP_K + N_EXPERTS * SUBLANES
TAB_W = LANES
assert TAB_W >= 3 * N_EXPERTS + 2
BM_EXP = 512
VMEM_LIMIT = 52 * 1024 * 1024

_NEG_BIG = -1e30


def _cparams(sem, **kw):
    return pltpu.CompilerParams(dimension_semantics=sem, vmem_limit_bytes=VMEM_LIMIT, **kw)


def _sigmoid(x):
    return 1.0 / (1.0 + jnp.exp(-x))


def _layer_norm(z, g, b):
    mu = jnp.mean(z, axis=-1, keepdims=True)
    zc = z - mu
    var = jnp.mean(zc * zc, axis=-1, keepdims=True)
    return zc * lax.rsqrt(var + LN_EPS) * g + b


def _pack_rows(y):
    u = pltpu.bitcast(y.astype(BF16).astype(F32), U32)
    return u[:, :HALF_D] | (u[:, HALF_D:] >> 16)


def _unpack_rows(p):
    hi = pltpu.bitcast(p & jnp.uint32(0xFFFF0000), F32)
    lo = pltpu.bitcast(p << 16, F32)
    return hi, lo


def _proj_kernel(x_ref, wqkv_ref, ws_ref, q_ref, k_ref, v_ref, up_ref, us_ref, f_ref):
    xb = x_ref[...].astype(BF16)
    for part, ref in enumerate((q_ref, k_ref)):
        h = jnp.dot(xb, wqkv_ref[:, part * ATT_WIDTH:(part + 1) * ATT_WIDTH], preferred_element_type=F32)
        for hh in range(ATT_HEADS):
            ref[0, hh] = h[:, hh * ATT_HEAD_DIM:(hh + 1) * ATT_HEAD_DIM].astype(BF16)
    h = jnp.dot(xb, wqkv_ref[:, 2 * ATT_WIDTH:], preferred_element_type=F32)
    lane = lax.broadcasted_iota(I32, (h.shape[0], V_PAD), 1)
    tail = jnp.where(lane == ATT_HEAD_DIM, 1.0, 0.0)
    for pair in range(ATT_HEADS // 2):
        slab = h[:, pair * V_PAD:(pair + 1) * V_PAD]
        v_ref[0, 2 * pair] = jnp.where(lane < ATT_HEAD_DIM, slab, tail).astype(BF16)
        v_ref[0, 2 * pair + 1] = jnp.where(lane < ATT_HEAD_DIM, pltpu.roll(slab, ATT_HEAD_DIM, 1), tail).astype(BF16)
    hs = jnp.dot(xb, ws_ref[...], preferred_element_type=F32)
    up_ref[...] = hs[:, :POOL_WIDTH]
    us_ref[...] = hs[:, POOL_WIDTH:POOL_WIDTH + SSM_WIDTH]
    f_ref[...] = hs[:, POOL_WIDTH + SSM_WIDTH:]


def _proj(x2, wqkv, wsmall, B, S):
    N = B * S
    nS = S // TM_PROJ
    hm = jax.ShapeDtypeStruct((B, ATT_HEADS, S, ATT_HEAD_DIM), BF16)
    hm_spec = pl.BlockSpec((1, ATT_HEADS, TM_PROJ, ATT_HEAD_DIM), lambda b, s: (b, 0, s, 0))
    hv = jax.ShapeDtypeStruct((B, ATT_HEADS, S, V_PAD), BF16)
    hv_spec = pl.BlockSpec((1, ATT_HEADS, TM_PROJ, V_PAD), lambda b, s: (b, 0, s, 0))
    return pl.pallas_call(
        _proj_kernel,
        grid=(B, nS),
        in_specs=[
            pl.BlockSpec((TM_PROJ, D_MODEL), lambda b, s: (b * nS + s, 0)),
            pl.BlockSpec((D_MODEL, QKV_W), lambda b, s: (0, 0)),
            pl.BlockSpec((D_MODEL, SMALL_W), lambda b, s: (0, 0)),
        ],
        out_specs=[
            hm_spec, hm_spec, hv_spec,
            pl.BlockSpec((TM_PROJ, POOL_WIDTH), lambda b, s: (b * nS + s, 0)),
            pl.BlockSpec((TM_PROJ, SSM_WIDTH), lambda b, s: (s, b)),
            pl.BlockSpec((TM_PROJ, F_PAD), lambda b, s: (b * nS + s, 0)),
        ],
        out_shape=[
            hm, hm, hv,
            jax.ShapeDtypeStruct((N, POOL_WIDTH), F32),
            jax.ShapeDtypeStruct((S, B * SSM_WIDTH), F32),
            jax.ShapeDtypeStruct((N, F_PAD), F32),
        ],
        compiler_params=_cparams(("parallel", "parallel")),
        name="in_proj",
    )(x2, wqkv, wsmall)


def _fcum_kernel(f_ref, b_ref, c_ref):
    rows, S = f_ref.shape
    lane = lax.broadcasted_iota(I32, (rows, LANES), 1)
    carry = jnp.zeros((rows, 1), F32)
    for ch in range(S // LANES):
        z = f_ref[:, ch * LANES:(ch + 1) * LANES] + b_ref[...]
        lf = jnp.minimum(z, 0.0) - jnp.log1p(jnp.exp(-jnp.abs(z)))
        sh = 1
        while sh < LANES:
            lf = lf + jnp.where(lane >= sh, pltpu.roll(lf, sh, 1), 0.0)
            sh *= 2
        lf = lf + carry
        c_ref[:, ch * LANES:(ch + 1) * LANES] = lf
        carry = lf[:, LANES - 1:LANES]


def _fcum(f_rows, b_rows):
    rows, S = f_rows.shape
    return pl.pallas_call(
        _fcum_kernel,
        out_shape=jax.ShapeDtypeStruct((rows, S), F32),
        compiler_params=pltpu.CompilerParams(vmem_limit_bytes=VMEM_LIMIT),
        name="forget_cumsum",
    )(f_rows, b_rows)


def _attn_kernel(q_ref, k_ref, v_ref, c_ref, o_ref):
    T = T_ATT
    H = T // 2
    qi = pl.program_id(2)
    qs = [q_ref[0, hh] for hh in range(2)]

    def update(carry, q_pair, key_start, n_keys, diag):
        out = []
        for hh in range(2):
            m, acc = carry[2 * hh], carry[2 * hh + 1]
            k = k_ref[0, hh, pl.ds(key_start, n_keys), :]
            v = v_ref[0, hh, pl.ds(key_start, n_keys), :]
            s = lax.dot_general(q_pair[hh], k, (((1,), (1,)), ((), ())), preferred_element_type=F32)
            s = s - c_ref[0, hh, :, pl.ds(key_start, n_keys)]
            if diag is not None:
                delta = lax.broadcasted_iota(I32, s.shape, 1) - lax.broadcasted_iota(I32, s.shape, 0)
                s = jnp.where(delta <= diag, s, -jnp.inf)
            m_new = jnp.maximum(m, jnp.max(s, axis=-1, keepdims=True))
            p = jnp.exp(s - m_new)
            acc = jnp.exp(m - m_new) * acc + jnp.dot(p.astype(BF16), v, preferred_element_type=F32)
            out += [m_new, acc]
        return tuple(out)

    m0 = jnp.full((T, 1), -jnp.inf, F32)
    acc0 = jnp.zeros((T, V_PAD), F32)
    carry = lax.fori_loop(
        0, qi, lambda j, c: update(c, qs, pl.multiple_of(j * T, T), T, None), (m0, acc0, m0, acc0))
    diag_start = pl.multiple_of(qi * T, T)
    halves = []
    for half in range(2):
        rows = slice(half * H, (half + 1) * H)
        halves.append(update(tuple(x[rows] for x in carry), [q[rows] for q in qs],
                             diag_start, (half + 1) * H, half * H))
    for hh in range(2):
        acc = jnp.concatenate([halves[0][2 * hh + 1], halves[1][2 * hh + 1]], axis=0)
        out = acc[:, :ATT_HEAD_DIM] / acc[:, ATT_HEAD_DIM:ATT_HEAD_DIM + 1]
        o_ref[0, :, hh * ATT_HEAD_DIM:(hh + 1) * ATT_HEAD_DIM] = out.astype(BF16)


def _attention(q, k, v, c, B, S):
    nq = S // T_ATT
    qspec = pl.BlockSpec((1, 2, T_ATT, ATT_HEAD_DIM), lambda b, hp, i: (b, hp, i, 0))
    kspec = pl.BlockSpec((1, 2, S, ATT_HEAD_DIM), lambda b, hp, i: (b, hp, 0, 0))
    vspec = pl.BlockSpec((1, 2, S, V_PAD), lambda b, hp, i: (b, hp, 0, 0))
    return pl.pallas_call(
        _attn_kernel,
        grid=(B, ATT_HEADS // 2, nq),
        in_specs=[qspec, kspec, vspec,
                  pl.BlockSpec((1, 2, 1, S), lambda b, hp, i: (b, hp, 0, 0))],
        out_specs=pl.BlockSpec((1, T_ATT, 2 * ATT_HEAD_DIM), lambda b, hp, i: (b, i, hp)),
        out_shape=jax.ShapeDtypeStruct((B, S, ATT_WIDTH), BF16),
        compiler_params=_cparams(("parallel", "parallel", "arbitrary")),
        name="fox_attention",
    )(q, k, v, c)


def _pool_kernel(u_ref, w_ref, sc_ref, o_ref, pad_ref):
    S = u_ref.shape[1]
    R = TM_POOL
    pad_ref[0:MAX_WINDOW, :] = jnp.zeros((MAX_WINDOW, POOL_WIDTH), F32)
    pad_ref[MAX_WINDOW:, :] = u_ref[0]
    lane = lax.broadcasted_iota(I32, (R, POOL_WIDTH), 1)
    trow = lax.broadcasted_iota(I32, (R, POOL_WIDTH), 0)
    grp = lane // POOL_GROUP_DIM
    win = jnp.where(grp == 0, POOL_WINDOWS[0],
                    jnp.where(grp == 1, POOL_WINDOWS[1], jnp.where(grp == 2, POOL_WINDOWS[2], POOL_WINDOWS[3])))
    for i in range(S // R):
        base = MAX_WINDOW + i * R
        u0 = pad_ref[base:base + R, :]
        acc = u0
        sums = {}
        for kk in range(1, MAX_WINDOW):
            acc = acc + pad_ref[base - kk:base - kk + R, :]
            if kk + 1 in POOL_WINDOWS:
                sums[kk + 1] = acc
        total = jnp.where(grp == 0, sums[POOL_WINDOWS[0]],
                          jnp.where(grp == 1, sums[POOL_WINDOWS[1]],
                                    jnp.where(grp == 2, sums[POOL_WINDOWS[2]], sums[POOL_WINDOWS[3]])))
        cnt = jnp.minimum(trow + (i * R + 1), win).astype(F32)
        mixed = total / cnt - u0
        y = jnp.dot(mixed.astype(BF16), w_ref[...], preferred_element_type=F32) * sc_ref[...]
        o_ref[0, i * R:(i + 1) * R, :] = y.astype(BF16)


def _pool(u3, w_bd, scale):
    B, S, _ = u3.shape
    return pl.pallas_call(
        _pool_kernel,
        grid=(B,),
        in_specs=[pl.BlockSpec((1, S, POOL_WIDTH), lambda b: (b, 0, 0)),
                  pl.BlockSpec((POOL_WIDTH, POOL_WIDTH), lambda b: (0, 0)),
                  pl.BlockSpec((1, POOL_WIDTH), lambda b: (0, 0))],
        out_specs=pl.BlockSpec((1, S, POOL_WIDTH), lambda b: (b, 0, 0)),
        out_shape=jax.ShapeDtypeStruct((B, S, POOL_WIDTH), BF16),
        scratch_shapes=[pltpu.VMEM((S + MAX_WINDOW, POOL_WIDTH), F32)],
        compiler_params=_cparams(("parallel",)),
        name="multiscale_pool",
    )(u3, w_bd, scale)


def _ssm_prep_kernel(lr_ref, li_ref, ldt_ref, brT_ref, biT_ref, ar_ref, ai_ref, bbr_ref, bbi_ref):
    lr = lr_ref[...]
    li = li_ref[...]
    dt = jnp.exp(ldt_ref[...])
    mag = jnp.exp(lr * dt)
    ar = mag * jnp.cos(li * dt)
    ai = mag * jnp.sin(li * dt)
    den = lr * lr + li * li
    nr = ar - 1.0
    zr = (nr * lr + ai * li) / den
    zi = (ai * lr - nr * li) / den
    ar_ref[...] = ar
    ai_ref[...] = ai
    br = brT_ref[...]
    bi = biT_ref[...]
    bbr_ref[...] = zr[:, None, :] * br - zi[:, None, :] * bi
    bbi_ref[...] = zr[:, None, :] * bi + zi[:, None, :] * br


def _ssm_prep(lr, li, log_dt, b_re, b_im):
    G, P, H = b_re.shape
    gp = jax.ShapeDtypeStruct((G, P), F32)
    ghp = jax.ShapeDtypeStruct((G, H, P), F32)
    return pl.pallas_call(
        _ssm_prep_kernel,
        out_shape=[gp, gp, ghp, ghp],
        name="s5_discretise",
    )(lr, li, log_dt.reshape(G, 1), b_re.transpose(0, 2, 1), b_im.transpose(0, 2, 1))


def _gelu_tanh(y):
    return 0.5 * y * (1.0 + jnp.tanh(GELU_C * (y + 0.044715 * (y * y * y))))


def _ssm_kernel(u_ref, bb_ref, ar_ref, ai_ref, cc_ref, d_ref, wg_ref, bg_ref, o_ref, x_scr, hr_scr, hi_scr):
    nb = hr_scr.shape[0]
    tt = u_ref.shape[0] // nb

    @pl.when(pl.program_id(0) == 0)
    def _():
        hr_scr[...] = jnp.zeros_like(hr_scr)
        hi_scr[...] = jnp.zeros_like(hi_scr)

    u = u_ref[...]
    x_scr[...] = jnp.dot(u.astype(BF16), bb_ref[...], preferred_element_type=F32)

    def step(t, carry):
        hr, hi = carry
        r0 = pl.multiple_of(t * nb, nb)
        xr = x_scr[pl.ds(r0, nb), 0:SSM_STATES]
        xi = x_scr[pl.ds(r0, nb), SSM_STATES:2 * SSM_STATES]
        ar = ar_ref[...]
        ai = ai_ref[...]
        nr = ar * hr - ai * hi + xr
        ni = ar * hi + ai * hr + xi
        x_scr[pl.ds(r0, nb), 0:SSM_STATES] = nr
        x_scr[pl.ds(r0, nb), SSM_STATES:2 * SSM_STATES] = ni
        return nr, ni

    hr, hi = lax.fori_loop(0, tt, step, (hr_scr[...], hi_scr[...]))
    hr_scr[...] = hr
    hi_scr[...] = hi
    y = jnp.dot(x_scr[...].astype(BF16), cc_ref[...], preferred_element_type=F32) + d_ref[...] * u
    y = _gelu_tanh(y)
    gl = jnp.dot(y.astype(BF16), wg_ref[...], preferred_element_type=F32) + bg_ref[...]
    o_ref[...] = (y * _sigmoid(gl)).astype(BF16)


def _ssm(u_tm, bblk, ar_b, ai_b, cblk, dvec, wglu, bglu, B, S):
    rows = TT_SSM * B
    const = lambda shape: pl.BlockSpec(shape, lambda i: (0, 0))
    return pl.pallas_call(
        _ssm_kernel,
        grid=(S // TT_SSM,),
        in_specs=[pl.BlockSpec((rows, SSM_WIDTH), lambda i: (i, 0)),
                  const((SSM_WIDTH, 2 * SSM_STATES)),
                  const((B, SSM_STATES)), const((B, SSM_STATES)),
                  const((2 * SSM_STATES, SSM_WIDTH)),
                  const((1, SSM_WIDTH)), const((SSM_WIDTH, SSM_WIDTH)), const((1, SSM_WIDTH))],
        out_specs=pl.BlockSpec((rows, SSM_WIDTH), lambda i: (i, 0)),
        out_shape=jax.ShapeDtypeStruct((S * B, SSM_WIDTH), BF16),
        scratch_shapes=[pltpu.VMEM((rows, 2 * SSM_STATES), F32),
                        pltpu.VMEM((B, SSM_STATES), F32), pltpu.VMEM((B, SSM_STATES), F32)],
        compiler_params=_cparams(("arbitrary",)),
        name="s5_scan",
    )(u_tm, bblk, ar_b, ai_b, cblk, dvec, wglu, bglu)


def _merge_kernel(x_ref, ya_ref, yb_ref, yc_ref, wg_ref, wa_ref, wb_ref, wc_ref, wo_ref, g_ref, b_ref, o_ref):
    x = x_ref[...]
    xb = x.astype(BF16)
    merged = None
    for i, (y_ref, w_ref) in enumerate(((ya_ref, wa_ref), (yb_ref, wb_ref), (yc_ref, wc_ref))):
        gate = _sigmoid(jnp.dot(xb, wg_ref[:, i * D_MODEL:(i + 1) * D_MODEL], preferred_element_type=F32))
        term = gate * jnp.dot(y_ref[...], w_ref[...], preferred_element_type=F32)
        merged = term if merged is None else merged + term
    mix = jnp.dot(merged.astype(BF16), wo_ref[...], preferred_element_type=F32)
    o_ref[...] = _layer_norm(DEEPNORM_ALPHA * x + mix, g_ref[...], b_ref[...])


def _merge(x2, ya, yb, yc_tm, wg, wa, wb, wc, wo, g, b, B, S):
    N = B * S
    nS = S // TM_MERGE
    const = lambda shape: pl.BlockSpec(shape, lambda bb, s: (0, 0))
    row = lambda w: pl.BlockSpec((TM_MERGE, w), lambda bb, s: (bb * nS + s, 0))
    return pl.pallas_call(
        _merge_kernel,
        grid=(B, nS),
        in_specs=[row(D_MODEL), row(ATT_WIDTH), row(POOL_WIDTH),
                  pl.BlockSpec((TM_MERGE, SSM_WIDTH), lambda bb, s: (s, bb)),
                  const((D_MODEL, N_BRANCH * D_MODEL)), const((ATT_WIDTH, D_MODEL)),
                  const((POOL_WIDTH, D_MODEL)), const((SSM_WIDTH, D_MODEL)), const((D_MODEL, D_MODEL)),
                  const((1, D_MODEL)), const((1, D_MODEL))],
        out_specs=row(D_MODEL),
        out_shape=jax.ShapeDtypeStruct((N, D_MODEL), F32),
        compiler_params=_cparams(("parallel", "parallel")),
        name="merge_ln1",
    )(x2, ya, yb, yc_tm, wg, wa, wb, wc, wo, g, b)


def _route_kernel(x_ref, wr_ref, br_ref, gate_ref, slot_ref, slott_ref, cnt_ref):
    tm = x_ref.shape[0]
    x = x_ref[...]
    xh = x.astype(BF16)
    xl = (x - xh.astype(F32)).astype(BF16)
    w = wr_ref[...]
    wh = w.astype(BF16)
    wl = (w - wh.astype(F32)).astype(BF16)
    logits = (jnp.dot(xh, wh, preferred_element_type=F32) + jnp.dot(xl, wh, preferred_element_type=F32)
              + jnp.dot(xh, wl, preferred_element_type=F32)) + br_ref[...]
    work = logits.T[0:N_EXPERTS, :]
    sub = lax.broadcasted_iota(I32, (N_EXPERTS, tm), 0).astype(F32)
    vals, idxs = [], []
    multihot = jnp.zeros((N_EXPERTS, tm), F32)
    for _ in range(TOP_K):
        mx = jnp.max(work, axis=0, keepdims=True)
        ix = jnp.min(jnp.where(work == mx, sub, float(N_EXPERTS)), axis=0, keepdims=True)
        sel = sub == ix
        work = jnp.where(sel, -jnp.inf, work)
        multihot = multihot + jnp.where(sel, 1.0, 0.0)
        vals.append(mx)
        idxs.append(ix)
    exps = [jnp.exp(v - vals[0]) for v in vals]
    denom = exps[0] + exps[1] + exps[2] + exps[3]
    r = lax.broadcasted_iota(I32, (tm, tm), 0)
    c = lax.broadcasted_iota(I32, (tm, tm), 1)
    earlier = jnp.where(r < c, 1.0, 0.0).astype(BF16)
    before = jnp.dot(multihot.astype(BF16), earlier, preferred_element_type=F32)
    n = jnp.sum(multihot, axis=1, keepdims=True)
    n_pad = jnp.floor((n + float(SUBLANES - 1)) * (1.0 / SUBLANES)) * float(SUBLANES)
    n_pad_b = jnp.broadcast_to(n_pad, (N_EXPERTS, LANES))
    er = lax.broadcasted_iota(I32, (N_EXPERTS, N_EXPERTS), 0)
    ec = lax.broadcasted_iota(I32, (N_EXPERTS, N_EXPERTS), 1)
    below = jnp.where(ec < er, 1.0, 0.0).astype(BF16)
    run_start = jnp.dot(below, n_pad_b.astype(BF16), preferred_element_type=F32)[:, 0:1]
    where_in_tile = before + run_start
    slots = [jnp.sum(jnp.where(sub == idxs[kk], where_in_tile, 0.0), axis=0, keepdims=True) for kk in range(TOP_K)]
    gates = [exps[kk] / denom for kk in range(TOP_K)]
    sub8 = lax.broadcasted_iota(I32, (SUBLANES, tm), 0)
    slot8 = jnp.zeros((SUBLANES, tm), F32)
    gate8 = jnp.zeros((SUBLANES, tm), F32)
    for kk in range(TOP_K):
        slot8 = jnp.where(sub8 == kk, slots[kk], slot8)
        gate8 = jnp.where(sub8 == kk, gates[kk], gate8)
    slott_ref[0] = slot8.astype(I32)
    pad_rows = jnp.zeros((LANES - SUBLANES, tm), F32)
    slot_ref[...] = jnp.concatenate([slot8, pad_rows], axis=0).T.astype(I32)
    gate_ref[...] = jnp.concatenate([gate8, pad_rows], axis=0).T
    cnt_ref[0] = n_pad_b.astype(I32)


def _route(x1, wr, br):
    N = x1.shape[0]
    nt = N // TM_ROW
    row = pl.BlockSpec((TM_ROW, LANES), lambda i: (i, 0))
    return pl.pallas_call(
        _route_kernel,
        grid=(nt,),
        in_specs=[pl.BlockSpec((TM_ROW, D_MODEL), lambda i: (i, 0)),
                  pl.BlockSpec((D_MODEL, LANES), lambda i: (0, 0)),
                  pl.BlockSpec((1, LANES), lambda i: (0, 0))],
        out_specs=[row, row,
                   pl.BlockSpec((1, SUBLANES, TM_ROW), lambda i: (i, 0, 0)),
                   pl.BlockSpec((1, N_EXPERTS, LANES), lambda i: (i, 0, 0))],
        out_shape=[jax.ShapeDtypeStruct((N, LANES), F32), jax.ShapeDtypeStruct((N, LANES), I32),
                   jax.ShapeDtypeStruct((nt, SUBLANES, TM_ROW), I32),
                   jax.ShapeDtypeStruct((nt, N_EXPERTS, LANES), I32)],
        compiler_params=_cparams(("parallel",)),
        name="router_top4",
    )(x1, wr, br)


def _issue_runs(tab_ref, make_copy):
    def issue_one(e, _):
        rows = tab_ref[0, 0, 2 * N_EXPERTS + e]

        @pl.when(rows > 0)
        def _():
            make_copy(pl.multiple_of(tab_ref[0, 0, e], SUBLANES),
                      pl.multiple_of(tab_ref[0, 0, N_EXPERTS + e], SUBLANES),
                      pl.multiple_of(rows, SUBLANES)).start()
        return 0

    lax.fori_loop(0, N_EXPERTS, issue_one, 0)


def _wait_runs(tab_ref, make_copy):
    total = tab_ref[0, 0, 3 * N_EXPERTS]

    @pl.when(total > 0)
    def _():
        make_copy(0, 0, pl.multiple_of(total, SUBLANES)).wait()


def _dispatch_kernel(tab_ref, tab_prev_ref, x_ref, slott_ref, xs_hbm, srt_scr, sem):
    tm = x_ref.shape[0]
    i = pl.program_id(0)
    last = pl.num_programs(0) - 1
    cur = lax.rem(i, 2)
    r = lax.broadcasted_iota(I32, (ROWS_TILE, tm), 0)
    hit = r == slott_ref[0, 0:1, :]
    for kk in range(1, TOP_K):
        hit = hit | (r == slott_ref[0, kk:kk + 1, :])
    sel = jnp.where(hit, 1.0, 0.0).astype(BF16)
    srt = jnp.dot(sel, x_ref[...].astype(BF16), preferred_element_type=F32)
    srt_scr[cur] = _pack_rows(srt)

    def out_copy(buf):
        return lambda local, remote, rows: pltpu.make_async_copy(
            srt_scr.at[buf, pl.ds(local, rows)], xs_hbm.at[pl.ds(remote, rows)], sem.at[buf])

    _issue_runs(tab_ref, out_copy(cur))

    @pl.when(i > 0)
    def _():
        _wait_runs(tab_prev_ref, out_copy(1 - cur))

    @pl.when(i == last)
    def _():
        _wait_runs(tab_ref, out_copy(cur))
        used = pl.multiple_of(tab_ref[0, 0, TAB_W - 1], SUBLANES)
        free = xs_hbm.shape[0] - used
        n_big = free // BM_EXP
        n_small = (free - n_big * BM_EXP) // SUBLANES
        srt_scr[0, 0:BM_EXP, :] = jnp.zeros((BM_EXP, HALF_D), U32)

        def big(i):
            return pltpu.make_async_copy(
                srt_scr.at[0, pl.ds(0, BM_EXP)],
                xs_hbm.at[pl.ds(pl.multiple_of(used + i * BM_EXP, SUBLANES), BM_EXP)], sem.at[0])

        def small(i):
            return pltpu.make_async_copy(
                srt_scr.at[0, pl.ds(0, SUBLANES)],
                xs_hbm.at[pl.ds(pl.multiple_of(used + n_big * BM_EXP + i * SUBLANES, SUBLANES), SUBLANES)],
                sem.at[0])

        def for_each(make, count, op):
            def body(i, c):
                op(make(i))
                return c
            lax.fori_loop(0, count, body, 0)

        for op in (lambda cp: cp.start(), lambda cp: cp.wait()):
            for_each(big, n_big, op)
            for_each(small, n_small, op)


def _dispatch(tab, x1, slott, m_pad):
    N = x1.shape[0]
    return pl.pallas_call(
        _dispatch_kernel,
        grid=(N // TM_ROW,),
        in_specs=[pl.BlockSpec((1, 1, TAB_W), lambda i: (i, 0, 0), memory_space=pltpu.SMEM),
                  pl.BlockSpec((1, 1, TAB_W), lambda i: (jnp.maximum(i - 1, 0), 0, 0), memory_space=pltpu.SMEM),
                  pl.BlockSpec((TM_ROW, D_MODEL), lambda i: (i, 0)),
                  pl.BlockSpec((1, SUBLANES, TM_ROW), lambda i: (i, 0, 0))],
        out_specs=pl.BlockSpec(memory_space=pl.ANY),
        out_shape=jax.ShapeDtypeStruct((m_pad, HALF_D), U32),
        scratch_shapes=[pltpu.VMEM((2, ROWS_TILE, HALF_D), U32), pltpu.SemaphoreType.DMA((2,))],
        compiler_params=_cparams(("arbitrary",), has_side_effects=True),
        name="moe_dispatch",
    )(tab, tab, x1, slott)


def _expert_kernel(blk_ref, exp_ref, lo_ref, hi_ref, new_ref, xs_ref, wu_ref, bu_ref, wd_ref, bd_ref, perm_ref,
                   o_ref, wu_scr, wd_scr):
    w = pl.program_id(0)
    lo = lo_ref[w]
    hi = hi_ref[w]

    @pl.when(lo > hi)
    def _():
        o_ref[...] = jnp.zeros_like(o_ref)

    @pl.when(new_ref[w] == 1)
    def _():
        for cc in range(2 * D_FF // (2 * LANES)):
            cols = slice(cc * 2 * LANES, (cc + 1) * 2 * LANES)
            wu_scr[:, cols] = jnp.dot(wu_ref[0, 0, :, cols].astype(BF16), perm_ref[...],
                                      preferred_element_type=F32).astype(BF16)
        wd_scr[...] = wd_ref[0, 0].astype(BF16)

    @pl.when(hi > lo)
    def _():
        xh, xl = _unpack_rows(xs_ref[...])
        x = jnp.concatenate([xh.astype(BF16), xl.astype(BF16)], axis=1)
        hu = jnp.dot(x, wu_scr[...], preferred_element_type=F32) + bu_ref[0]
        chunks = []
        for cc in range(D_FF // LANES):
            g = jnp.minimum(hu[:, 2 * cc * LANES:(2 * cc + 1) * LANES], SWIGLU_LIMIT)
            l = jnp.clip(hu[:, (2 * cc + 1) * LANES:(2 * cc + 2) * LANES], -SWIGLU_LIMIT, SWIGLU_LIMIT)
            chunks.append(g * _sigmoid(SWIGLU_ALPHA * g) * (l + 1.0))
        act = jnp.concatenate(chunks, axis=1)
        y = jnp.dot(act.astype(BF16), wd_scr[...], preferred_element_type=F32) + bd_ref[0]
        packed = _pack_rows(y)

        @pl.when(lo == 0)
        def _():
            o_ref[...] = packed

        @pl.when(lo > 0)
        def _():
            row = lax.broadcasted_iota(I32, packed.shape, 0)
            o_ref[...] = jnp.where((row >= lo) & (row < hi), packed, o_ref[...])


def _pair_perm():
    p = np.zeros((2 * LANES, 2 * LANES), np.float32)
    j = np.arange(LANES)
    p[2 * j, j] = 1.0
    p[2 * j + 1, LANES + j] = 1.0
    return jnp.asarray(p, BF16)


def _experts(meta, xs, w_up, bu, w_down, bd, layer, perm):
    M = xs.shape[0]
    W = meta[0].shape[0]
    by_expert = lambda *shape: pl.BlockSpec((1,) + shape, lambda w, blk, ex, lo, hi, new: (ex[w], 0, 0))
    stacked = lambda *shape: pl.BlockSpec((1, 1) + shape, lambda w, blk, ex, lo, hi, new: (layer, ex[w], 0, 0))
    rows = pl.BlockSpec((BM_EXP, HALF_D), lambda w, blk, ex, lo, hi, new: (blk[w], 0))
    return pl.pallas_call(
        _expert_kernel,
        grid_spec=pltpu.PrefetchScalarGridSpec(
            num_scalar_prefetch=5,
            grid=(W,),
            in_specs=[rows, stacked(D_MODEL, 2 * D_FF), by_expert(1, 2 * D_FF),
                      stacked(D_FF, D_MODEL), by_expert(1, D_MODEL),
                      pl.BlockSpec((2 * LANES, 2 * LANES), lambda w, blk, ex, lo, hi, new: (0, 0))],
            out_specs=rows,
            scratch_shapes=[pltpu.VMEM((D_MODEL, 2 * D_FF), BF16), pltpu.VMEM((D_FF, D_MODEL), BF16)],
        ),
        out_shape=jax.ShapeDtypeStruct((M, HALF_D), U32),
        compiler_params=_cparams(("arbitrary",)),
        name="moe_experts",
    )(*meta, xs, w_up, bu, w_down, bd, perm)


def _combine_kernel(tab_ref, tab_next_ref, x_ref, gate_ref, slot_ref, g_ref, b_ref, yb_hbm, o_ref, buf, sem):
    tm = x_ref.shape[0]
    i = pl.program_id(0)
    cur = lax.rem(i, 2)

    def in_copy(b):
        return lambda local, remote, rows: pltpu.make_async_copy(
            yb_hbm.at[pl.ds(remote, rows)], buf.at[b, pl.ds(local, rows)], sem.at[b])

    @pl.when(i == 0)
    def _():
        buf[...] = jnp.zeros_like(buf)
        _issue_runs(tab_ref, in_copy(cur))

    @pl.when(i + 1 < pl.num_programs(0))
    def _():
        _issue_runs(tab_next_ref, in_copy(1 - cur))

    r = lax.broadcasted_iota(I32, (tm, ROWS_TILE), 1)
    slots = slot_ref[...]
    gates = gate_ref[...]
    pw = jnp.zeros((tm, ROWS_TILE), F32)
    for kk in range(TOP_K):
        pw = jnp.where(r == slots[:, kk:kk + 1], gates[:, kk:kk + 1], pw)
    pw = pw.astype(BF16)
    _wait_runs(tab_ref, in_copy(cur))
    yh, yl = _unpack_rows(buf[cur])
    ys = jnp.concatenate([yh.astype(BF16), yl.astype(BF16)], axis=1)
    y = jnp.dot(pw, ys, preferred_element_type=F32)
    o_ref[...] = _layer_norm(DEEPNORM_ALPHA * x_ref[...] + y, g_ref[...], b_ref[...])


def _combine(tab, x1, gate, slot, g, b, yb):
    N = x1.shape[0]
    nt = N // TM_ROW
    return pl.pallas_call(
        _combine_kernel,
        grid=(N // TM_ROW,),
        in_specs=[pl.BlockSpec((1, 1, TAB_W), lambda i: (i, 0, 0), memory_space=pltpu.SMEM),
                  pl.BlockSpec((1, 1, TAB_W), lambda i: (jnp.minimum(i + 1, nt - 1), 0, 0), memory_space=pltpu.SMEM),
                  pl.BlockSpec((TM_ROW, D_MODEL), lambda i: (i, 0)),
                  pl.BlockSpec((TM_ROW, LANES), lambda i: (i, 0)),
                  pl.BlockSpec((TM_ROW, LANES), lambda i: (i, 0)),
                  pl.BlockSpec((1, D_MODEL), lambda i: (0, 0)),
                  pl.BlockSpec((1, D_MODEL), lambda i: (0, 0)),
                  pl.BlockSpec(memory_space=pl.ANY)],
        out_specs=pl.BlockSpec((TM_ROW, D_MODEL), lambda i: (i, 0)),
        out_shape=jax.ShapeDtypeStruct((N, D_MODEL), F32),
        scratch_shapes=[pltpu.VMEM((2, ROWS_TILE, HALF_D), U32), pltpu.SemaphoreType.DMA((2,))],
        compiler_params=_cparams(("arbitrary",)),
        name="moe_combine_ln2",
    )(tab, tab, x1, gate, slot, g, b, yb)


def _moe_tables(cnt, m_pad):
    n_pad = cnt[:, :, 0]
    meta, region = _work_items(n_pad.sum(axis=0), m_pad)
    run_end = jnp.cumsum(n_pad, axis=1)
    run_start = run_end - n_pad
    hbm_start = region[None, :] + jnp.cumsum(n_pad, axis=0) - n_pad
    total = run_end[:, -1:]
    pad = jnp.zeros((n_pad.shape[0], TAB_W - 3 * N_EXPERTS - 2), I32)
    used = jnp.broadcast_to(n_pad.sum(), total.shape)
    tab = jnp.concatenate([run_start, hbm_start, n_pad, total, pad, used], axis=1).astype(I32)
    return tab[:, None, :], meta


def _work_items(counts, M):
    nblk = M // BM_EXP
    W = nblk + N_EXPERTS
    ends = jnp.cumsum(counts)
    starts = ends - counts
    first = starts // BM_EXP
    last = jnp.maximum(ends - 1, 0) // BM_EXP
    n_items = jnp.where(counts > 0, last - first + 1, 0)
    item_end = jnp.cumsum(n_items)
    item_start = item_end - n_items
    total = item_end[-1]
    w = jnp.arange(W, dtype=I32)
    w_eff = jnp.minimum(w, total - 1)
    owner = (item_end[None, :] <= w_eff[:, None]).sum(axis=1).astype(I32)
    sel = owner[:, None] == jnp.arange(N_EXPERTS, dtype=I32)[None, :]
    pick = lambda table: jnp.sum(jnp.where(sel, table[None, :], 0), axis=1)
    blk = pick(first) + (w_eff - pick(item_start))
    lo = jnp.clip(pick(starts) - blk * BM_EXP, 0, BM_EXP)
    hi = jnp.clip(pick(ends) - blk * BM_EXP, 0, BM_EXP)
    valid = w < total
    used_blocks = (ends[-1] + BM_EXP - 1) // BM_EXP
    spare_blk = used_blocks + (w - total)
    fill = (~valid) & (spare_blk < nblk)
    blk = jnp.where(valid, blk, jnp.minimum(spare_blk, nblk - 1))
    lo = jnp.where(valid, lo, jnp.where(fill, 1, 0))
    hi = jnp.where(valid, hi, 0)
    new = valid & (w == pick(item_start))
    return (blk.astype(I32), owner, lo.astype(I32), hi.astype(I32), new.astype(I32)), starts


def _block_diag(blocks):
    G, a, b = blocks.shape
    eye = jnp.eye(G, dtype=blocks.dtype)
    return jnp.einsum("gab,gh->gahb", blocks, eye).reshape(G * a, G * b)


def kernel(x, w_in, b_forget, w_pool, pool_scale, ssm_lambda_re, ssm_lambda_im, ssm_log_dt, ssm_b_re, ssm_b_im, ssm_c_re, ssm_c_im, ssm_d, w_glu, b_glu, w_branch_a, w_branch_b, w_branch_c, w_out, ln1_g, ln1_b, w_router, b_router, w_up, b_up, w_down, b_down, ln2_g, ln2_b):
    B, S, D = x.shape
    assert D == D_MODEL and S % T_ATT == 0 and S % TM_PROJ == 0
    N = B * S
    m_pad = (N // TM_ROW) * ROWS_TILE
    assert m_pad % BM_EXP == 0
    perm = _pair_perm()
    x2 = x.reshape(N, D)
    scale = ATT_HEAD_DIM ** -0.5
    for l in range(DEPTH):
        wl = w_in[l]
        wq = wl[:, :ATT_WIDTH] * scale
        wqkv = jnp.concatenate([wq, wl[:, ATT_WIDTH:QKV_W]], axis=1).astype(BF16)
        c0 = QKV_W
        wf = wl[:, c0:c0 + ATT_HEADS]
        wsmall = jnp.concatenate(
            [wl[:, c0 + ATT_HEADS:c0 + ATT_HEADS + POOL_WIDTH + SSM_WIDTH], wf,
             jnp.zeros((D, F_PAD - ATT_HEADS), F32)], axis=1).astype(BF16)
        wgates = wl[:, c0 + ATT_HEADS + POOL_WIDTH + SSM_WIDTH:].astype(BF16)

        q, k, v, u_pool, u_ssm_tm, f_pad = _proj(x2, wqkv, wsmall, B, S)

        f_rows = f_pad[:, :ATT_HEADS].reshape(B, S, ATT_HEADS).transpose(0, 2, 1).reshape(B * ATT_HEADS, S)
        b_rows = jnp.tile(b_forget[l], B).reshape(B * ATT_HEADS, 1)
        c = _fcum(f_rows, b_rows).reshape(B, ATT_HEADS, 1, S)
        y_a = _attention(q, k, v, c, B, S).reshape(N, ATT_WIDTH)

        w_pool_bd = _block_diag(w_pool[l]).astype(BF16)
        y_b = _pool(u_pool.reshape(B, S, POOL_WIDTH), w_pool_bd, pool_scale[l].reshape(1, POOL_WIDTH)).reshape(N, POOL_WIDTH)

        ar, ai, bbrT, bbiT = _ssm_prep(ssm_lambda_re[l], ssm_lambda_im[l], ssm_log_dt[l], ssm_b_re[l], ssm_b_im[l])
        bblk = jnp.concatenate([_block_diag(bbrT), _block_diag(bbiT)], axis=1).astype(BF16)
        cblk = jnp.concatenate([_block_diag(ssm_c_re[l].transpose(0, 2, 1)),
                                -_block_diag(ssm_c_im[l].transpose(0, 2, 1))], axis=0).astype(BF16)
        ar_b = jnp.broadcast_to(ar.reshape(1, SSM_STATES), (B, SSM_STATES))
        ai_b = jnp.broadcast_to(ai.reshape(1, SSM_STATES), (B, SSM_STATES))
        y_c_tm = _ssm(u_ssm_tm.reshape(S * B, SSM_WIDTH), bblk, ar_b, ai_b, cblk,
                      ssm_d[l].reshape(1, SSM_WIDTH), w_glu[l].astype(BF16), b_glu[l].reshape(1, SSM_WIDTH), B, S)

        x1 = _merge(x2, y_a, y_b, y_c_tm.reshape(S, B * SSM_WIDTH), wgates,
                    w_branch_a[l].astype(BF16), w_branch_b[l].astype(BF16), w_branch_c[l].astype(BF16),
                    w_out[l].astype(BF16), ln1_g[l].reshape(1, D), ln1_b[l].reshape(1, D), B, S)

        wr = jnp.concatenate([w_router[l], jnp.zeros((D, LANES - N_EXPERTS), F32)], axis=1)
        br = jnp.concatenate([b_router[l], jnp.full((LANES - N_EXPERTS,), _NEG_BIG, F32)]).reshape(1, LANES)
        gate, slot, slott, cnt = _route(x1, wr, br)
        tab, meta = _moe_tables(cnt, m_pad)
        xs = _dispatch(tab, x1, slott, m_pad)
        bu = b_up[l].reshape(N_EXPERTS, D_FF // LANES, LANES, 2).transpose(0, 1, 3, 2).reshape(N_EXPERTS, 1, 2 * D_FF)
        yb = _experts(meta, xs, w_up, bu, w_down, b_down[l][:, None, :], l, perm)
        x2 = _combine(tab, x1, gate, slot, ln2_g[l].reshape(1, D), ln2_b[l].reshape(1, D), yb)
    return x2.reshape(B, S, D)
```

```python
import math

import jax
import jax.numpy as jnp
import numpy as np
from jax import lax
from jax.experimental import pallas as pl
from jax.experimental.pallas import tpu as pltpu

F32 = jnp.float32
BF16 = jnp.bfloat16
I32 = jnp.int32
U32 = jnp.uint32

D_MODEL = 1024
DEPTH = 4
ATT_HEADS = 8
ATT_HEAD_DIM = 64
ATT_WIDTH = ATT_HEADS * ATT_HEAD_DIM
POOL_WINDOWS = (2, 4, 8, 16)
POOL_GROUPS = 4
POOL_WIDTH = 256
POOL_GROUP_DIM = 64
MAX_WINDOW = max(POOL_WINDOWS)
SSM_WIDTH = 256
SSM_GROUP_DIM = 16
SSM_GROUPS = 16
SSM_STATE = 64
SSM_STATES = SSM_GROUPS * SSM_STATE
N_BRANCH = 3
N_EXPERTS = 32
TOP_K = 4
D_FF = D_MODEL
SWIGLU_LIMIT = 7.0
SWIGLU_ALPHA = 1.702
LN_EPS = 1e-5
DEEPNORM_ALPHA = (2.0 * DEPTH) ** 0.25
GELU_C = math.sqrt(2.0 / math.pi)

LANES = 128
HALF_D = D_MODEL // 2
F_PAD = LANES
V_PAD = LANES
SMALL_W = POOL_WIDTH + SSM_WIDTH + F_PAD
QKV_W = 3 * ATT_WIDTH
TM_PROJ = 512
T_ATT = 1024
TM_POOL = 256
TT_SSM = 128
SUB_SSM = 16
TM_MERGE = 512
SUBLANES = 8
TM_ROW = 256
ROWS_TILE = TM_ROW * TOP_K + N_EXPERTS * SUBLANES
TAB_W = LANES
assert TAB_W >= 3 * N_EXPERTS + 2
BM_EXP = 512
VMEM_LIMIT = 52 * 1024 * 1024

_NEG_BIG = -1e30


def _cparams(sem, **kw):
    return pltpu.CompilerParams(dimension_semantics=sem, vmem_limit_bytes=VMEM_LIMIT, **kw)


def _sigmoid(x):
    return 1.0 / (1.0 + jnp.exp(-x))


def _layer_norm(z, g, b):
    mu = jnp.mean(z, axis=-1, keepdims=True)
    zc = z - mu
    var = jnp.mean(zc * zc, axis=-1, keepdims=True)
    return zc * lax.rsqrt(var + LN_EPS) * g + b


def _pack_rows(y):
    u = pltpu.bitcast(y.astype(BF16).astype(F32), U32)
    return u[:, :HALF_D] | (u[:, HALF_D:] >> 16)


def _unpack_rows(p):
    hi = pltpu.bitcast(p & jnp.uint32(0xFFFF0000), F32)
    lo = pltpu.bitcast(p << 16, F32)
    return hi, lo


def _proj_kernel(x_ref, wqkv_ref, wkt_ref, ws_ref, q_ref, kt_ref, v_ref, up_ref, us_ref, f_ref):
    xb = x_ref[...].astype(BF16)
    h = jnp.dot(xb, wqkv_ref[:, :ATT_WIDTH], preferred_element_type=F32)
    for hh in range(ATT_HEADS):
        q_ref[0, hh] = h[:, hh * ATT_HEAD_DIM:(hh + 1) * ATT_HEAD_DIM].astype(BF16)
    ht = lax.dot_general(wkt_ref[...], xb, (((1,), (1,)), ((), ())), preferred_element_type=F32)
    for hh in range(ATT_HEADS):
        kt_ref[0, hh] = ht[hh * ATT_HEAD_DIM:(hh + 1) * ATT_HEAD_DIM, :].astype(BF16)
    h = jnp.dot(xb, wqkv_ref[:, 2 * ATT_WIDTH:], preferred_element_type=F32)
    lane = lax.broadcasted_iota(I32, (h.shape[0], V_PAD), 1)
    tail = jnp.where(lane == ATT_HEAD_DIM, 1.0, 0.0)
    for pair in range(ATT_HEADS // 2):
        slab = h[:, pair * V_PAD:(pair + 1) * V_PAD]
        v_ref[0, 2 * pair] = jnp.where(lane < ATT_HEAD_DIM, slab, tail).astype(BF16)
        v_ref[0, 2 * pair + 1] = jnp.where(lane < ATT_HEAD_DIM, pltpu.roll(slab, ATT_HEAD_DIM, 1), tail).astype(BF16)
    hs = jnp.dot(xb, ws_ref[...], preferred_element_type=F32)
    up_ref[...] = hs[:, :POOL_WIDTH]
    us_ref[...] = hs[:, POOL_WIDTH:POOL_WIDTH + SSM_WIDTH]
    f_ref[...] = hs[:, POOL_WIDTH + SSM_WIDTH:]


def _proj(x2, wqkv, wkt, wsmall, B, S):
    N = B * S
    nS = S // TM_PROJ
    hm = jax.ShapeDtypeStruct((B, ATT_HEADS, S, ATT_HEAD_DIM), BF16)
    hm_spec = pl.BlockSpec((1, ATT_HEADS, TM_PROJ, ATT_HEAD_DIM), lambda b, s: (b, 0, s, 0))
    ht = jax.ShapeDtypeStruct((B, ATT_HEADS, ATT_HEAD_DIM, S), BF16)
    ht_spec = pl.BlockSpec((1, ATT_HEADS, ATT_HEAD_DIM, TM_PROJ), lambda b, s: (b, 0, 0, s))
    hv = jax.ShapeDtypeStruct((B, ATT_HEADS, S, V_PAD), BF16)
    hv_spec = pl.BlockSpec((1, ATT_HEADS, TM_PROJ, V_PAD), lambda b, s: (b, 0, s, 0))
    return pl.pallas_call(
        _proj_kernel,
        grid=(B, nS),
        in_specs=[
            pl.BlockSpec((TM_PROJ, D_MODEL), lambda b, s: (b * nS + s, 0)),
            pl.BlockSpec((D_MODEL, QKV_W), lambda b, s: (0, 0)),
            pl.BlockSpec((ATT_WIDTH, D_MODEL), lambda b, s: (0, 0)),
            pl.BlockSpec((D_MODEL, SMALL_W), lambda b, s: (0, 0)),
        ],
        out_specs=[
            hm_spec, ht_spec, hv_spec,
            pl.BlockSpec((TM_PROJ, POOL_WIDTH), lambda b, s: (b * nS + s, 0)),
            pl.BlockSpec((TM_PROJ, SSM_WIDTH), lambda b, s: (s, b)),
            pl.BlockSpec((TM_PROJ, F_PAD), lambda b, s: (b * nS + s, 0)),
        ],
        out_shape=[
            hm, ht, hv,
            jax.ShapeDtypeStruct((N, POOL_WIDTH), F32),
            jax.ShapeDtypeStruct((S, B * SSM_WIDTH), F32),
            jax.ShapeDtypeStruct((N, F_PAD), F32),
        ],
        compiler_params=_cparams(("parallel", "parallel")),
        name="in_proj",
    )(x2, wqkv, wkt, wsmall)


def _fcum_kernel(f_ref, b_ref, c_ref):
    rows, S = f_ref.shape
    lane = lax.broadcasted_iota(I32, (rows, LANES), 1)
    carry = jnp.zeros((rows, 1), F32)
    for ch in range(S // LANES):
        z = f_ref[:, ch * LANES:(ch + 1) * LANES] + b_ref[...]
        lf = jnp.minimum(z, 0.0) - jnp.log1p(jnp.exp(-jnp.abs(z)))
        sh = 1
        while sh < LANES:
            lf = lf + jnp.where(lane >= sh, pltpu.roll(lf, sh, 1), 0.0)
            sh *= 2
        lf = lf + carry
        c_ref[:, ch * LANES:(ch + 1) * LANES] = lf
        carry = lf[:, LANES - 1:LANES]


def _fcum(f_rows, b_rows):
    rows, S = f_rows.shape
    return pl.pallas_call(
        _fcum_kernel,
        out_shape=jax.ShapeDtypeStruct((rows, S), F32),
        compiler_params=pltpu.CompilerParams(vmem_limit_bytes=VMEM_LIMIT),
        name="forget_cumsum",
    )(f_rows, b_rows)


def _attn_kernel(q_ref, kt_ref, v_ref, c_ref, o_ref):
    T = T_ATT
    H = T // 2
    qi = pl.program_id(2)
    qs = [q_ref[0, hh] for hh in range(2)]

    def update(carry, q_pair, key_start, n_keys, diag):
        out = []
        for hh in range(2):
            m, acc = carry[2 * hh], carry[2 * hh + 1]
            kt = kt_ref[0, hh, :, pl.ds(key_start, n_keys)]
            v = v_ref[0, hh, pl.ds(key_start, n_keys), :]
            s = jnp.dot(q_pair[hh], kt, preferred_element_type=F32)
            s = s - c_ref[0, hh, :, pl.ds(key_start, n_keys)]
            if diag is not None:
                delta = lax.broadcasted_iota(I32, s.shape, 1) - lax.broadcasted_iota(I32, s.shape, 0)
                s = jnp.where(delta <= diag, s, -jnp.inf)
            m_new = jnp.maximum(m, jnp.max(s, axis=-1, keepdims=True))
            p = jnp.exp(s - m_new)
            acc = jnp.exp(m - m_new) * acc + jnp.dot(p.astype(BF16), v, preferred_element_type=F32)
            out += [m_new, acc]
        return tuple(out)

    m0 = jnp.full((T, 1), -jnp.inf, F32)
    acc0 = jnp.zeros((T, V_PAD), F32)
    carry = lax.fori_loop(
        0, qi, lambda j, c: update(c, qs, pl.multiple_of(j * T, T), T, None), (m0, acc0, m0, acc0))
    diag_start = pl.multiple_of(qi * T, T)
    halves = []
    for half in range(2):
        rows = slice(half * H, (half + 1) * H)
        halves.append(update(tuple(x[rows] for x in carry), [q[rows] for q in qs],
                             diag_start, (half + 1) * H, half * H))
    for hh in range(2):
        acc = jnp.concatenate([halves[0][2 * hh + 1], halves[1][2 * hh + 1]], axis=0)
        out = acc[:, :ATT_HEAD_DIM] / acc[:, ATT_HEAD_DIM:ATT_HEAD_DIM + 1]
        o_ref[0, :, hh * ATT_HEAD_DIM:(hh + 1) * ATT_HEAD_DIM] = out.astype(BF16)


def _attention(q, k, v, c, B, S):
    nq = S // T_ATT
    qspec = pl.BlockSpec((1, 2, T_ATT, ATT_HEAD_DIM), lambda b, hp, i: (b, hp, i, 0))
    kspec = pl.BlockSpec((1, 2, ATT_HEAD_DIM, S), lambda b, hp, i: (b, hp, 0, 0))
    vspec = pl.BlockSpec((1, 2, S, V_PAD), lambda b, hp, i: (b, hp, 0, 0))
    return pl.pallas_call(
        _attn_kernel,
        grid=(B, ATT_HEADS // 2, nq),
        in_specs=[qspec, kspec, vspec,
                  pl.BlockSpec((1, 2, 1, S), lambda b, hp, i: (b, hp, 0, 0))],
        out_specs=pl.BlockSpec((1, T_ATT, 2 * ATT_HEAD_DIM), lambda b, hp, i: (b, i, hp)),
        out_shape=jax.ShapeDtypeStruct((B, S, ATT_WIDTH), BF16),
        compiler_params=_cparams(("parallel", "parallel", "arbitrary")),
        name="fox_attention",
    )(q, k, v, c)


def _pool_kernel(u_ref, w_ref, sc_ref, o_ref, pad_ref):
    S = u_ref.shape[1]
    R = TM_POOL
    pad_ref[0:MAX_WINDOW, :] = jnp.zeros((MAX_WINDOW, POOL_WIDTH), F32)
    pad_ref[MAX_WINDOW:, :] = u_ref[0]
    lane = lax.broadcasted_iota(I32, (R, POOL_WIDTH), 1)
    trow = lax.broadcasted_iota(I32, (R, POOL_WIDTH), 0)
    grp = lane // POOL_GROUP_DIM
    win = jnp.where(grp == 0, POOL_WINDOWS[0],
                    jnp.where(grp == 1, POOL_WINDOWS[1], jnp.where(grp == 2, POOL_WINDOWS[2], POOL_WINDOWS[3])))
    for i in range(S // R):
        base = MAX_WINDOW + i * R
        u0 = pad_ref[base:base + R, :]
        acc = u0
        sums = {}
        for kk in range(1, MAX_WINDOW):
            acc = acc + pad_ref[base - kk:base - kk + R, :]
            if kk + 1 in POOL_WINDOWS:
                sums[kk + 1] = acc
        total = jnp.where(grp == 0, sums[POOL_WINDOWS[0]],
                          jnp.where(grp == 1, sums[POOL_WINDOWS[1]],
                                    jnp.where(grp == 2, sums[POOL_WINDOWS[2]], sums[POOL_WINDOWS[3]])))
        cnt = jnp.minimum(trow + (i * R + 1), win).astype(F32)
        mixed = total / cnt - u0
        y = jnp.dot(mixed.astype(BF16), w_ref[...], preferred_element_type=F32) * sc_ref[...]
        o_ref[0, i * R:(i + 1) * R, :] = y.astype(BF16)


def _pool(u3, w_bd, scale):
    B, S, _ = u3.shape
    return pl.pallas_call(
        _pool_kernel,
        grid=(B,),
        in_specs=[pl.BlockSpec((1, S, POOL_WIDTH), lambda b: (b, 0, 0)),
                  pl.BlockSpec((POOL_WIDTH, POOL_WIDTH), lambda b: (0, 0)),
                  pl.BlockSpec((1, POOL_WIDTH), lambda b: (0, 0))],
        out_specs=pl.BlockSpec((1, S, POOL_WIDTH), lambda b: (b, 0, 0)),
        out_shape=jax.ShapeDtypeStruct((B, S, POOL_WIDTH), BF16),
        scratch_shapes=[pltpu.VMEM((S + MAX_WINDOW, POOL_WIDTH), F32)],
        compiler_params=_cparams(("parallel",)),
        name="multiscale_pool",
    )(u3, w_bd, scale)


def _ssm_prep_kernel(lr_ref, li_ref, ldt_ref, brT_ref, biT_ref, ar_ref, ai_ref, bbr_ref, bbi_ref):
    lr = lr_ref[...]
    li = li_ref[...]
    dt = jnp.exp(ldt_ref[...])
    mag = jnp.exp(lr * dt)
    ar = mag * jnp.cos(li * dt)
    ai = mag * jnp.sin(li * dt)
    den = lr * lr + li * li
    nr = ar - 1.0
    zr = (nr * lr + ai * li) / den
    zi = (ai * lr - nr * li) / den
    ar_ref[...] = ar
    ai_ref[...] = ai
    br = brT_ref[...]
    bi = biT_ref[...]
    bbr_ref[...] = zr[:, None, :] * br - zi[:, None, :] * bi
    bbi_ref[...] = zr[:, None, :] * bi + zi[:, None, :] * br


def _ssm_prep(lr, li, log_dt, b_re, b_im):
    G, P, H = b_re.shape
    gp = jax.ShapeDtypeStruct((G, P), F32)
    ghp = jax.ShapeDtypeStruct((G, H, P), F32)
    return pl.pallas_call(
        _ssm_prep_kernel,
        out_shape=[gp, gp, ghp, ghp],
        name="s5_discretise",
    )(lr, li, log_dt.reshape(G, 1), b_re.transpose(0, 2, 1), b_im.transpose(0, 2, 1))


def _gelu_tanh(y):
    return 0.5 * y * (1.0 + jnp.tanh(GELU_C * (y + 0.044715 * (y * y * y))))


def _ssm_kernel(u_ref, bb_ref, ar_ref, ai_ref, cc_ref, d_ref, wg_ref, bg_ref, o_ref, x_scr, hr_scr, hi_scr):
    nb = hr_scr.shape[0]
    tt = u_ref.shape[0] // nb

    @pl.when(pl.program_id(0) == 0)
    def _():
        hr_scr[...] = jnp.zeros_like(hr_scr)
        hi_scr[...] = jnp.zeros_like(hi_scr)

    hr = hr_scr[...]
    hi = hi_scr[...]
    ar = ar_ref[...]
    ai = ai_ref[...]
    rows_sub = SUB_SSM * nb
    for sc in range(tt // SUB_SSM):
        rows = slice(sc * rows_sub, (sc + 1) * rows_sub)
        u = u_ref[rows, :]
        x = jnp.dot(u.astype(BF16), bb_ref[...], preferred_element_type=F32)
        for t in range(SUB_SSM):
            xr = x[t * nb:(t + 1) * nb, 0:SSM_STATES]
            xi = x[t * nb:(t + 1) * nb, SSM_STATES:2 * SSM_STATES]
            hr, hi = ar * hr - ai * hi + xr, ar * hi + ai * hr + xi
            r0 = sc * rows_sub + t * nb
            x_scr[r0:r0 + nb, 0:SSM_STATES] = hr
            x_scr[r0:r0 + nb, SSM_STATES:2 * SSM_STATES] = hi
        y = jnp.dot(x_scr[rows, :].astype(BF16), cc_ref[...], preferred_element_type=F32) + d_ref[...] * u
        y = _gelu_tanh(y)
        gl = jnp.dot(y.astype(BF16), wg_ref[...], preferred_element_type=F32) + bg_ref[...]
        o_ref[rows, :] = (y * _sigmoid(gl)).astype(BF16)
    hr_scr[...] = hr
    hi_scr[...] = hi


def _ssm(u_tm, bblk, ar_b, ai_b, cblk, dvec, wglu, bglu, B, S):
    rows = TT_SSM * B
    const = lambda shape: pl.BlockSpec(shape, lambda i: (0, 0))
    return pl.pallas_call(
        _ssm_kernel,
        grid=(S // TT_SSM,),
        in_specs=[pl.BlockSpec((rows, SSM_WIDTH), lambda i: (i, 0)),
                  const((SSM_WIDTH, 2 * SSM_STATES)),
                  const((B, SSM_STATES)), const((B, SSM_STATES)),
                  const((2 * SSM_STATES, SSM_WIDTH)),
                  const((1, SSM_WIDTH)), const((SSM_WIDTH, SSM_WIDTH)), const((1, SSM_WIDTH))],
        out_specs=pl.BlockSpec((rows, SSM_WIDTH), lambda i: (i, 0)),
        out_shape=jax.ShapeDtypeStruct((S * B, SSM_WIDTH), BF16),
        scratch_shapes=[pltpu.VMEM((rows, 2 * SSM_STATES), F32),
                        pltpu.VMEM((B, SSM_STATES), F32), pltpu.VMEM((B, SSM_STATES), F32)],
        compiler_params=_cparams(("arbitrary",)),
        name="s5_scan",
    )(u_tm, bblk, ar_b, ai_b, cblk, dvec, wglu, bglu)


def _merge_kernel(x_ref, ya_ref, yb_ref, yc_ref, wg_ref, wa_ref, wb_ref, wc_ref, wo_ref, g_ref, b_ref, o_ref):
    x = x_ref[...]
    xb = x.astype(BF16)
    merged = None
    for i, (y_ref, w_ref) in enumerate(((ya_ref, wa_ref), (yb_ref, wb_ref), (yc_ref, wc_ref))):
        gate = _sigmoid(jnp.dot(xb, wg_ref[:, i * D_MODEL:(i + 1) * D_MODEL], preferred_element_type=F32))
        term = gate * jnp.dot(y_ref[...], w_ref[...], preferred_element_type=F32)
        merged = term if merged is None else merged + term
    mix = jnp.dot(merged.astype(BF16), wo_ref[...], preferred_element_type=F32)
    o_ref[...] = _layer_norm(DEEPNORM_ALPHA * x + mix, g_ref[...], b_ref[...])


def _merge(x2, ya, yb, yc_tm, wg, wa, wb, wc, wo, g, b, B, S):
    N = B * S
    nS = S // TM_MERGE
    const = lambda shape: pl.BlockSpec(shape, lambda bb, s: (0, 0))
    row = lambda w: pl.BlockSpec((TM_MERGE, w), lambda bb, s: (bb * nS + s, 0))
    return pl.pallas_call(
        _merge_kernel,
        grid=(B, nS),
        in_specs=[row(D_MODEL), row(ATT_WIDTH), row(POOL_WIDTH),
                  pl.BlockSpec((TM_MERGE, SSM_WIDTH), lambda bb, s: (s, bb)),
                  const((D_MODEL, N_BRANCH * D_MODEL)), const((ATT_WIDTH, D_MODEL)),
                  const((POOL_WIDTH, D_MODEL)), const((SSM_WIDTH, D_MODEL)), const((D_MODEL, D_MODEL)),
                  const((1, D_MODEL)), const((1, D_MODEL))],
        out_specs=row(D_MODEL),
        out_shape=jax.ShapeDtypeStruct((N, D_MODEL), F32),
        compiler_params=_cparams(("parallel", "parallel")),
        name="merge_ln1",
    )(x2, ya, yb, yc_tm, wg, wa, wb, wc, wo, g, b)


def _route_kernel(x_ref, wr_ref, br_ref, gate_ref, slot_ref, slott_ref, cnt_ref):
    tm = x_ref.shape[0]
    x = x_ref[...]
    xh = x.astype(BF16)
    xl = (x - xh.astype(F32)).astype(BF16)
    w = wr_ref[...]
    wh = w.astype(BF16)
    wl = (w - wh.astype(F32)).astype(BF16)
    logits = (jnp.dot(xh, wh, preferred_element_type=F32) + jnp.dot(xl, wh, preferred_element_type=F32)
              + jnp.dot(xh, wl, preferred_element_type=F32)) + br_ref[...]
    work = logits.T[0:N_EXPERTS, :]
    sub = lax.broadcasted_iota(I32, (N_EXPERTS, tm), 0).astype(F32)
    vals, idxs = [], []
    multihot = jnp.zeros((N_EXPERTS, tm), F32)
    for _ in range(TOP_K):
        mx = jnp.max(work, axis=0, keepdims=True)
        ix = jnp.min(jnp.where(work == mx, sub, float(N_EXPERTS)), axis=0, keepdims=True)
        sel = sub == ix
        work = jnp.where(sel, -jnp.inf, work)
        multihot = multihot + jnp.where(sel, 1.0, 0.0)
        vals.append(mx)
        idxs.append(ix)
    exps = [jnp.exp(v - vals[0]) for v in vals]
    denom = exps[0] + exps[1] + exps[2] + exps[3]
    r = lax.broadcasted_iota(I32, (tm, tm), 0)
    c = lax.broadcasted_iota(I32, (tm, tm), 1)
    earlier = jnp.where(r < c, 1.0, 0.0).astype(BF16)
    before = jnp.dot(multihot.astype(BF16), earlier, preferred_element_type=F32)
    n = jnp.sum(multihot, axis=1, keepdims=True)
    n_pad = jnp.floor((n + float(SUBLANES - 1)) * (1.0 / SUBLANES)) * float(SUBLANES)
    n_pad_b = jnp.broadcast_to(n_pad, (N_EXPERTS, LANES))
    er = lax.broadcasted_iota(I32, (N_EXPERTS, N_EXPERTS), 0)
    ec = lax.broadcasted_iota(I32, (N_EXPERTS, N_EXPERTS), 1)
    below = jnp.where(ec < er, 1.0, 0.0).astype(BF16)
    run_start = jnp.dot(below, n_pad_b.astype(BF16), preferred_element_type=F32)[:, 0:1]
    where_in_tile = before + run_start
    slots = [jnp.sum(jnp.where(sub == idxs[kk], where_in_tile, 0.0), axis=0, keepdims=True) for kk in range(TOP_K)]
    gates = [exps[kk] / denom for kk in range(TOP_K)]
    sub8 = lax.broadcasted_iota(I32, (SUBLANES, tm), 0)
    slot8 = jnp.zeros((SUBLANES, tm), F32)
    gate8 = jnp.zeros((SUBLANES, tm), F32)
    for kk in range(TOP_K):
        slot8 = jnp.where(sub8 == kk, slots[kk], slot8)
        gate8 = jnp.where(sub8 == kk, gates[kk], gate8)
    slott_ref[0] = slot8.astype(I32)
    pad_rows = jnp.zeros((LANES - SUBLANES, tm), F32)
    slot_ref[...] = jnp.concatenate([slot8, pad_rows], axis=0).T.astype(I32)
    gate_ref[...] = jnp.concatenate([gate8, pad_rows], axis=0).T
    cnt_ref[0] = n_pad_b.astype(I32)


def _route(x1, wr, br):
    N = x1.shape[0]
    nt = N // TM_ROW
    row = pl.BlockSpec((TM_ROW, LANES), lambda i: (i, 0))
    return pl.pallas_call(
        _route_kernel,
        grid=(nt,),
        in_specs=[pl.BlockSpec((TM_ROW, D_MODEL), lambda i: (i, 0)),
                  pl.BlockSpec((D_MODEL, LANES), lambda i: (0, 0)),
                  pl.BlockSpec((1, LANES), lambda i: (0, 0))],
        out_specs=[row, row,
                   pl.BlockSpec((1, SUBLANES, TM_ROW), lambda i: (i, 0, 0)),
                   pl.BlockSpec((1, N_EXPERTS, LANES), lambda i: (i, 0, 0))],
        out_shape=[jax.ShapeDtypeStruct((N, LANES), F32), jax.ShapeDtypeStruct((N, LANES), I32),
                   jax.ShapeDtypeStruct((nt, SUBLANES, TM_ROW), I32),
                   jax.ShapeDtypeStruct((nt, N_EXPERTS, LANES), I32)],
        compiler_params=_cparams(("parallel",)),
        name="router_top4",
    )(x1, wr, br)


def _issue_runs(tab_ref, make_copy):
    def issue_one(e, _):
        rows = tab_ref[0, 0, 2 * N_EXPERTS + e]

        @pl.when(rows > 0)
        def _():
            make_copy(pl.multiple_of(tab_ref[0, 0, e], SUBLANES),
                      pl.multiple_of(tab_ref[0, 0, N_EXPERTS + e], SUBLANES),
                      pl.multiple_of(rows, SUBLANES)).start()
        return 0

    lax.fori_loop(0, N_EXPERTS, issue_one, 0)


def _wait_runs(tab_ref, make_copy):
    total = tab_ref[0, 0, 3 * N_EXPERTS]

    @pl.when(total > 0)
    def _():
        make_copy(0, 0, pl.multiple_of(total, SUBLANES)).wait()


def _dispatch_kernel(tab_ref, tab_prev_ref, x_ref, slott_ref, xs_hbm, srt_scr, sem):
    tm = x_ref.shape[0]
    i = pl.program_id(0)
    last = pl.num_programs(0) - 1
    cur = lax.rem(i, 2)
    r = lax.broadcasted_iota(I32, (ROWS_TILE, tm), 0)
    hit = r == slott_ref[0, 0:1, :]
    for kk in range(1, TOP_K):
        hit = hit | (r == slott_ref[0, kk:kk + 1, :])
    sel = jnp.where(hit, 1.0, 0.0).astype(BF16)
    srt = jnp.dot(sel, x_ref[...].astype(BF16), preferred_element_type=F32)
    srt_scr[cur] = _pack_rows(srt)

    def out_copy(buf):
        return lambda local, remote, rows: pltpu.make_async_copy(
            srt_scr.at[buf, pl.ds(local, rows)], xs_hbm.at[pl.ds(remote, rows)], sem.at[buf])

    _issue_runs(tab_ref, out_copy(cur))

    @pl.when(i > 0)
    def _():
        _wait_runs(tab_prev_ref, out_copy(1 - cur))

    @pl.when(i == last)
    def _():
        _wait_runs(tab_ref, out_copy(cur))
        used = pl.multiple_of(tab_ref[0, 0, TAB_W - 1], SUBLANES)
        free = xs_hbm.shape[0] - used
        n_big = free // BM_EXP
        n_small = (free - n_big * BM_EXP) // SUBLANES
        srt_scr[0, 0:BM_EXP, :] = jnp.zeros((BM_EXP, HALF_D), U32)

        def big(i):
            return pltpu.make_async_copy(
                srt_scr.at[0, pl.ds(0, BM_EXP)],
                xs_hbm.at[pl.ds(pl.multiple_of(used + i * BM_EXP, SUBLANES), BM_EXP)], sem.at[0])

        def small(i):
            return pltpu.make_async_copy(
                srt_scr.at[0, pl.ds(0, SUBLANES)],
                xs_hbm.at[pl.ds(pl.multiple_of(used + n_big * BM_EXP + i * SUBLANES, SUBLANES), SUBLANES)],
                sem.at[0])

        def for_each(make, count, op):
            def body(i, c):
                op(make(i))
                return c
            lax.fori_loop(0, count, body, 0)

        for op in (lambda cp: cp.start(), lambda cp: cp.wait()):
            for_each(big, n_big, op)
            for_each(small, n_small, op)


def _dispatch(tab, x1, slott, m_pad):
    N = x1.shape[0]
    return pl.pallas_call(
        _dispatch_kernel,
        grid=(N // TM_ROW,),
        in_specs=[pl.BlockSpec((1, 1, TAB_W), lambda i: (i, 0, 0), memory_space=pltpu.SMEM),
                  pl.BlockSpec((1, 1, TAB_W), lambda i: (jnp.maximum(i - 1, 0), 0, 0), memory_space=pltpu.SMEM),
                  pl.BlockSpec((TM_ROW, D_MODEL), lambda i: (i, 0)),
                  pl.BlockSpec((1, SUBLANES, TM_ROW), lambda i: (i, 0, 0))],
        out_specs=pl.BlockSpec(memory_space=pl.ANY),
        out_shape=jax.ShapeDtypeStruct((m_pad, HALF_D), U32),
        scratch_shapes=[pltpu.VMEM((2, ROWS_TILE, HALF_D), U32), pltpu.SemaphoreType.DMA((2,))],
        compiler_params=_cparams(("arbitrary",), has_side_effects=True),
        name="moe_dispatch",
    )(tab, tab, x1, slott)


def _expert_kernel(blk_ref, exp_ref, lo_ref, hi_ref, new_ref, xs_ref, wu_ref, bu_ref, wd_ref, bd_ref, perm_ref,
                   o_ref, wu_scr, wd_scr):
    w = pl.program_id(0)
    lo = lo_ref[w]
    hi = hi_ref[w]

    @pl.when(lo > hi)
    def _():
        o_ref[...] = jnp.zeros_like(o_ref)

    @pl.when(new_ref[w] == 1)
    def _():
        for cc in range(2 * D_FF // (2 * LANES)):
            cols = slice(cc * 2 * LANES, (cc + 1) * 2 * LANES)
            wu_scr[:, cols] = jnp.dot(wu_ref[0, 0, :, cols].astype(BF16), perm_ref[...],
                                      preferred_element_type=F32).astype(BF16)
        wd_scr[...] = wd_ref[0, 0].astype(BF16)

    @pl.when(hi > lo)
    def _():
        xh, xl = _unpack_rows(xs_ref[...])
        x = jnp.concatenate([xh.astype(BF16), xl.astype(BF16)], axis=1)
        hu = jnp.dot(x, wu_scr[...], preferred_element_type=F32) + bu_ref[0]
        chunks = []
        for cc in range(D_FF // LANES):
            g = jnp.minimum(hu[:, 2 * cc * LANES:(2 * cc + 1) * LANES], SWIGLU_LIMIT)
            l = jnp.clip(hu[:, (2 * cc + 1) * LANES:(2 * cc + 2) * LANES], -SWIGLU_LIMIT, SWIGLU_LIMIT)
            chunks.append(g * _sigmoid(SWIGLU_ALPHA * g) * (l + 1.0))
        act = jnp.concatenate(chunks, axis=1)
        y = jnp.dot(act.astype(BF16), wd_scr[...], preferred_element_type=F32) + bd_ref[0]
        packed = _pack_rows(y)

        @pl.when(lo == 0)
        def _():
            o_ref[...] = packed

        @pl.when(lo > 0)
        def _():
            row = lax.broadcasted_iota(I32, packed.shape, 0)
            o_ref[...] = jnp.where((row >= lo) & (row < hi), packed, o_ref[...])


def _pair_perm():
    p = np.zeros((2 * LANES, 2 * LANES), np.float32)
    j = np.arange(LANES)
    p[2 * j, j] = 1.0
    p[2 * j + 1, LANES + j] = 1.0
    return jnp.asarray(p, BF16)


def _experts(meta, xs, w_up, bu, w_down, bd, layer, perm):
    M = xs.shape[0]
    W = meta[0].shape[0]
    by_expert = lambda *shape: pl.BlockSpec((1,) + shape, lambda w, blk, ex, lo, hi, new: (ex[w], 0, 0))
    stacked = lambda *shape: pl.BlockSpec((1, 1) + shape, lambda w, blk, ex, lo, hi, new: (layer, ex[w], 0, 0))
    rows = pl.BlockSpec((BM_EXP, HALF_D), lambda w, blk, ex, lo, hi, new: (blk[w], 0))
    return pl.pallas_call(
        _expert_kernel,
        grid_spec=pltpu.PrefetchScalarGridSpec(
            num_scalar_prefetch=5,
            grid=(W,),
            in_specs=[rows, stacked(D_MODEL, 2 * D_FF), by_expert(1, 2 * D_FF),
                      stacked(D_FF, D_MODEL), by_expert(1, D_MODEL),
                      pl.BlockSpec((2 * LANES, 2 * LANES), lambda w, blk, ex, lo, hi, new: (0, 0))],
            out_specs=rows,
            scratch_shapes=[pltpu.VMEM((D_MODEL, 2 * D_FF), BF16), pltpu.VMEM((D_FF, D_MODEL), BF16)],
        ),
        out_shape=jax.ShapeDtypeStruct((M, HALF_D), U32),
        compiler_params=_cparams(("arbitrary",)),
        name="moe_experts",
    )(*meta, xs, w_up, bu, w_down, bd, perm)


def _combine_kernel(tab_ref, tab_next_ref, x_ref, gate_ref, slot_ref, g_ref, b_ref, yb_hbm, o_ref, buf, sem):
    tm = x_ref.shape[0]
    i = pl.program_id(0)
    cur = lax.rem(i, 2)

    def in_copy(b):
        return lambda local, remote, rows: pltpu.make_async_copy(
            yb_hbm.at[pl.ds(remote, rows)], buf.at[b, pl.ds(local, rows)], sem.at[b])

    @pl.when(i == 0)
    def _():
        buf[...] = jnp.zeros_like(buf)
        _issue_runs(tab_ref, in_copy(cur))

    @pl.when(i + 1 < pl.num_programs(0))
    def _():
        _issue_runs(tab_next_ref, in_copy(1 - cur))

    r = lax.broadcasted_iota(I32, (tm, ROWS_TILE), 1)
    slots = slot_ref[...]
    gates = gate_ref[...]
    pw = jnp.zeros((tm, ROWS_TILE), F32)
    for kk in range(TOP_K):
        pw = jnp.where(r == slots[:, kk:kk + 1], gates[:, kk:kk + 1], pw)
    pw = pw.astype(BF16)
    _wait_runs(tab_ref, in_copy(cur))
    yh, yl = _unpack_rows(buf[cur])
    ys = jnp.concatenate([yh.astype(BF16), yl.astype(BF16)], axis=1)
    y = jnp.dot(pw, ys, preferred_element_type=F32)
    o_ref[...] = _layer_norm(DEEPNORM_ALPHA * x_ref[...] + y, g_ref[...], b_ref[...])


def _combine(tab, x1, gate, slot, g, b, yb):
    N = x1.shape[0]
    nt = N // TM_ROW
    return pl.pallas_call(
        _combine_kernel,
        grid=(N // TM_ROW,),
        in_specs=[pl.BlockSpec((1, 1, TAB_W), lambda i: (i, 0, 0), memory_space=pltpu.SMEM),
                  pl.BlockSpec((1, 1, TAB_W), lambda i: (jnp.minimum(i + 1, nt - 1), 0, 0), memory_space=pltpu.SMEM),
                  pl.BlockSpec((TM_ROW, D_MODEL), lambda i: (i, 0)),
                  pl.BlockSpec((TM_ROW, LANES), lambda i: (i, 0)),
                  pl.BlockSpec((TM_ROW, LANES), lambda i: (i, 0)),
                  pl.BlockSpec((1, D_MODEL), lambda i: (0, 0)),
                  pl.BlockSpec((1, D_MODEL), lambda i: (0, 0)),
                  pl.BlockSpec(memory_space=pl.ANY)],
        out_specs=pl.BlockSpec((TM_ROW, D_MODEL), lambda i: (i, 0)),
        out_shape=jax.ShapeDtypeStruct((N, D_MODEL), F32),
        scratch_shapes=[pltpu.VMEM((2, ROWS_TILE, HALF_D), U32), pltpu.SemaphoreType.DMA((2,))],
        compiler_params=_cparams(("arbitrary",)),
        name="moe_combine_ln2",
    )(tab, tab, x1, gate, slot, g, b, yb)


def _moe_tables(cnt, m_pad):
    n_pad = cnt[:, :, 0]
    meta, region = _work_items(n_pad.sum(axis=0), m_pad)
    run_end = jnp.cumsum(n_pad, axis=1)
    run_start = run_end - n_pad
    hbm_start = region[None, :] + jnp.cumsum(n_pad, axis=0) - n_pad
    total = run_end[:, -1:]
    pad = jnp.zeros((n_pad.shape[0], TAB_W - 3 * N_EXPERTS - 2), I32)
    used = jnp.broadcast_to(n_pad.sum(), total.shape)
    tab = jnp.concatenate([run_start, hbm_start, n_pad, total, pad, used], axis=1).astype(I32)
    return tab[:, None, :], meta


def _work_items(counts, M):
    nblk = M // BM_EXP
    W = nblk + N_EXPERTS
    ends = jnp.cumsum(counts)
    starts = ends - counts
    first = starts // BM_EXP
    last = jnp.maximum(ends - 1, 0) // BM_EXP
    n_items = jnp.where(counts > 0, last - first + 1, 0)
    item_end = jnp.cumsum(n_items)
    item_start = item_end - n_items
    total = item_end[-1]
    w = jnp.arange(W, dtype=I32)
    w_eff = jnp.minimum(w, total - 1)
    owner = (item_end[None, :] <= w_eff[:, None]).sum(axis=1).astype(I32)
    sel = owner[:, None] == jnp.arange(N_EXPERTS, dtype=I32)[None, :]
    pick = lambda table: jnp.sum(jnp.where(sel, table[None, :], 0), axis=1)
    blk = pick(first) + (w_eff - pick(item_start))
    lo = jnp.clip(pick(starts) - blk * BM_EXP, 0, BM_EXP)
    hi = jnp.clip(pick(ends) - blk * BM_EXP, 0, BM_EXP)
    valid = w < total
    used_blocks = (ends[-1] + BM_EXP - 1) // BM_EXP
    spare_blk = used_blocks + (w - total)
    fill = (~valid) & (spare_blk < nblk)
    blk = jnp.where(valid, blk, jnp.minimum(spare_blk, nblk - 1))
    lo = jnp.where(valid, lo, jnp.where(fill, 1, 0))
    hi = jnp.where(valid, hi, 0)
    new = valid & (w == pick(item_start))
    return (blk.astype(I32), owner, lo.astype(I32), hi.astype(I32), new.astype(I32)), starts


def _block_diag(blocks):
    G, a, b = blocks.shape
    eye = jnp.eye(G, dtype=blocks.dtype)
    return jnp.einsum("gab,gh->gahb", blocks, eye).reshape(G * a, G * b)


def kernel(x, w_in, b_forget, w_pool, pool_scale, ssm_lambda_re, ssm_lambda_im, ssm_log_dt, ssm_b_re, ssm_b_im, ssm_c_re, ssm_c_im, ssm_d, w_glu, b_glu, w_branch_a, w_branch_b, w_branch_c, w_out, ln1_g, ln1_b, w_router, b_router, w_up, b_up, w_down, b_down, ln2_g, ln2_b):
    B, S, D = x.shape
    assert D == D_MODEL and S % T_ATT == 0 and S % TM_PROJ == 0
    N = B * S
    m_pad = (N // TM_ROW) * ROWS_TILE
    assert m_pad % BM_EXP == 0
    perm = _pair_perm()
    x2 = x.reshape(N, D)
    scale = ATT_HEAD_DIM ** -0.5
    for l in range(DEPTH):
        wl = w_in[l]
        wq = wl[:, :ATT_WIDTH] * scale
        wqkv = jnp.concatenate([wq, wl[:, ATT_WIDTH:QKV_W]], axis=1).astype(BF16)
        c0 = QKV_W
        wf = wl[:, c0:c0 + ATT_HEADS]
        wsmall = jnp.concatenate(
            [wl[:, c0 + ATT_HEADS:c0 + ATT_HEADS + POOL_WIDTH + SSM_WIDTH], wf,
             jnp.zeros((D, F_PAD - ATT_HEADS), F32)], axis=1).astype(BF16)
        wgates = wl[:, c0 + ATT_HEADS + POOL_WIDTH + SSM_WIDTH:].astype(BF16)

        wkt = wl[:, ATT_WIDTH:2 * ATT_WIDTH].T.astype(BF16)
        q, k, v, u_pool, u_ssm_tm, f_pad = _proj(x2, wqkv, wkt, wsmall, B, S)

        f_rows = f_pad[:, :ATT_HEADS].reshape(B, S, ATT_HEADS).transpose(0, 2, 1).reshape(B * ATT_HEADS, S)
        b_rows = jnp.tile(b_forget[l], B).reshape(B * ATT_HEADS, 1)
        c = _fcum(f_rows, b_rows).reshape(B, ATT_HEADS, 1, S)
        y_a = _attention(q, k, v, c, B, S).reshape(N, ATT_WIDTH)

        w_pool_bd = _block_diag(w_pool[l]).astype(BF16)
        y_b = _pool(u_pool.reshape(B, S, POOL_WIDTH), w_pool_bd, pool_scale[l].reshape(1, POOL_WIDTH)).reshape(N, POOL_WIDTH)

        ar, ai, bbrT, bbiT = _ssm_prep(ssm_lambda_re[l], ssm_lambda_im[l], ssm_log_dt[l], ssm_b_re[l], ssm_b_im[l])
        bblk = jnp.concatenate([_block_diag(bbrT), _block_diag(bbiT)], axis=1).astype(BF16)
        cblk = jnp.concatenate([_block_diag(ssm_c_re[l].transpose(0, 2, 1)),
                                -_block_diag(ssm_c_im[l].transpose(0, 2, 1))], axis=0).astype(BF16)
        ar_b = jnp.broadcast_to(ar.reshape(1, SSM_STATES), (B, SSM_STATES))
        ai_b = jnp.broadcast_to(ai.reshape(1, SSM_STATES), (B, SSM_STATES))
        y_c_tm = _ssm(u_ssm_tm.reshape(S * B, SSM_WIDTH), bblk, ar_b, ai_b, cblk,
                      ssm_d[l].reshape(1, SSM_WIDTH), w_glu[l].astype(BF16), b_glu[l].reshape(1, SSM_WIDTH), B, S)

        x1 = _merge(x2, y_a, y_b, y_c_tm.reshape(S, B * SSM_WIDTH), wgates,
                    w_branch_a[l].astype(BF16), w_branch_b[l].astype(BF16), w_branch_c[l].astype(BF16),
                    w_out[l].astype(BF16), ln1_g[l].reshape(1, D), ln1_b[l].reshape(1, D), B, S)

        wr = jnp.concatenate([w_router[l], jnp.zeros((D, LANES - N_EXPERTS), F32)], axis=1)
        br = jnp.concatenate([b_router[l], jnp.full((LANES - N_EXPERTS,), _NEG_BIG, F32)]).reshape(1, LANES)
        gate, slot, slott, cnt = _route(x1, wr, br)
        tab, meta = _moe_tables(cnt, m_pad)
        xs = _dispatch(tab, x1, slott, m_pad)
        bu = b_up[l].reshape(N_EXPERTS, D_FF // LANES, LANES, 2).transpose(0, 1, 3, 2).reshape(N_EXPERTS, 1, 2 * D_FF)
        yb = _experts(meta, xs, w_up, bu, w_down, b_down[l][:, None, :], l, perm)
        x2 = _combine(tab, x1, gate, slot, ln2_g[l].reshape(1, D), ln2_b[l].reshape(1, D), yb)
    return x2.reshape(B, S, D)
```

```python
import math

import jax
import jax.numpy as jnp
import numpy as np
from jax import lax
from jax.experimental import pallas as pl
from jax.experimental.pallas import tpu as pltpu

F32 = jnp.float32
BF16 = jnp.bfloat16
I32 = jnp.int32
U32 = jnp.uint32

D_MODEL = 1024
DEPTH = 4
ATT_HEADS = 8
ATT_HEAD_DIM = 64
ATT_WIDTH = ATT_HEADS * ATT_HEAD_DIM
POOL_WINDOWS = (2, 4, 8, 16)
POOL_GROUPS = 4
POOL_WIDTH = 256
POOL_GROUP_DIM = 64
MAX_WINDOW = max(POOL_WINDOWS)
SSM_WIDTH = 256
SSM_GROUP_DIM = 16
SSM_GROUPS = 16
SSM_STATE = 64
SSM_STATES = SSM_GROUPS * SSM_STATE
N_BRANCH = 3
N_EXPERTS = 32
TOP_K = 4
D_FF = D_MODEL
SWIGLU_LIMIT = 7.0
SWIGLU_ALPHA = 1.702
LN_EPS = 1e-5
DEEPNORM_ALPHA = (2.0 * DEPTH) ** 0.25
GELU_C = math.sqrt(2.0 / math.pi)

LANES = 128
HALF_D = D_MODEL // 2
F_PAD = LANES
V_PAD = LANES
SMALL_W = POOL_WIDTH + SSM_WIDTH + F_PAD
QKV_W = 3 * ATT_WIDTH
TM_PROJ = 512
T_ATT = 1024
TM_POOL = 256
TT_SSM = 128
SUB_SSM = 16
TM_MERGE = 512
SUBLANES = 8
TM_ROW = 256
ROWS_TILE = TM_ROW * TOP_K + N_EXPERTS * SUBLANES
TAB_W = LANES
assert TAB_W >= 3 * N_EXPERTS + 2
BM_EXP = 512
VMEM_LIMIT = 52 * 1024 * 1024

_NEG_BIG = -1e30


def _cparams(sem, **kw):
    return pltpu.CompilerParams(dimension_semantics=sem, vmem_limit_bytes=VMEM_LIMIT, **kw)


def _sigmoid(x):
    return 1.0 / (1.0 + jnp.exp(-x))


def _layer_norm(z, g, b):
    mu = jnp.mean(z, axis=-1, keepdims=True)
    zc = z - mu
    var = jnp.mean(zc * zc, axis=-1, keepdims=True)
    return zc * lax.rsqrt(var + LN_EPS) * g + b


def _pack_rows(y):
    u = pltpu.bitcast(y.astype(BF16).astype(F32), U32)
    return u[:, :HALF_D] | (u[:, HALF_D:] >> 16)


def _unpack_rows(p):
    hi = pltpu.bitcast(p & jnp.uint32(0xFFFF0000), F32)
    lo = pltpu.bitcast(p << 16, F32)
    return hi, lo


def _proj_kernel(x_ref, wqkv_ref, wkt_ref, ws_ref, q_ref, kt_ref, v_ref, up_ref, us_ref, f_ref):
    xb = x_ref[...].astype(BF16)
    h = jnp.dot(xb, wqkv_ref[:, :ATT_WIDTH], preferred_element_type=F32)
    for hh in range(ATT_HEADS):
        q_ref[0, hh] = h[:, hh * ATT_HEAD_DIM:(hh + 1) * ATT_HEAD_DIM].astype(BF16)
    ht = lax.dot_general(wkt_ref[...], xb, (((1,), (1,)), ((), ())), preferred_element_type=F32)
    for hh in range(ATT_HEADS):
        kt_ref[0, hh] = ht[hh * ATT_HEAD_DIM:(hh + 1) * ATT_HEAD_DIM, :].astype(BF16)
    h = jnp.dot(xb, wqkv_ref[:, 2 * ATT_WIDTH:], preferred_element_type=F32)
    lane = lax.broadcasted_iota(I32, (h.shape[0], V_PAD), 1)
    tail = jnp.where(lane == ATT_HEAD_DIM, 1.0, 0.0)
    for pair in range(ATT_HEADS // 2):
        slab = h[:, pair * V_PAD:(pair + 1) * V_PAD]
        v_ref[0, 2 * pair] = jnp.where(lane < ATT_HEAD_DIM, slab, tail).astype(BF16)
        v_ref[0, 2 * pair + 1] = jnp.where(lane < ATT_HEAD_DIM, pltpu.roll(slab, ATT_HEAD_DIM, 1), tail).astype(BF16)
    hs = jnp.dot(xb, ws_ref[...], preferred_element_type=F32)
    up_ref[...] = hs[:, :POOL_WIDTH]
    us_ref[...] = hs[:, POOL_WIDTH:POOL_WIDTH + SSM_WIDTH]
    f_ref[...] = hs[:, POOL_WIDTH + SSM_WIDTH:]


def _proj(x2, wqkv, wkt, wsmall, B, S):
    N = B * S
    nS = S // TM_PROJ
    hm = jax.ShapeDtypeStruct((B, ATT_HEADS, S, ATT_HEAD_DIM), BF16)
    hm_spec = pl.BlockSpec((1, ATT_HEADS, TM_PROJ, ATT_HEAD_DIM), lambda b, s: (b, 0, s, 0))
    ht = jax.ShapeDtypeStruct((B, ATT_HEADS, ATT_HEAD_DIM, S), BF16)
    ht_spec = pl.BlockSpec((1, ATT_HEADS, ATT_HEAD_DIM, TM_PROJ), lambda b, s: (b, 0, 0, s))
    hv = jax.ShapeDtypeStruct((B, ATT_HEADS, S, V_PAD), BF16)
    hv_spec = pl.BlockSpec((1, ATT_HEADS, TM_PROJ, V_PAD), lambda b, s: (b, 0, s, 0))
    return pl.pallas_call(
        _proj_kernel,
        grid=(B, nS),
        in_specs=[
            pl.BlockSpec((TM_PROJ, D_MODEL), lambda b, s: (b * nS + s, 0)),
            pl.BlockSpec((D_MODEL, QKV_W), lambda b, s: (0, 0)),
            pl.BlockSpec((ATT_WIDTH, D_MODEL), lambda b, s: (0, 0)),
            pl.BlockSpec((D_MODEL, SMALL_W), lambda b, s: (0, 0)),
        ],
        out_specs=[
            hm_spec, ht_spec, hv_spec,
            pl.BlockSpec((TM_PROJ, POOL_WIDTH), lambda b, s: (b * nS + s, 0)),
            pl.BlockSpec((TM_PROJ, SSM_WIDTH), lambda b, s: (s, b)),
            pl.BlockSpec((TM_PROJ, F_PAD), lambda b, s: (b * nS + s, 0)),
        ],
        out_shape=[
            hm, ht, hv,
            jax.ShapeDtypeStruct((N, POOL_WIDTH), F32),
            jax.ShapeDtypeStruct((S, B * SSM_WIDTH), F32),
            jax.ShapeDtypeStruct((N, F_PAD), F32),
        ],
        compiler_params=_cparams(("parallel", "parallel")),
        name="in_proj",
    )(x2, wqkv, wkt, wsmall)


def _fcum_kernel(f_ref, b_ref, c_ref):
    rows, S = f_ref.shape
    lane = lax.broadcasted_iota(I32, (rows, LANES), 1)
    carry = jnp.zeros((rows, 1), F32)
    for ch in range(S // LANES):
        z = f_ref[:, ch * LANES:(ch + 1) * LANES] + b_ref[...]
        lf = jnp.minimum(z, 0.0) - jnp.log1p(jnp.exp(-jnp.abs(z)))
        sh = 1
        while sh < LANES:
            lf = lf + jnp.where(lane >= sh, pltpu.roll(lf, sh, 1), 0.0)
            sh *= 2
        lf = lf + carry
        c_ref[:, ch * LANES:(ch + 1) * LANES] = lf
        carry = lf[:, LANES - 1:LANES]


def _fcum(f_rows, b_rows):
    rows, S = f_rows.shape
    return pl.pallas_call(
        _fcum_kernel,
        out_shape=jax.ShapeDtypeStruct((rows, S), F32),
        compiler_params=pltpu.CompilerParams(vmem_limit_bytes=VMEM_LIMIT),
        name="forget_cumsum",
    )(f_rows, b_rows)


def _attn_kernel(q_ref, kt_ref, v_ref, c_ref, o_ref):
    T = T_ATT
    H = T // 2
    qi = pl.program_id(2)
    qs = [q_ref[0, hh] for hh in range(2)]

    def update(carry, q_pair, key_start, n_keys, diag):
        out = []
        for hh in range(2):
            m, acc = carry[2 * hh], carry[2 * hh + 1]
            kt = kt_ref[0, hh, :, pl.ds(key_start, n_keys)]
            v = v_ref[0, hh, pl.ds(key_start, n_keys), :]
            s = jnp.dot(q_pair[hh], kt, preferred_element_type=F32)
            s = s - c_ref[0, hh, :, pl.ds(key_start, n_keys)]
            if diag is not None:
                delta = lax.broadcasted_iota(I32, s.shape, 1) - lax.broadcasted_iota(I32, s.shape, 0)
                s = jnp.where(delta <= diag, s, -jnp.inf)
            m_new = jnp.maximum(m, jnp.max(s, axis=-1, keepdims=True))
            p = jnp.exp(s - m_new)
            acc = jnp.exp(m - m_new) * acc + jnp.dot(p.astype(BF16), v, preferred_element_type=F32)
            out += [m_new, acc]
        return tuple(out)

    m0 = jnp.full((T, 1), -jnp.inf, F32)
    acc0 = jnp.zeros((T, V_PAD), F32)
    carry = lax.fori_loop(
        0, qi, lambda j, c: update(c, qs, pl.multiple_of(j * T, T), T, None), (m0, acc0, m0, acc0))
    diag_start = pl.multiple_of(qi * T, T)
    halves = []
    for half in range(2):
        rows = slice(half * H, (half + 1) * H)
        halves.append(update(tuple(x[rows] for x in carry), [q[rows] for q in qs],
                             diag_start, (half + 1) * H, half * H))
    for hh in range(2):
        acc = jnp.concatenate([halves[0][2 * hh + 1], halves[1][2 * hh + 1]], axis=0)
        out = acc[:, :ATT_HEAD_DIM] / acc[:, ATT_HEAD_DIM:ATT_HEAD_DIM + 1]
        o_ref[0, :, hh * ATT_HEAD_DIM:(hh + 1) * ATT_HEAD_DIM] = out.astype(BF16)


def _attention(q, k, v, c, B, S):
    nq = S // T_ATT
    qspec = pl.BlockSpec((1, 2, T_ATT, ATT_HEAD_DIM), lambda b, hp, i: (b, hp, i, 0))
    kspec = pl.BlockSpec((1, 2, ATT_HEAD_DIM, S), lambda b, hp, i: (b, hp, 0, 0))
    vspec = pl.BlockSpec((1, 2, S, V_PAD), lambda b, hp, i: (b, hp, 0, 0))
    return pl.pallas_call(
        _attn_kernel,
        grid=(B, ATT_HEADS // 2, nq),
        in_specs=[qspec, kspec, vspec,
                  pl.BlockSpec((1, 2, 1, S), lambda b, hp, i: (b, hp, 0, 0))],
        out_specs=pl.BlockSpec((1, T_ATT, 2 * ATT_HEAD_DIM), lambda b, hp, i: (b, i, hp)),
        out_shape=jax.ShapeDtypeStruct((B, S, ATT_WIDTH), BF16),
        compiler_params=_cparams(("parallel", "parallel", "arbitrary")),
        name="fox_attention",
    )(q, k, v, c)


def _pool_kernel(u_ref, w_ref, sc_ref, o_ref, pad_ref):
    S = u_ref.shape[1]
    R = TM_POOL
    pad_ref[0:MAX_WINDOW, :] = jnp.zeros((MAX_WINDOW, POOL_WIDTH), F32)
    pad_ref[MAX_WINDOW:, :] = u_ref[0]
    lane = lax.broadcasted_iota(I32, (R, POOL_WIDTH), 1)
    trow = lax.broadcasted_iota(I32, (R, POOL_WIDTH), 0)
    grp = lane // POOL_GROUP_DIM
    win = jnp.where(grp == 0, POOL_WINDOWS[0],
                    jnp.where(grp == 1, POOL_WINDOWS[1], jnp.where(grp == 2, POOL_WINDOWS[2], POOL_WINDOWS[3])))
    for i in range(S // R):
        base = MAX_WINDOW + i * R
        u0 = pad_ref[base:base + R, :]
        acc = u0
        sums = {}
        for kk in range(1, MAX_WINDOW):
            acc = acc + pad_ref[base - kk:base - kk + R, :]
            if kk + 1 in POOL_WINDOWS:
                sums[kk + 1] = acc
        total = jnp.where(grp == 0, sums[POOL_WINDOWS[0]],
                          jnp.where(grp == 1, sums[POOL_WINDOWS[1]],
                                    jnp.where(grp == 2, sums[POOL_WINDOWS[2]], sums[POOL_WINDOWS[3]])))
        cnt = jnp.minimum(trow + (i * R + 1), win).astype(F32)
        mixed = total / cnt - u0
        y = jnp.dot(mixed.astype(BF16), w_ref[...], preferred_element_type=F32) * sc_ref[...]
        o_ref[0, i * R:(i + 1) * R, :] = y.astype(BF16)


def _pool(u3, w_bd, scale):
    B, S, _ = u3.shape
    return pl.pallas_call(
        _pool_kernel,
        grid=(B,),
        in_specs=[pl.BlockSpec((1, S, POOL_WIDTH), lambda b: (b, 0, 0)),
                  pl.BlockSpec((POOL_WIDTH, POOL_WIDTH), lambda b: (0, 0)),
                  pl.BlockSpec((1, POOL_WIDTH), lambda b: (0, 0))],
        out_specs=pl.BlockSpec((1, S, POOL_WIDTH), lambda b: (b, 0, 0)),
        out_shape=jax.ShapeDtypeStruct((B, S, POOL_WIDTH), BF16),
        scratch_shapes=[pltpu.VMEM((S + MAX_WINDOW, POOL_WIDTH), F32)],
        compiler_params=_cparams(("parallel",)),
        name="multiscale_pool",
    )(u3, w_bd, scale)


def _ssm_prep_kernel(lr_ref, li_ref, ldt_ref, brT_ref, biT_ref, ar_ref, ai_ref, bbr_ref, bbi_ref):
    lr = lr_ref[...]
    li = li_ref[...]
    dt = jnp.exp(ldt_ref[...])
    mag = jnp.exp(lr * dt)
    ar = mag * jnp.cos(li * dt)
    ai = mag * jnp.sin(li * dt)
    den = lr * lr + li * li
    nr = ar - 1.0
    zr = (nr * lr + ai * li) / den
    zi = (ai * lr - nr * li) / den
    ar_ref[...] = ar
    ai_ref[...] = ai
    br = brT_ref[...]
    bi = biT_ref[...]
    bbr_ref[...] = zr[:, None, :] * br - zi[:, None, :] * bi
    bbi_ref[...] = zr[:, None, :] * bi + zi[:, None, :] * br


def _ssm_prep(lr, li, log_dt, b_re, b_im):
    G, P, H = b_re.shape
    gp = jax.ShapeDtypeStruct((G, P), F32)
    ghp = jax.ShapeDtypeStruct((G, H, P), F32)
    return pl.pallas_call(
        _ssm_prep_kernel,
        out_shape=[gp, gp, ghp, ghp],
        name="s5_discretise",
    )(lr, li, log_dt.reshape(G, 1), b_re.transpose(0, 2, 1), b_im.transpose(0, 2, 1))


def _gelu_tanh(y):
    return 0.5 * y * (1.0 + jnp.tanh(GELU_C * (y + 0.044715 * (y * y * y))))


def _ssm_kernel(u_ref, bb_ref, ar_ref, ai_ref, cc_ref, d_ref, wg_ref, bg_ref, o_ref, x_scr, hr_scr, hi_scr):
    nb = hr_scr.shape[0]
    tt = u_ref.shape[0] // nb

    @pl.when(pl.program_id(0) == 0)
    def _():
        hr_scr[...] = jnp.zeros_like(hr_scr)
        hi_scr[...] = jnp.zeros_like(hi_scr)

    hr = hr_scr[...]
    hi = hi_scr[...]
    ar = ar_ref[...]
    ai = ai_ref[...]
    rows_sub = SUB_SSM * nb
    for sc in range(tt // SUB_SSM):
        rows = slice(sc * rows_sub, (sc + 1) * rows_sub)
        u = u_ref[rows, :]
        x = jnp.dot(u.astype(BF16), bb_ref[...], preferred_element_type=F32)
        for t in range(SUB_SSM):
            xr = x[t * nb:(t + 1) * nb, 0:SSM_STATES]
            xi = x[t * nb:(t + 1) * nb, SSM_STATES:2 * SSM_STATES]
            hr, hi = ar * hr - ai * hi + xr, ar * hi + ai * hr + xi
            r0 = sc * rows_sub + t * nb
            x_scr[r0:r0 + nb, 0:SSM_STATES] = hr
            x_scr[r0:r0 + nb, SSM_STATES:2 * SSM_STATES] = hi
        y = jnp.dot(x_scr[rows, :].astype(BF16), cc_ref[...], preferred_element_type=F32) + d_ref[...] * u
        y = _gelu_tanh(y)
        gl = jnp.dot(y.astype(BF16), wg_ref[...], preferred_element_type=F32) + bg_ref[...]
        o_ref[rows, :] = (y * _sigmoid(gl)).astype(BF16)
    hr_scr[...] = hr
    hi_scr[...] = hi


def _ssm(u_tm, bblk, ar_b, ai_b, cblk, dvec, wglu, bglu, B, S):
    rows = TT_SSM * B
    const = lambda shape: pl.BlockSpec(shape, lambda i: (0, 0))
    return pl.pallas_call(
        _ssm_kernel,
        grid=(S // TT_SSM,),
        in_specs=[pl.BlockSpec((rows, SSM_WIDTH), lambda i: (i, 0)),
                  const((SSM_WIDTH, 2 * SSM_STATES)),
                  const((B, SSM_STATES)), const((B, SSM_STATES)),
                  const((2 * SSM_STATES, SSM_WIDTH)),
                  const((1, SSM_WIDTH)), const((SSM_WIDTH, SSM_WIDTH)), const((1, SSM_WIDTH))],
        out_specs=pl.BlockSpec((rows, SSM_WIDTH), lambda i: (i, 0)),
        out_shape=jax.ShapeDtypeStruct((S * B, SSM_WIDTH), BF16),
        scratch_shapes=[pltpu.VMEM((rows, 2 * SSM_STATES), F32),
                        pltpu.VMEM((B, SSM_STATES), F32), pltpu.VMEM((B, SSM_STATES), F32)],
        compiler_params=_cparams(("arbitrary",)),
        name="s5_scan",
    )(u_tm, bblk, ar_b, ai_b, cblk, dvec, wglu, bglu)


def _merge_kernel(x_ref, ya_ref, yb_ref, yc_ref, wg_ref, wa_ref, wb_ref, wc_ref, wo_ref, g_ref, b_ref, o_ref):
    x = x_ref[...]
    xb = x.astype(BF16)
    merged = None
    for i, (y_ref, w_ref) in enumerate(((ya_ref, wa_ref), (yb_ref, wb_ref), (yc_ref, wc_ref))):
        gate = _sigmoid(jnp.dot(xb, wg_ref[:, i * D_MODEL:(i + 1) * D_MODEL], preferred_element_type=F32))
        term = gate * jnp.dot(y_ref[...], w_ref[...], preferred_element_type=F32)
        merged = term if merged is None else merged + term
    mix = jnp.dot(merged.astype(BF16), wo_ref[...], preferred_element_type=F32)
    o_ref[...] = _layer_norm(DEEPNORM_ALPHA * x + mix, g_ref[...], b_ref[...])


def _merge(x2, ya, yb, yc_tm, wg, wa, wb, wc, wo, g, b, B, S):
    N = B * S
    nS = S // TM_MERGE
    const = lambda shape: pl.BlockSpec(shape, lambda bb, s: (0, 0))
    row = lambda w: pl.BlockSpec((TM_MERGE, w), lambda bb, s: (bb * nS + s, 0))
    return pl.pallas_call(
        _merge_kernel,
        grid=(B, nS),
        in_specs=[row(D_MODEL), row(ATT_WIDTH), row(POOL_WIDTH),
                  pl.BlockSpec((TM_MERGE, SSM_WIDTH), lambda bb, s: (s, bb)),
                  const((D_MODEL, N_BRANCH * D_MODEL)), const((ATT_WIDTH, D_MODEL)),
                  const((POOL_WIDTH, D_MODEL)), const((SSM_WIDTH, D_MODEL)), const((D_MODEL, D_MODEL)),
                  const((1, D_MODEL)), const((1, D_MODEL))],
        out_specs=row(D_MODEL),
        out_shape=jax.ShapeDtypeStruct((N, D_MODEL), F32),
        compiler_params=_cparams(("parallel", "parallel")),
        name="merge_ln1",
    )(x2, ya, yb, yc_tm, wg, wa, wb, wc, wo, g, b)


def _route_kernel(x_ref, wr_ref, br_ref, gate_ref, slot_ref, slott_ref, cnt_ref):
    tm = x_ref.shape[0]
    x = x_ref[...]
    xh = x.astype(BF16)
    xl = (x - xh.astype(F32)).astype(BF16)
    w = wr_ref[...]
    wh = w.astype(BF16)
    wl = (w - wh.astype(F32)).astype(BF16)
    logits = (jnp.dot(xh, wh, preferred_element_type=F32) + jnp.dot(xl, wh, preferred_element_type=F32)
              + jnp.dot(xh, wl, preferred_element_type=F32)) + br_ref[...]
    work = logits.T[0:N_EXPERTS, :]
    sub = lax.broadcasted_iota(I32, (N_EXPERTS, tm), 0).astype(F32)
    vals, idxs = [], []
    multihot = jnp.zeros((N_EXPERTS, tm), F32)
    for _ in range(TOP_K):
        mx = jnp.max(work, axis=0, keepdims=True)
        ix = jnp.min(jnp.where(work == mx, sub, float(N_EXPERTS)), axis=0, keepdims=True)
        sel = sub == ix
        work = jnp.where(sel, -jnp.inf, work)
        multihot = multihot + jnp.where(sel, 1.0, 0.0)
        vals.append(mx)
        idxs.append(ix)
    exps = [jnp.exp(v - vals[0]) for v in vals]
    denom = exps[0] + exps[1] + exps[2] + exps[3]
    r = lax.broadcasted_iota(I32, (tm, tm), 0)
    c = lax.broadcasted_iota(I32, (tm, tm), 1)
    earlier = jnp.where(r < c, 1.0, 0.0).astype(BF16)
    before = jnp.dot(multihot.astype(BF16), earlier, preferred_element_type=F32)
    n = jnp.sum(multihot, axis=1, keepdims=True)
    n_pad = jnp.floor((n + float(SUBLANES - 1)) * (1.0 / SUBLANES)) * float(SUBLANES)
    n_pad_b = jnp.broadcast_to(n_pad, (N_EXPERTS, LANES))
    er = lax.broadcasted_iota(I32, (N_EXPERTS, N_EXPERTS), 0)
    ec = lax.broadcasted_iota(I32, (N_EXPERTS, N_EXPERTS), 1)
    below = jnp.where(ec < er, 1.0, 0.0).astype(BF16)
    run_start = jnp.dot(below, n_pad_b.astype(BF16), preferred_element_type=F32)[:, 0:1]
    where_in_tile = before + run_start
    slots = [jnp.sum(jnp.where(sub == idxs[kk], where_in_tile, 0.0), axis=0, keepdims=True) for kk in range(TOP_K)]
    gates = [exps[kk] / denom for kk in range(TOP_K)]
    sub8 = lax.broadcasted_iota(I32, (SUBLANES, tm), 0)
    slot8 = jnp.zeros((SUBLANES, tm), F32)
    gate8 = jnp.zeros((SUBLANES, tm), F32)
    for kk in range(TOP_K):
        slot8 = jnp.where(sub8 == kk, slots[kk], slot8)
        gate8 = jnp.where(sub8 == kk, gates[kk], gate8)
    slott_ref[0] = slot8.astype(I32)
    pad_rows = jnp.zeros((LANES - SUBLANES, tm), F32)
    slot_ref[...] = jnp.concatenate([slot8, pad_rows], axis=0).T.astype(I32)
    gate_ref[...] = jnp.concatenate([gate8, pad_rows], axis=0).T
    cnt_ref[0] = n_pad_b.astype(I32)


def _route(x1, wr, br):
    N = x1.shape[0]
    nt = N // TM_ROW
    row = pl.BlockSpec((TM_ROW, LANES), lambda i: (i, 0))
    return pl.pallas_call(
        _route_kernel,
        grid=(nt,),
        in_specs=[pl.BlockSpec((TM_ROW, D_MODEL), lambda i: (i, 0)),
                  pl.BlockSpec((D_MODEL, LANES), lambda i: (0, 0)),
                  pl.BlockSpec((1, LANES), lambda i: (0, 0))],
        out_specs=[row, row,
                   pl.BlockSpec((1, SUBLANES, TM_ROW), lambda i: (i, 0, 0)),
                   pl.BlockSpec((1, N_EXPERTS, LANES), lambda i: (i, 0, 0))],
        out_shape=[jax.ShapeDtypeStruct((N, LANES), F32), jax.ShapeDtypeStruct((N, LANES), I32),
                   jax.ShapeDtypeStruct((nt, SUBLANES, TM_ROW), I32),
                   jax.ShapeDtypeStruct((nt, N_EXPERTS, LANES), I32)],
        compiler_params=_cparams(("parallel",)),
        name="router_top4",
    )(x1, wr, br)


def _issue_runs(tab_ref, make_copy):
    def issue_one(e, _):
        rows = tab_ref[0, 0, 2 * N_EXPERTS + e]

        @pl.when(rows > 0)
        def _():
            make_copy(pl.multiple_of(tab_ref[0, 0, e], SUBLANES),
                      pl.multiple_of(tab_ref[0, 0, N_EXPERTS + e], SUBLANES),
                      pl.multiple_of(rows, SUBLANES)).start()
        return 0

    lax.fori_loop(0, N_EXPERTS, issue_one, 0)


def _wait_runs(tab_ref, make_copy):
    total = tab_ref[0, 0, 3 * N_EXPERTS]

    @pl.when(total > 0)
    def _():
        make_copy(0, 0, pl.multiple_of(total, SUBLANES)).wait()


def _dispatch_kernel(tab_ref, tab_prev_ref, gaps_ref, x_ref, slott_ref, xs_hbm, srt_scr, sem):
    tm = x_ref.shape[0]
    i = pl.program_id(0)
    last = pl.num_programs(0) - 1
    cur = lax.rem(i, 2)
    r = lax.broadcasted_iota(I32, (ROWS_TILE, tm), 0)
    hit = r == slott_ref[0, 0:1, :]
    for kk in range(1, TOP_K):
        hit = hit | (r == slott_ref[0, kk:kk + 1, :])
    sel = jnp.where(hit, 1.0, 0.0).astype(BF16)
    srt = jnp.dot(sel, x_ref[...].astype(BF16), preferred_element_type=F32)
    srt_scr[cur] = _pack_rows(srt)

    def out_copy(buf):
        return lambda local, remote, rows: pltpu.make_async_copy(
            srt_scr.at[buf, pl.ds(local, rows)], xs_hbm.at[pl.ds(remote, rows)], sem.at[buf])

    _issue_runs(tab_ref, out_copy(cur))

    @pl.when(i > 0)
    def _():
        _wait_runs(tab_prev_ref, out_copy(1 - cur))

    @pl.when(i == last)
    def _():
        _wait_runs(tab_ref, out_copy(cur))
        srt_scr[0, 0:BM_EXP, :] = jnp.zeros((BM_EXP, HALF_D), U32)
        tail = pl.multiple_of(gaps_ref[0, 2 * N_EXPERTS], BM_EXP)
        n_tail = (xs_hbm.shape[0] - tail) // BM_EXP

        def zero_copy(first_row, rows):
            return pltpu.make_async_copy(srt_scr.at[0, pl.ds(0, rows)], xs_hbm.at[pl.ds(first_row, rows)], sem.at[0])

        def gap(e, op):
            rows = gaps_ref[0, N_EXPERTS + e]

            @pl.when(rows > 0)
            def _():
                op(zero_copy(pl.multiple_of(gaps_ref[0, e], SUBLANES), pl.multiple_of(rows, SUBLANES)))

        def tail_block(b, op):
            op(zero_copy(pl.multiple_of(tail + b * BM_EXP, BM_EXP), BM_EXP))

        def for_each(count, body, op):
            def step(k, c):
                body(k, op)
                return c
            lax.fori_loop(0, count, step, 0)

        for op in (lambda cp: cp.start(), lambda cp: cp.wait()):
            for_each(N_EXPERTS, gap, op)
            for_each(n_tail, tail_block, op)


def _dispatch(tab, gaps, x1, slott, m_pad):
    N = x1.shape[0]
    return pl.pallas_call(
        _dispatch_kernel,
        grid=(N // TM_ROW,),
        in_specs=[pl.BlockSpec((1, 1, TAB_W), lambda i: (i, 0, 0), memory_space=pltpu.SMEM),
                  pl.BlockSpec((1, 1, TAB_W), lambda i: (jnp.maximum(i - 1, 0), 0, 0), memory_space=pltpu.SMEM),
                  pl.BlockSpec((1, TAB_W), lambda i: (0, 0), memory_space=pltpu.SMEM),
                  pl.BlockSpec((TM_ROW, D_MODEL), lambda i: (i, 0)),
                  pl.BlockSpec((1, SUBLANES, TM_ROW), lambda i: (i, 0, 0))],
        out_specs=pl.BlockSpec(memory_space=pl.ANY),
        out_shape=jax.ShapeDtypeStruct((m_pad, HALF_D), U32),
        scratch_shapes=[pltpu.VMEM((2, ROWS_TILE, HALF_D), U32), pltpu.SemaphoreType.DMA((2,))],
        compiler_params=_cparams(("arbitrary",), has_side_effects=True),
        name="moe_dispatch",
    )(tab, tab, gaps, x1, slott)


def _expert_kernel(exp_ref, used_ref, new_ref, xs_ref, wu_ref, bu_ref, wd_ref, bd_ref, perm_ref,
                   o_ref, wu_scr, wd_scr):
    w = pl.program_id(0)

    @pl.when(used_ref[w] == 0)
    def _():
        o_ref[...] = jnp.zeros_like(o_ref)

    @pl.when(new_ref[w] == 1)
    def _():
        for cc in range(2 * D_FF // (2 * LANES)):
            cols = slice(cc * 2 * LANES, (cc + 1) * 2 * LANES)
            wu_scr[:, cols] = jnp.dot(wu_ref[0, 0, :, cols].astype(BF16), perm_ref[...],
                                      preferred_element_type=F32).astype(BF16)
        wd_scr[...] = wd_ref[0, 0].astype(BF16)

    @pl.when(used_ref[w] == 1)
    def _():
        xh, xl = _unpack_rows(xs_ref[...])
        x = jnp.concatenate([xh.astype(BF16), xl.astype(BF16)], axis=1)
        hu = jnp.dot(x, wu_scr[...], preferred_element_type=F32) + bu_ref[0]
        chunks = []
        for cc in range(D_FF // LANES):
            g = jnp.minimum(hu[:, 2 * cc * LANES:(2 * cc + 1) * LANES], SWIGLU_LIMIT)
            l = jnp.clip(hu[:, (2 * cc + 1) * LANES:(2 * cc + 2) * LANES], -SWIGLU_LIMIT, SWIGLU_LIMIT)
            chunks.append(g * _sigmoid(SWIGLU_ALPHA * g) * (l + 1.0))
        act = jnp.concatenate(chunks, axis=1)
        y = jnp.dot(act.astype(BF16), wd_scr[...], preferred_element_type=F32) + bd_ref[0]
        o_ref[...] = _pack_rows(y)


def _pair_perm():
    p = np.zeros((2 * LANES, 2 * LANES), np.float32)
    j = np.arange(LANES)
    p[2 * j, j] = 1.0
    p[2 * j + 1, LANES + j] = 1.0
    return jnp.asarray(p, BF16)


def _experts(meta, xs, w_up, bu, w_down, bd, layer, perm):
    M = xs.shape[0]
    by_expert = lambda *shape: pl.BlockSpec((1,) + shape, lambda w, ex, used, new: (ex[w], 0, 0))
    stacked = lambda *shape: pl.BlockSpec((1, 1) + shape, lambda w, ex, used, new: (layer, ex[w], 0, 0))
    rows = pl.BlockSpec((BM_EXP, HALF_D), lambda w, ex, used, new: (w, 0))
    return pl.pallas_call(
        _expert_kernel,
        grid_spec=pltpu.PrefetchScalarGridSpec(
            num_scalar_prefetch=3,
            grid=(M // BM_EXP,),
            in_specs=[rows, stacked(D_MODEL, 2 * D_FF), by_expert(1, 2 * D_FF),
                      stacked(D_FF, D_MODEL), by_expert(1, D_MODEL),
                      pl.BlockSpec((2 * LANES, 2 * LANES), lambda w, ex, used, new: (0, 0))],
            out_specs=rows,
            scratch_shapes=[pltpu.VMEM((D_MODEL, 2 * D_FF), BF16), pltpu.VMEM((D_FF, D_MODEL), BF16)],
        ),
        out_shape=jax.ShapeDtypeStruct((M, HALF_D), U32),
        compiler_params=_cparams(("arbitrary",)),
        name="moe_experts",
    )(*meta, xs, w_up, bu, w_down, bd, perm)


def _combine_kernel(tab_ref, tab_next_ref, x_ref, gate_ref, slot_ref, g_ref, b_ref, yb_hbm, o_ref, buf, sem):
    tm = x_ref.shape[0]
    i = pl.program_id(0)
    cur = lax.rem(i, 2)

    def in_copy(b):
        return lambda local, remote, rows: pltpu.make_async_copy(
            yb_hbm.at[pl.ds(remote, rows)], buf.at[b, pl.ds(local, rows)], sem.at[b])

    @pl.when(i == 0)
    def _():
        buf[...] = jnp.zeros_like(buf)
        _issue_runs(tab_ref, in_copy(cur))

    @pl.when(i + 1 < pl.num_programs(0))
    def _():
        _issue_runs(tab_next_ref, in_copy(1 - cur))

    r = lax.broadcasted_iota(I32, (tm, ROWS_TILE), 1)
    slots = slot_ref[...]
    gates = gate_ref[...]
    pw = jnp.zeros((tm, ROWS_TILE), F32)
    for kk in range(TOP_K):
        pw = jnp.where(r == slots[:, kk:kk + 1], gates[:, kk:kk + 1], pw)
    pw = pw.astype(BF16)
    _wait_runs(tab_ref, in_copy(cur))
    yh, yl = _unpack_rows(buf[cur])
    ys = jnp.concatenate([yh.astype(BF16), yl.astype(BF16)], axis=1)
    y = jnp.dot(pw, ys, preferred_element_type=F32)
    o_ref[...] = _layer_norm(DEEPNORM_ALPHA * x_ref[...] + y, g_ref[...], b_ref[...])


def _combine(tab, x1, gate, slot, g, b, yb):
    N = x1.shape[0]
    nt = N // TM_ROW
    return pl.pallas_call(
        _combine_kernel,
        grid=(N // TM_ROW,),
        in_specs=[pl.BlockSpec((1, 1, TAB_W), lambda i: (i, 0, 0), memory_space=pltpu.SMEM),
                  pl.BlockSpec((1, 1, TAB_W), lambda i: (jnp.minimum(i + 1, nt - 1), 0, 0), memory_space=pltpu.SMEM),
                  pl.BlockSpec((TM_ROW, D_MODEL), lambda i: (i, 0)),
                  pl.BlockSpec((TM_ROW, LANES), lambda i: (i, 0)),
                  pl.BlockSpec((TM_ROW, LANES), lambda i: (i, 0)),
                  pl.BlockSpec((1, D_MODEL), lambda i: (0, 0)),
                  pl.BlockSpec((1, D_MODEL), lambda i: (0, 0)),
                  pl.BlockSpec(memory_space=pl.ANY)],
        out_specs=pl.BlockSpec((TM_ROW, D_MODEL), lambda i: (i, 0)),
        out_shape=jax.ShapeDtypeStruct((N, D_MODEL), F32),
        scratch_shapes=[pltpu.VMEM((2, ROWS_TILE, HALF_D), U32), pltpu.SemaphoreType.DMA((2,))],
        compiler_params=_cparams(("arbitrary",)),
        name="moe_combine_ln2",
    )(tab, tab, x1, gate, slot, g, b, yb)


def _moe_tables(cnt, m_pad):
    n_pad = cnt[:, :, 0]
    rows_e = n_pad.sum(axis=0)
    meta, region, blocks_used = _work_items(rows_e, m_pad)
    run_end = jnp.cumsum(n_pad, axis=1)
    run_start = run_end - n_pad
    hbm_start = region[None, :] + jnp.cumsum(n_pad, axis=0) - n_pad
    total = run_end[:, -1:]
    pad = jnp.zeros((n_pad.shape[0], TAB_W - 3 * N_EXPERTS - 1), I32)
    tab = jnp.concatenate([run_start, hbm_start, n_pad, total, pad], axis=1).astype(I32)
    gap_rows = (rows_e + BM_EXP - 1) // BM_EXP * BM_EXP - rows_e
    gaps = jnp.concatenate([region + rows_e, gap_rows, (blocks_used * BM_EXP)[None],
                            jnp.zeros((TAB_W - 2 * N_EXPERTS - 1,), I32)]).astype(I32)
    return tab[:, None, :], gaps[None, :], meta


def _work_items(counts, M):
    nblk = M // BM_EXP
    n_items = (counts + BM_EXP - 1) // BM_EXP
    item_end = jnp.cumsum(n_items)
    item_start = item_end - n_items
    total = item_end[-1]
    w = jnp.arange(nblk, dtype=I32)
    w_eff = jnp.minimum(w, total - 1)
    owner = (item_end[None, :] <= w_eff[:, None]).sum(axis=1).astype(I32)
    sel = owner[:, None] == jnp.arange(N_EXPERTS, dtype=I32)[None, :]
    first_of_owner = jnp.sum(jnp.where(sel, item_start[None, :], 0), axis=1)
    used = w < total
    new = used & (w == first_of_owner)
    return (owner, used.astype(I32), new.astype(I32)), item_start * BM_EXP, total


def _block_diag(blocks):
    G, a, b = blocks.shape
    eye = jnp.eye(G, dtype=blocks.dtype)
    return jnp.einsum("gab,gh->gahb", blocks, eye).reshape(G * a, G * b)


def kernel(x, w_in, b_forget, w_pool, pool_scale, ssm_lambda_re, ssm_lambda_im, ssm_log_dt, ssm_b_re, ssm_b_im, ssm_c_re, ssm_c_im, ssm_d, w_glu, b_glu, w_branch_a, w_branch_b, w_branch_c, w_out, ln1_g, ln1_b, w_router, b_router, w_up, b_up, w_down, b_down, ln2_g, ln2_b):
    B, S, D = x.shape
    assert D == D_MODEL and S % T_ATT == 0 and S % TM_PROJ == 0
    N = B * S
    m_pad = (N // TM_ROW) * ROWS_TILE + N_EXPERTS * BM_EXP
    assert m_pad % BM_EXP == 0
    perm = _pair_perm()
    x2 = x.reshape(N, D)
    scale = ATT_HEAD_DIM ** -0.5
    for l in range(DEPTH):
        wl = w_in[l]
        wq = wl[:, :ATT_WIDTH] * scale
        wqkv = jnp.concatenate([wq, wl[:, ATT_WIDTH:QKV_W]], axis=1).astype(BF16)
        c0 = QKV_W
        wf = wl[:, c0:c0 + ATT_HEADS]
        wsmall = jnp.concatenate(
            [wl[:, c0 + ATT_HEADS:c0 + ATT_HEADS + POOL_WIDTH + SSM_WIDTH], wf,
             jnp.zeros((D, F_PAD - ATT_HEADS), F32)], axis=1).astype(BF16)
        wgates = wl[:, c0 + ATT_HEADS + POOL_WIDTH + SSM_WIDTH:].astype(BF16)

        wkt = wl[:, ATT_WIDTH:2 * ATT_WIDTH].T.astype(BF16)
        q, k, v, u_pool, u_ssm_tm, f_pad = _proj(x2, wqkv, wkt, wsmall, B, S)

        f_rows = f_pad[:, :ATT_HEADS].reshape(B, S, ATT_HEADS).transpose(0, 2, 1).reshape(B * ATT_HEADS, S)
        b_rows = jnp.tile(b_forget[l], B).reshape(B * ATT_HEADS, 1)
        c = _fcum(f_rows, b_rows).reshape(B, ATT_HEADS, 1, S)
        y_a = _attention(q, k, v, c, B, S).reshape(N, ATT_WIDTH)

        w_pool_bd = _block_diag(w_pool[l]).astype(BF16)
        y_b = _pool(u_pool.reshape(B, S, POOL_WIDTH), w_pool_bd, pool_scale[l].reshape(1, POOL_WIDTH)).reshape(N, POOL_WIDTH)

        ar, ai, bbrT, bbiT = _ssm_prep(ssm_lambda_re[l], ssm_lambda_im[l], ssm_log_dt[l], ssm_b_re[l], ssm_b_im[l])
        bblk = jnp.concatenate([_block_diag(bbrT), _block_diag(bbiT)], axis=1).astype(BF16)
        cblk = jnp.concatenate([_block_diag(ssm_c_re[l].transpose(0, 2, 1)),
                                -_block_diag(ssm_c_im[l].transpose(0, 2, 1))], axis=0).astype(BF16)
        ar_b = jnp.broadcast_to(ar.reshape(1, SSM_STATES), (B, SSM_STATES))
        ai_b = jnp.broadcast_to(ai.reshape(1, SSM_STATES), (B, SSM_STATES))
        y_c_tm = _ssm(u_ssm_tm.reshape(S * B, SSM_WIDTH), bblk, ar_b, ai_b, cblk,
                      ssm_d[l].reshape(1, SSM_WIDTH), w_glu[l].astype(BF16), b_glu[l].reshape(1, SSM_WIDTH), B, S)

        x1 = _merge(x2, y_a, y_b, y_c_tm.reshape(S, B * SSM_WIDTH), wgates,
                    w_branch_a[l].astype(BF16), w_branch_b[l].astype(BF16), w_branch_c[l].astype(BF16),
                    w_out[l].astype(BF16), ln1_g[l].reshape(1, D), ln1_b[l].reshape(1, D), B, S)

        wr = jnp.concatenate([w_router[l], jnp.zeros((D, LANES - N_EXPERTS), F32)], axis=1)
        br = jnp.concatenate([b_router[l], jnp.full((LANES - N_EXPERTS,), _NEG_BIG, F32)]).reshape(1, LANES)
        gate, slot, slott, cnt = _route(x1, wr, br)
        tab, gaps, meta = _moe_tables(cnt, m_pad)
        xs = _dispatch(tab, gaps, x1, slott, m_pad)
        bu = b_up[l].reshape(N_EXPERTS, D_FF // LANES, LANES, 2).transpose(0, 1, 3, 2).reshape(N_EXPERTS, 1, 2 * D_FF)
        yb = _experts(meta, xs, w_up, bu, w_down, b_down[l][:, None, :], l, perm)
        x2 = _combine(tab, x1, gate, slot, ln2_g[l].reshape(1, D), ln2_b[l].reshape(1, D), yb)
    return x2.reshape(B, S, D)
```

```python
import math

import jax
import jax.numpy as jnp
import numpy as np
from jax import lax
from jax.experimental import pallas as pl
from jax.experimental.pallas import tpu as pltpu

F32 = jnp.float32
BF16 = jnp.bfloat16
I32 = jnp.int32
U32 = jnp.uint32

D_MODEL = 1024
DEPTH = 4
ATT_HEADS = 8
ATT_HEAD_DIM = 64
ATT_WIDTH = ATT_HEADS * ATT_HEAD_DIM
POOL_WINDOWS = (2, 4, 8, 16)
POOL_GROUPS = 4
POOL_WIDTH = 256
POOL_GROUP_DIM = 64
MAX_WINDOW = max(POOL_WINDOWS)
SSM_WIDTH = 256
SSM_GROUP_DIM = 16
SSM_GROUPS = 16
SSM_STATE = 64
SSM_STATES = SSM_GROUPS * SSM_STATE
N_BRANCH = 3
N_EXPERTS = 32
TOP_K = 4
D_FF = D_MODEL
SWIGLU_LIMIT = 7.0
SWIGLU_ALPHA = 1.702
LN_EPS = 1e-5
DEEPNORM_ALPHA = (2.0 * DEPTH) ** 0.25
GELU_C = math.sqrt(2.0 / math.pi)

LANES = 128
HALF_D = D_MODEL // 2
F_PAD = LANES
V_PAD = LANES
SMALL_W = POOL_WIDTH + SSM_WIDTH + F_PAD
TM_PROJ = 512
T_ATT = 1024
TM_POOL = 256
TT_SSM = 128
SUB_SSM = 16
TM_MERGE = 512
SUBLANES = 8
TM_ROW = 256
ROWS_TILE = TM_ROW * TOP_K + N_EXPERTS * SUBLANES
TAB_W = LANES
assert TAB_W >= 3 * N_EXPERTS + 2
BM_EXP = 512
VMEM_LIMIT = 52 * 1024 * 1024

_NEG_BIG = -1e30


def _cparams(sem, **kw):
    return pltpu.CompilerParams(dimension_semantics=sem, vmem_limit_bytes=VMEM_LIMIT, **kw)


def _sigmoid(x):
    return 1.0 / (1.0 + jnp.exp(-x))


def _layer_norm(z, g, b):
    mu = jnp.mean(z, axis=-1, keepdims=True)
    zc = z - mu
    var = jnp.mean(zc * zc, axis=-1, keepdims=True)
    return zc * lax.rsqrt(var + LN_EPS) * g + b


def _pack_rows(y):
    u = pltpu.bitcast(y.astype(BF16).astype(F32), U32)
    return u[:, :HALF_D] | (u[:, HALF_D:] >> 16)


def _unpack_rows(p):
    hi = pltpu.bitcast(p & jnp.uint32(0xFFFF0000), F32)
    lo = pltpu.bitcast(p << 16, F32)
    return hi, lo


def _proj_kernel(x_ref, wqv_ref, wkt_ref, ws_ref, q_ref, kt_ref, v_ref, up_ref, us_ref, f_ref):
    xb = x_ref[...].astype(BF16)
    h = jnp.dot(xb, wqv_ref[:, :ATT_WIDTH], preferred_element_type=F32)
    for hh in range(ATT_HEADS):
        q_ref[0, hh] = h[:, hh * ATT_HEAD_DIM:(hh + 1) * ATT_HEAD_DIM].astype(BF16)
    ht = lax.dot_general(wkt_ref[...], xb, (((1,), (1,)), ((), ())), preferred_element_type=F32)
    for hh in range(ATT_HEADS):
        kt_ref[0, hh] = ht[hh * ATT_HEAD_DIM:(hh + 1) * ATT_HEAD_DIM, :].astype(BF16)
    h = jnp.dot(xb, wqv_ref[:, ATT_WIDTH:], preferred_element_type=F32)
    lane = lax.broadcasted_iota(I32, (h.shape[0], V_PAD), 1)
    tail = jnp.where(lane == ATT_HEAD_DIM, 1.0, 0.0)
    for pair in range(ATT_HEADS // 2):
        slab = h[:, pair * V_PAD:(pair + 1) * V_PAD]
        v_ref[0, 2 * pair] = jnp.where(lane < ATT_HEAD_DIM, slab, tail).astype(BF16)
        v_ref[0, 2 * pair + 1] = jnp.where(lane < ATT_HEAD_DIM, pltpu.roll(slab, ATT_HEAD_DIM, 1), tail).astype(BF16)
    hs = jnp.dot(xb, ws_ref[...], preferred_element_type=F32)
    up_ref[...] = hs[:, :POOL_WIDTH]
    us_ref[...] = hs[:, POOL_WIDTH:POOL_WIDTH + SSM_WIDTH]
    f_ref[...] = hs[:, POOL_WIDTH + SSM_WIDTH:]


def _proj(x2, wqv, wkt, wsmall, B, S):
    N = B * S
    nS = S // TM_PROJ
    hm = jax.ShapeDtypeStruct((B, ATT_HEADS, S, ATT_HEAD_DIM), BF16)
    hm_spec = pl.BlockSpec((1, ATT_HEADS, TM_PROJ, ATT_HEAD_DIM), lambda b, s: (b, 0, s, 0))
    ht = jax.ShapeDtypeStruct((B, ATT_HEADS, ATT_HEAD_DIM, S), BF16)
    ht_spec = pl.BlockSpec((1, ATT_HEADS, ATT_HEAD_DIM, TM_PROJ), lambda b, s: (b, 0, 0, s))
    hv = jax.ShapeDtypeStruct((B, ATT_HEADS, S, V_PAD), BF16)
    hv_spec = pl.BlockSpec((1, ATT_HEADS, TM_PROJ, V_PAD), lambda b, s: (b, 0, s, 0))
    return pl.pallas_call(
        _proj_kernel,
        grid=(B, nS),
        in_specs=[
            pl.BlockSpec((TM_PROJ, D_MODEL), lambda b, s: (b * nS + s, 0)),
            pl.BlockSpec((D_MODEL, 2 * ATT_WIDTH), lambda b, s: (0, 0)),
            pl.BlockSpec((ATT_WIDTH, D_MODEL), lambda b, s: (0, 0)),
            pl.BlockSpec((D_MODEL, SMALL_W), lambda b, s: (0, 0)),
        ],
        out_specs=[
            hm_spec, ht_spec, hv_spec,
            pl.BlockSpec((TM_PROJ, POOL_WIDTH), lambda b, s: (b * nS + s, 0)),
            pl.BlockSpec((TM_PROJ, SSM_WIDTH), lambda b, s: (s, b)),
            pl.BlockSpec((TM_PROJ, F_PAD), lambda b, s: (b * nS + s, 0)),
        ],
        out_shape=[
            hm, ht, hv,
            jax.ShapeDtypeStruct((N, POOL_WIDTH), F32),
            jax.ShapeDtypeStruct((S, B * SSM_WIDTH), F32),
            jax.ShapeDtypeStruct((N, F_PAD), F32),
        ],
        compiler_params=_cparams(("parallel", "parallel")),
        name="in_proj",
    )(x2, wqv, wkt, wsmall)


def _fcum_kernel(f_ref, b_ref, c_ref):
    rows, S = f_ref.shape
    lane = lax.broadcasted_iota(I32, (rows, LANES), 1)
    carry = jnp.zeros((rows, 1), F32)
    for ch in range(S // LANES):
        z = f_ref[:, ch * LANES:(ch + 1) * LANES] + b_ref[...]
        lf = jnp.minimum(z, 0.0) - jnp.log1p(jnp.exp(-jnp.abs(z)))
        sh = 1
        while sh < LANES:
            lf = lf + jnp.where(lane >= sh, pltpu.roll(lf, sh, 1), 0.0)
            sh *= 2
        lf = lf + carry
        c_ref[:, ch * LANES:(ch + 1) * LANES] = lf
        carry = lf[:, LANES - 1:LANES]


def _fcum(f_rows, b_rows):
    rows, S = f_rows.shape
    return pl.pallas_call(
        _fcum_kernel,
        out_shape=jax.ShapeDtypeStruct((rows, S), F32),
        compiler_params=pltpu.CompilerParams(vmem_limit_bytes=VMEM_LIMIT),
        name="forget_cumsum",
    )(f_rows, b_rows)


def _attn_kernel(q_ref, kt_ref, v_ref, c_ref, o_ref):
    T = T_ATT
    H = T // 2
    qi = pl.program_id(2)
    qs = [q_ref[0, hh] for hh in range(2)]

    def update(carry, q_pair, key_start, n_keys, diag):
        out = []
        for hh in range(2):
            m, acc = carry[2 * hh], carry[2 * hh + 1]
            kt = kt_ref[0, hh, :, pl.ds(key_start, n_keys)]
            v = v_ref[0, hh, pl.ds(key_start, n_keys), :]
            s = jnp.dot(q_pair[hh], kt, preferred_element_type=F32)
            s = s - c_ref[0, hh, :, pl.ds(key_start, n_keys)]
            if diag is not None:
                delta = lax.broadcasted_iota(I32, s.shape, 1) - lax.broadcasted_iota(I32, s.shape, 0)
                s = jnp.where(delta <= diag, s, -jnp.inf)
            m_new = jnp.maximum(m, jnp.max(s, axis=-1, keepdims=True))
            p = jnp.exp(s - m_new)
            acc = jnp.exp(m - m_new) * acc + jnp.dot(p.astype(BF16), v, preferred_element_type=F32)
            out += [m_new, acc]
        return tuple(out)

    m0 = jnp.full((T, 1), -jnp.inf, F32)
    acc0 = jnp.zeros((T, V_PAD), F32)
    carry = lax.fori_loop(
        0, qi, lambda j, c: update(c, qs, pl.multiple_of(j * T, T), T, None), (m0, acc0, m0, acc0))
    diag_start = pl.multiple_of(qi * T, T)
    halves = []
    for half in range(2):
        rows = slice(half * H, (half + 1) * H)
        halves.append(update(tuple(x[rows] for x in carry), [q[rows] for q in qs],
                             diag_start, (half + 1) * H, half * H))
    for hh in range(2):
        acc = jnp.concatenate([halves[0][2 * hh + 1], halves[1][2 * hh + 1]], axis=0)
        out = acc[:, :ATT_HEAD_DIM] / acc[:, ATT_HEAD_DIM:ATT_HEAD_DIM + 1]
        o_ref[0, :, hh * ATT_HEAD_DIM:(hh + 1) * ATT_HEAD_DIM] = out.astype(BF16)


def _attention(q, k, v, c, B, S):
    nq = S // T_ATT
    qspec = pl.BlockSpec((1, 2, T_ATT, ATT_HEAD_DIM), lambda b, hp, i: (b, hp, i, 0))
    kspec = pl.BlockSpec((1, 2, ATT_HEAD_DIM, S), lambda b, hp, i: (b, hp, 0, 0))
    vspec = pl.BlockSpec((1, 2, S, V_PAD), lambda b, hp, i: (b, hp, 0, 0))
    return pl.pallas_call(
        _attn_kernel,
        grid=(B, ATT_HEADS // 2, nq),
        in_specs=[qspec, kspec, vspec,
                  pl.BlockSpec((1, 2, 1, S), lambda b, hp, i: (b, hp, 0, 0))],
        out_specs=pl.BlockSpec((1, T_ATT, 2 * ATT_HEAD_DIM), lambda b, hp, i: (b, i, hp)),
        out_shape=jax.ShapeDtypeStruct((B, S, ATT_WIDTH), BF16),
        compiler_params=_cparams(("parallel", "parallel", "arbitrary")),
        name="fox_attention",
    )(q, k, v, c)


def _pool_kernel(u_ref, w_ref, sc_ref, o_ref, pad_ref):
    S = u_ref.shape[1]
    R = TM_POOL
    pad_ref[0:MAX_WINDOW, :] = jnp.zeros((MAX_WINDOW, POOL_WIDTH), F32)
    pad_ref[MAX_WINDOW:, :] = u_ref[0]
    lane = lax.broadcasted_iota(I32, (R, POOL_WIDTH), 1)
    trow = lax.broadcasted_iota(I32, (R, POOL_WIDTH), 0)
    grp = lane // POOL_GROUP_DIM
    win = jnp.where(grp == 0, POOL_WINDOWS[0],
                    jnp.where(grp == 1, POOL_WINDOWS[1], jnp.where(grp == 2, POOL_WINDOWS[2], POOL_WINDOWS[3])))
    for i in range(S // R):
        base = MAX_WINDOW + i * R
        u0 = pad_ref[base:base + R, :]
        acc = u0
        sums = {}
        for kk in range(1, MAX_WINDOW):
            acc = acc + pad_ref[base - kk:base - kk + R, :]
            if kk + 1 in POOL_WINDOWS:
                sums[kk + 1] = acc
        total = jnp.where(grp == 0, sums[POOL_WINDOWS[0]],
                          jnp.where(grp == 1, sums[POOL_WINDOWS[1]],
                                    jnp.where(grp == 2, sums[POOL_WINDOWS[2]], sums[POOL_WINDOWS[3]])))
        cnt = jnp.minimum(trow + (i * R + 1), win).astype(F32)
        mixed = total / cnt - u0
        y = jnp.dot(mixed.astype(BF16), w_ref[...], preferred_element_type=F32) * sc_ref[...]
        o_ref[0, i * R:(i + 1) * R, :] = y.astype(BF16)


def _pool(u3, w_bd, scale):
    B, S, _ = u3.shape
    return pl.pallas_call(
        _pool_kernel,
        grid=(B,),
        in_specs=[pl.BlockSpec((1, S, POOL_WIDTH), lambda b: (b, 0, 0)),
                  pl.BlockSpec((POOL_WIDTH, POOL_WIDTH), lambda b: (0, 0)),
                  pl.BlockSpec((1, POOL_WIDTH), lambda b: (0, 0))],
        out_specs=pl.BlockSpec((1, S, POOL_WIDTH), lambda b: (b, 0, 0)),
        out_shape=jax.ShapeDtypeStruct((B, S, POOL_WIDTH), BF16),
        scratch_shapes=[pltpu.VMEM((S + MAX_WINDOW, POOL_WIDTH), F32)],
        compiler_params=_cparams(("parallel",)),
        name="multiscale_pool",
    )(u3, w_bd, scale)


def _ssm_prep_kernel(lr_ref, li_ref, ldt_ref, brT_ref, biT_ref, ar_ref, ai_ref, bbr_ref, bbi_ref):
    lr = lr_ref[...]
    li = li_ref[...]
    dt = jnp.exp(ldt_ref[...])
    mag = jnp.exp(lr * dt)
    ar = mag * jnp.cos(li * dt)
    ai = mag * jnp.sin(li * dt)
    den = lr * lr + li * li
    nr = ar - 1.0
    zr = (nr * lr + ai * li) / den
    zi = (ai * lr - nr * li) / den
    ar_ref[...] = ar
    ai_ref[...] = ai
    br = brT_ref[...]
    bi = biT_ref[...]
    bbr_ref[...] = zr[:, None, :] * br - zi[:, None, :] * bi
    bbi_ref[...] = zr[:, None, :] * bi + zi[:, None, :] * br


def _ssm_prep(lr, li, log_dt, b_re, b_im):
    G, P, H = b_re.shape
    gp = jax.ShapeDtypeStruct((G, P), F32)
    ghp = jax.ShapeDtypeStruct((G, H, P), F32)
    return pl.pallas_call(
        _ssm_prep_kernel,
        out_shape=[gp, gp, ghp, ghp],
        name="s5_discretise",
    )(lr, li, log_dt.reshape(G, 1), b_re.transpose(0, 2, 1), b_im.transpose(0, 2, 1))


def _gelu_tanh(y):
    return 0.5 * y * (1.0 + jnp.tanh(GELU_C * (y + 0.044715 * (y * y * y))))


def _ssm_kernel(u_ref, bb_ref, ar_ref, ai_ref, cc_ref, d_ref, wg_ref, bg_ref, o_ref, x_scr, hr_scr, hi_scr):
    nb = hr_scr.shape[0]
    tt = u_ref.shape[0] // nb

    @pl.when(pl.program_id(0) == 0)
    def _():
        hr_scr[...] = jnp.zeros_like(hr_scr)
        hi_scr[...] = jnp.zeros_like(hi_scr)

    hr = hr_scr[...]
    hi = hi_scr[...]
    ar = ar_ref[...]
    ai = ai_ref[...]
    rows_sub = SUB_SSM * nb
    for sc in range(tt // SUB_SSM):
        rows = slice(sc * rows_sub, (sc + 1) * rows_sub)
        u = u_ref[rows, :]
        x = jnp.dot(u.astype(BF16), bb_ref[...], preferred_element_type=F32)
        for t in range(SUB_SSM):
            xr = x[t * nb:(t + 1) * nb, 0:SSM_STATES]
            xi = x[t * nb:(t + 1) * nb, SSM_STATES:2 * SSM_STATES]
            hr, hi = ar * hr - ai * hi + xr, ar * hi + ai * hr + xi
            r0 = sc * rows_sub + t * nb
            x_scr[r0:r0 + nb, 0:SSM_STATES] = hr
            x_scr[r0:r0 + nb, SSM_STATES:2 * SSM_STATES] = hi
        y = jnp.dot(x_scr[rows, :].astype(BF16), cc_ref[...], preferred_element_type=F32) + d_ref[...] * u
        y = _gelu_tanh(y)
        gl = jnp.dot(y.astype(BF16), wg_ref[...], preferred_element_type=F32) + bg_ref[...]
        o_ref[rows, :] = (y * _sigmoid(gl)).astype(BF16)
    hr_scr[...] = hr
    hi_scr[...] = hi


def _ssm(u_tm, bblk, ar_b, ai_b, cblk, dvec, wglu, bglu, B, S):
    rows = TT_SSM * B
    const = lambda shape: pl.BlockSpec(shape, lambda i: (0, 0))
    return pl.pallas_call(
        _ssm_kernel,
        grid=(S // TT_SSM,),
        in_specs=[pl.BlockSpec((rows, SSM_WIDTH), lambda i: (i, 0)),
                  const((SSM_WIDTH, 2 * SSM_STATES)),
                  const((B, SSM_STATES)), const((B, SSM_STATES)),
                  const((2 * SSM_STATES, SSM_WIDTH)),
                  const((1, SSM_WIDTH)), const((SSM_WIDTH, SSM_WIDTH)), const((1, SSM_WIDTH))],
        out_specs=pl.BlockSpec((rows, SSM_WIDTH), lambda i: (i, 0)),
        out_shape=jax.ShapeDtypeStruct((S * B, SSM_WIDTH), BF16),
        scratch_shapes=[pltpu.VMEM((rows, 2 * SSM_STATES), F32),
                        pltpu.VMEM((B, SSM_STATES), F32), pltpu.VMEM((B, SSM_STATES), F32)],
        compiler_params=_cparams(("arbitrary",)),
        name="s5_scan",
    )(u_tm, bblk, ar_b, ai_b, cblk, dvec, wglu, bglu)


def _merge_kernel(x_ref, ya_ref, yb_ref, yc_ref, wg_ref, wa_ref, wb_ref, wc_ref, wo_ref, g_ref, b_ref, o_ref):
    x = x_ref[...]
    xb = x.astype(BF16)
    merged = None
    for i, (y_ref, w_ref) in enumerate(((ya_ref, wa_ref), (yb_ref, wb_ref), (yc_ref, wc_ref))):
        gate = _sigmoid(jnp.dot(xb, wg_ref[:, i * D_MODEL:(i + 1) * D_MODEL], preferred_element_type=F32))
        term = gate * jnp.dot(y_ref[...], w_ref[...], preferred_element_type=F32)
        merged = term if merged is None else merged + term
    mix = jnp.dot(merged.astype(BF16), wo_ref[...], preferred_element_type=F32)
    o_ref[...] = _layer_norm(DEEPNORM_ALPHA * x + mix, g_ref[...], b_ref[...])


def _merge(x2, ya, yb, yc_tm, wg, wa, wb, wc, wo, g, b, B, S):
    N = B * S
    nS = S // TM_MERGE
    const = lambda shape: pl.BlockSpec(shape, lambda bb, s: (0, 0))
    row = lambda w: pl.BlockSpec((TM_MERGE, w), lambda bb, s: (bb * nS + s, 0))
    return pl.pallas_call(
        _merge_kernel,
        grid=(B, nS),
        in_specs=[row(D_MODEL), row(ATT_WIDTH), row(POOL_WIDTH),
                  pl.BlockSpec((TM_MERGE, SSM_WIDTH), lambda bb, s: (s, bb)),
                  const((D_MODEL, N_BRANCH * D_MODEL)), const((ATT_WIDTH, D_MODEL)),
                  const((POOL_WIDTH, D_MODEL)), const((SSM_WIDTH, D_MODEL)), const((D_MODEL, D_MODEL)),
                  const((1, D_MODEL)), const((1, D_MODEL))],
        out_specs=row(D_MODEL),
        out_shape=jax.ShapeDtypeStruct((N, D_MODEL), F32),
        compiler_params=_cparams(("parallel", "parallel")),
        name="merge_ln1",
    )(x2, ya, yb, yc_tm, wg, wa, wb, wc, wo, g, b)


def _route_kernel(x_ref, wr_ref, br_ref, gate_ref, slot_ref, slott_ref, cnt_ref):
    tm = x_ref.shape[0]
    x = x_ref[...]
    xh = x.astype(BF16)
    xl = (x - xh.astype(F32)).astype(BF16)
    w = wr_ref[...]
    wh = w.astype(BF16)
    wl = (w - wh.astype(F32)).astype(BF16)
    logits = (jnp.dot(xh, wh, preferred_element_type=F32) + jnp.dot(xl, wh, preferred_element_type=F32)
              + jnp.dot(xh, wl, preferred_element_type=F32)) + br_ref[...]
    work = logits.T[0:N_EXPERTS, :]
    sub = lax.broadcasted_iota(I32, (N_EXPERTS, tm), 0).astype(F32)
    vals, idxs = [], []
    multihot = jnp.zeros((N_EXPERTS, tm), F32)
    for _ in range(TOP_K):
        mx = jnp.max(work, axis=0, keepdims=True)
        ix = jnp.min(jnp.where(work == mx, sub, float(N_EXPERTS)), axis=0, keepdims=True)
        sel = sub == ix
        work = jnp.where(sel, -jnp.inf, work)
        multihot = multihot + jnp.where(sel, 1.0, 0.0)
        vals.append(mx)
        idxs.append(ix)
    exps = [jnp.exp(v - vals[0]) for v in vals]
    denom = exps[0] + exps[1] + exps[2] + exps[3]
    r = lax.broadcasted_iota(I32, (tm, tm), 0)
    c = lax.broadcasted_iota(I32, (tm, tm), 1)
    earlier = jnp.where(r < c, 1.0, 0.0).astype(BF16)
    before = jnp.dot(multihot.astype(BF16), earlier, preferred_element_type=F32)
    n = jnp.sum(multihot, axis=1, keepdims=True)
    n_pad = jnp.floor((n + float(SUBLANES - 1)) * (1.0 / SUBLANES)) * float(SUBLANES)
    n_pad_b = jnp.broadcast_to(n_pad, (N_EXPERTS, LANES))
    er = lax.broadcasted_iota(I32, (N_EXPERTS, N_EXPERTS), 0)
    ec = lax.broadcasted_iota(I32, (N_EXPERTS, N_EXPERTS), 1)
    below = jnp.where(ec < er, 1.0, 0.0).astype(BF16)
    run_start = jnp.dot(below, n_pad_b.astype(BF16), preferred_element_type=F32)[:, 0:1]
    where_in_tile = before + run_start
    slots = [jnp.sum(jnp.where(sub == idxs[kk], where_in_tile, 0.0), axis=0, keepdims=True) for kk in range(TOP_K)]
    gates = [exps[kk] / denom for kk in range(TOP_K)]
    sub8 = lax.broadcasted_iota(I32, (SUBLANES, tm), 0)
    slot8 = jnp.zeros((SUBLANES, tm), F32)
    gate8 = jnp.zeros((SUBLANES, tm), F32)
    for kk in range(TOP_K):
        slot8 = jnp.where(sub8 == kk, slots[kk], slot8)
        gate8 = jnp.where(sub8 == kk, gates[kk], gate8)
    slott_ref[0] = slot8.astype(I32)
    pad_rows = jnp.zeros((LANES - SUBLANES, tm), F32)
    slot_ref[...] = jnp.concatenate([slot8, pad_rows], axis=0).T.astype(I32)
    gate_ref[...] = jnp.concatenate([gate8, pad_rows], axis=0).T
    cnt_ref[0] = n_pad_b.astype(I32)


def _route(x1, wr, br):
    N = x1.shape[0]
    nt = N // TM_ROW
    row = pl.BlockSpec((TM_ROW, LANES), lambda i: (i, 0))
    return pl.pallas_call(
        _route_kernel,
        grid=(nt,),
        in_specs=[pl.BlockSpec((TM_ROW, D_MODEL), lambda i: (i, 0)),
                  pl.BlockSpec((D_MODEL, LANES), lambda i: (0, 0)),
                  pl.BlockSpec((1, LANES), lambda i: (0, 0))],
        out_specs=[row, row,
                   pl.BlockSpec((1, SUBLANES, TM_ROW), lambda i: (i, 0, 0)),
                   pl.BlockSpec((1, N_EXPERTS, LANES), lambda i: (i, 0, 0))],
        out_shape=[jax.ShapeDtypeStruct((N, LANES), F32), jax.ShapeDtypeStruct((N, LANES), I32),
                   jax.ShapeDtypeStruct((nt, SUBLANES, TM_ROW), I32),
                   jax.ShapeDtypeStruct((nt, N_EXPERTS, LANES), I32)],
        compiler_params=_cparams(("parallel",)),
        name="router_top4",
    )(x1, wr, br)


def _issue_runs(tab_ref, make_copy):
    def issue_one(e, _):
        rows = tab_ref[0, 0, 2 * N_EXPERTS + e]

        @pl.when(rows > 0)
        def _():
            make_copy(pl.multiple_of(tab_ref[0, 0, e], SUBLANES),
                      pl.multiple_of(tab_ref[0, 0, N_EXPERTS + e], SUBLANES),
                      pl.multiple_of(rows, SUBLANES)).start()
        return 0

    lax.fori_loop(0, N_EXPERTS, issue_one, 0)


def _wait_runs(tab_ref, make_copy):
    total = tab_ref[0, 0, 3 * N_EXPERTS]

    @pl.when(total > 0)
    def _():
        make_copy(0, 0, pl.multiple_of(total, SUBLANES)).wait()


def _dispatch_kernel(tab_ref, tab_prev_ref, gaps_ref, x_ref, slott_ref, xs_hbm, srt_scr, sem):
    tm = x_ref.shape[0]
    i = pl.program_id(0)
    last = pl.num_programs(0) - 1
    cur = lax.rem(i, 2)
    r = lax.broadcasted_iota(I32, (ROWS_TILE, tm), 0)
    hit = r == slott_ref[0, 0:1, :]
    for kk in range(1, TOP_K):
        hit = hit | (r == slott_ref[0, kk:kk + 1, :])
    sel = jnp.where(hit, 1.0, 0.0).astype(BF16)
    srt = jnp.dot(sel, x_ref[...].astype(BF16), preferred_element_type=F32)
    srt_scr[cur] = _pack_rows(srt)

    def out_copy(buf):
        return lambda local, remote, rows: pltpu.make_async_copy(
            srt_scr.at[buf, pl.ds(local, rows)], xs_hbm.at[pl.ds(remote, rows)], sem.at[buf])

    _issue_runs(tab_ref, out_copy(cur))

    @pl.when(i > 0)
    def _():
        _wait_runs(tab_prev_ref, out_copy(1 - cur))

    @pl.when(i == last)
    def _():
        _wait_runs(tab_ref, out_copy(cur))
        srt_scr[0, 0:BM_EXP, :] = jnp.zeros((BM_EXP, HALF_D), U32)
        tail = pl.multiple_of(gaps_ref[0, 2 * N_EXPERTS], BM_EXP)
        n_tail = (xs_hbm.shape[0] - tail) // BM_EXP

        def zero_copy(first_row, rows):
            return pltpu.make_async_copy(srt_scr.at[0, pl.ds(0, rows)], xs_hbm.at[pl.ds(first_row, rows)], sem.at[0])

        def gap(e, op):
            rows = gaps_ref[0, N_EXPERTS + e]

            @pl.when(rows > 0)
            def _():
                op(zero_copy(pl.multiple_of(gaps_ref[0, e], SUBLANES), pl.multiple_of(rows, SUBLANES)))

        def tail_block(b, op):
            op(zero_copy(pl.multiple_of(tail + b * BM_EXP, BM_EXP), BM_EXP))

        def for_each(count, body, op):
            def step(k, c):
                body(k, op)
                return c
            lax.fori_loop(0, count, step, 0)

        for op in (lambda cp: cp.start(), lambda cp: cp.wait()):
            for_each(N_EXPERTS, gap, op)
            for_each(n_tail, tail_block, op)


def _dispatch(tab, gaps, x1, slott, m_pad):
    N = x1.shape[0]
    return pl.pallas_call(
        _dispatch_kernel,
        grid=(N // TM_ROW,),
        in_specs=[pl.BlockSpec((1, 1, TAB_W), lambda i: (i, 0, 0), memory_space=pltpu.SMEM),
                  pl.BlockSpec((1, 1, TAB_W), lambda i: (jnp.maximum(i - 1, 0), 0, 0), memory_space=pltpu.SMEM),
                  pl.BlockSpec((1, TAB_W), lambda i: (0, 0), memory_space=pltpu.SMEM),
                  pl.BlockSpec((TM_ROW, D_MODEL), lambda i: (i, 0)),
                  pl.BlockSpec((1, SUBLANES, TM_ROW), lambda i: (i, 0, 0))],
        out_specs=pl.BlockSpec(memory_space=pl.ANY),
        out_shape=jax.ShapeDtypeStruct((m_pad, HALF_D), U32),
        scratch_shapes=[pltpu.VMEM((2, ROWS_TILE, HALF_D), U32), pltpu.SemaphoreType.DMA((2,))],
        compiler_params=_cparams(("arbitrary",), has_side_effects=True),
        name="moe_dispatch",
    )(tab, tab, gaps, x1, slott)


def _expert_kernel(exp_ref, used_ref, new_ref, xs_ref, wu_ref, bu_ref, wd_ref, bd_ref, perm_ref,
                   o_ref, wu_scr, wd_scr):
    w = pl.program_id(0)

    @pl.when(used_ref[w] == 0)
    def _():
        o_ref[...] = jnp.zeros_like(o_ref)

    @pl.when(new_ref[w] == 1)
    def _():
        for cc in range(2 * D_FF // (2 * LANES)):
            cols = slice(cc * 2 * LANES, (cc + 1) * 2 * LANES)
            wu_scr[:, cols] = jnp.dot(wu_ref[0, 0, :, cols].astype(BF16), perm_ref[...],
                                      preferred_element_type=F32).astype(BF16)
        wd_scr[...] = wd_ref[0, 0].astype(BF16)

    @pl.when(used_ref[w] == 1)
    def _():
        xh, xl = _unpack_rows(xs_ref[...])
        x = jnp.concatenate([xh.astype(BF16), xl.astype(BF16)], axis=1)
        hu = jnp.dot(x, wu_scr[...], preferred_element_type=F32) + bu_ref[0]
        chunks = []
        for cc in range(D_FF // LANES):
            g = jnp.minimum(hu[:, 2 * cc * LANES:(2 * cc + 1) * LANES], SWIGLU_LIMIT)
            l = jnp.clip(hu[:, (2 * cc + 1) * LANES:(2 * cc + 2) * LANES], -SWIGLU_LIMIT, SWIGLU_LIMIT)
            chunks.append(g * _sigmoid(SWIGLU_ALPHA * g) * (l + 1.0))
        act = jnp.concatenate(chunks, axis=1)
        y = jnp.dot(act.astype(BF16), wd_scr[...], preferred_element_type=F32) + bd_ref[0]
        o_ref[...] = _pack_rows(y)


def _pair_perm():
    p = np.zeros((2 * LANES, 2 * LANES), np.float32)
    j = np.arange(LANES)
    p[2 * j, j] = 1.0
    p[2 * j + 1, LANES + j] = 1.0
    return jnp.asarray(p, BF16)


def _experts(meta, xs, w_up, bu, w_down, bd, layer, perm):
    M = xs.shape[0]
    by_expert = lambda *shape: pl.BlockSpec((1,) + shape, lambda w, ex, used, new: (ex[w], 0, 0))
    stacked = lambda *shape: pl.BlockSpec((1, 1) + shape, lambda w, ex, used, new: (layer, ex[w], 0, 0))
    rows = pl.BlockSpec((BM_EXP, HALF_D), lambda w, ex, used, new: (w, 0))
    return pl.pallas_call(
        _expert_kernel,
        grid_spec=pltpu.PrefetchScalarGridSpec(
            num_scalar_prefetch=3,
            grid=(M // BM_EXP,),
            in_specs=[rows, stacked(D_MODEL, 2 * D_FF), by_expert(1, 2 * D_FF),
                      stacked(D_FF, D_MODEL), by_expert(1, D_MODEL),
                      pl.BlockSpec((2 * LANES, 2 * LANES), lambda w, ex, used, new: (0, 0))],
            out_specs=rows,
            scratch_shapes=[pltpu.VMEM((D_MODEL, 2 * D_FF), BF16), pltpu.VMEM((D_FF, D_MODEL), BF16)],
        ),
        out_shape=jax.ShapeDtypeStruct((M, HALF_D), U32),
        compiler_params=_cparams(("arbitrary",)),
        name="moe_experts",
    )(*meta, xs, w_up, bu, w_down, bd, perm)


def _combine_kernel(tab_ref, tab_next_ref, x_ref, gate_ref, slot_ref, g_ref, b_ref, yb_hbm, o_ref, buf, sem):
    tm = x_ref.shape[0]
    i = pl.program_id(0)
    cur = lax.rem(i, 2)

    def in_copy(b):
        return lambda local, remote, rows: pltpu.make_async_copy(
            yb_hbm.at[pl.ds(remote, rows)], buf.at[b, pl.ds(local, rows)], sem.at[b])

    @pl.when(i == 0)
    def _():
        buf[...] = jnp.zeros_like(buf)
        _issue_runs(tab_ref, in_copy(cur))

    @pl.when(i + 1 < pl.num_programs(0))
    def _():
        _issue_runs(tab_next_ref, in_copy(1 - cur))

    r = lax.broadcasted_iota(I32, (tm, ROWS_TILE), 1)
    slots = slot_ref[...]
    gates = gate_ref[...]
    pw = jnp.zeros((tm, ROWS_TILE), F32)
    for kk in range(TOP_K):
        pw = jnp.where(r == slots[:, kk:kk + 1], gates[:, kk:kk + 1], pw)
    pw = pw.astype(BF16)
    _wait_runs(tab_ref, in_copy(cur))
    yh, yl = _unpack_rows(buf[cur])
    ys = jnp.concatenate([yh.astype(BF16), yl.astype(BF16)], axis=1)
    y = jnp.dot(pw, ys, preferred_element_type=F32)
    o_ref[...] = _layer_norm(DEEPNORM_ALPHA * x_ref[...] + y, g_ref[...], b_ref[...])


def _combine(tab, x1, gate, slot, g, b, yb):
    N = x1.shape[0]
    nt = N // TM_ROW
    return pl.pallas_call(
        _combine_kernel,
        grid=(N // TM_ROW,),
        in_specs=[pl.BlockSpec((1, 1, TAB_W), lambda i: (i, 0, 0), memory_space=pltpu.SMEM),
                  pl.BlockSpec((1, 1, TAB_W), lambda i: (jnp.minimum(i + 1, nt - 1), 0, 0), memory_space=pltpu.SMEM),
                  pl.BlockSpec((TM_ROW, D_MODEL), lambda i: (i, 0)),
                  pl.BlockSpec((TM_ROW, LANES), lambda i: (i, 0)),
                  pl.BlockSpec((TM_ROW, LANES), lambda i: (i, 0)),
                  pl.BlockSpec((1, D_MODEL), lambda i: (0, 0)),
                  pl.BlockSpec((1, D_MODEL), lambda i: (0, 0)),
                  pl.BlockSpec(memory_space=pl.ANY)],
        out_specs=pl.BlockSpec((TM_ROW, D_MODEL), lambda i: (i, 0)),
        out_shape=jax.ShapeDtypeStruct((N, D_MODEL), F32),
        scratch_shapes=[pltpu.VMEM((2, ROWS_TILE, HALF_D), U32), pltpu.SemaphoreType.DMA((2,))],
        compiler_params=_cparams(("arbitrary",)),
        name="moe_combine_ln2",
    )(tab, tab, x1, gate, slot, g, b, yb)


def _moe_tables(cnt, m_pad):
    n_pad = cnt[:, :, 0]
    rows_e = n_pad.sum(axis=0)
    meta, region, blocks_used = _work_items(rows_e, m_pad)
    run_end = jnp.cumsum(n_pad, axis=1)
    run_start = run_end - n_pad
    hbm_start = region[None, :] + jnp.cumsum(n_pad, axis=0) - n_pad
    total = run_end[:, -1:]
    pad = jnp.zeros((n_pad.shape[0], TAB_W - 3 * N_EXPERTS - 1), I32)
    tab = jnp.concatenate([run_start, hbm_start, n_pad, total, pad], axis=1).astype(I32)
    gap_rows = (rows_e + BM_EXP - 1) // BM_EXP * BM_EXP - rows_e
    gaps = jnp.concatenate([region + rows_e, gap_rows, (blocks_used * BM_EXP)[None],
                            jnp.zeros((TAB_W - 2 * N_EXPERTS - 1,), I32)]).astype(I32)
    return tab[:, None, :], gaps[None, :], meta


def _work_items(counts, M):
    nblk = M // BM_EXP
    n_items = (counts + BM_EXP - 1) // BM_EXP
    item_end = jnp.cumsum(n_items)
    item_start = item_end - n_items
    total = item_end[-1]
    w = jnp.arange(nblk, dtype=I32)
    w_eff = jnp.minimum(w, total - 1)
    owner = (item_end[None, :] <= w_eff[:, None]).sum(axis=1).astype(I32)
    sel = owner[:, None] == jnp.arange(N_EXPERTS, dtype=I32)[None, :]
    first_of_owner = jnp.sum(jnp.where(sel, item_start[None, :], 0), axis=1)
    used = w < total
    new = used & (w == first_of_owner)
    return (owner, used.astype(I32), new.astype(I32)), item_start * BM_EXP, total


def _block_diag(blocks):
    G, a, b = blocks.shape
    eye = jnp.eye(G, dtype=blocks.dtype)
    return jnp.einsum("gab,gh->gahb", blocks, eye).reshape(G * a, G * b)


def kernel(x, w_in, b_forget, w_pool, pool_scale, ssm_lambda_re, ssm_lambda_im, ssm_log_dt, ssm_b_re, ssm_b_im, ssm_c_re, ssm_c_im, ssm_d, w_glu, b_glu, w_branch_a, w_branch_b, w_branch_c, w_out, ln1_g, ln1_b, w_router, b_router, w_up, b_up, w_down, b_down, ln2_g, ln2_b):
    B, S, D = x.shape
    assert D == D_MODEL and S % T_ATT == 0 and S % TM_PROJ == 0
    N = B * S
    m_pad = (N // TM_ROW) * ROWS_TILE + N_EXPERTS * BM_EXP
    assert m_pad % BM_EXP == 0
    perm = _pair_perm()
    x2 = x.reshape(N, D)
    scale = ATT_HEAD_DIM ** -0.5
    for l in range(DEPTH):
        wl = w_in[l]
        wq = wl[:, :ATT_WIDTH] * scale
        wqv = jnp.concatenate([wq, wl[:, 2 * ATT_WIDTH:3 * ATT_WIDTH]], axis=1).astype(BF16)
        wkt = wl[:, ATT_WIDTH:2 * ATT_WIDTH].T.astype(BF16)
        c0 = 3 * ATT_WIDTH
        wf = wl[:, c0:c0 + ATT_HEADS]
        wsmall = jnp.concatenate(
            [wl[:, c0 + ATT_HEADS:c0 + ATT_HEADS + POOL_WIDTH + SSM_WIDTH], wf,
             jnp.zeros((D, F_PAD - ATT_HEADS), F32)], axis=1).astype(BF16)
        wgates = wl[:, c0 + ATT_HEADS + POOL_WIDTH + SSM_WIDTH:].astype(BF16)

        q, kt, v, u_pool, u_ssm_tm, f_pad = _proj(x2, wqv, wkt, wsmall, B, S)

        f_rows = f_pad[:, :ATT_HEADS].reshape(B, S, ATT_HEADS).transpose(0, 2, 1).reshape(B * ATT_HEADS, S)
        b_rows = jnp.tile(b_forget[l], B).reshape(B * ATT_HEADS, 1)
        c = _fcum(f_rows, b_rows).reshape(B, ATT_HEADS, 1, S)
        y_a = _attention(q, kt, v, c, B, S).reshape(N, ATT_WIDTH)

        w_pool_bd = _block_diag(w_pool[l]).astype(BF16)
        y_b = _pool(u_pool.reshape(B, S, POOL_WIDTH), w_pool_bd, pool_scale[l].reshape(1, POOL_WIDTH)).reshape(N, POOL_WIDTH)

        ar, ai, bbrT, bbiT = _ssm_prep(ssm_lambda_re[l], ssm_lambda_im[l], ssm_log_dt[l], ssm_b_re[l], ssm_b_im[l])
        bblk = jnp.concatenate([_block_diag(bbrT), _block_diag(bbiT)], axis=1).astype(BF16)
        cblk = jnp.concatenate([_block_diag(ssm_c_re[l].transpose(0, 2, 1)),
                                -_block_diag(ssm_c_im[l].transpose(0, 2, 1))], axis=0).astype(BF16)
        ar_b = jnp.broadcast_to(ar.reshape(1, SSM_STATES), (B, SSM_STATES))
        ai_b = jnp.broadcast_to(ai.reshape(1, SSM_STATES), (B, SSM_STATES))
        y_c_tm = _ssm(u_ssm_tm.reshape(S * B, SSM_WIDTH), bblk, ar_b, ai_b, cblk,
                      ssm_d[l].reshape(1, SSM_WIDTH), w_glu[l].astype(BF16), b_glu[l].reshape(1, SSM_WIDTH), B, S)

        x1 = _merge(x2, y_a, y_b, y_c_tm.reshape(S, B * SSM_WIDTH), wgates,
                    w_branch_a[l].astype(BF16), w_branch_b[l].astype(BF16), w_branch_c[l].astype(BF16),
                    w_out[l].astype(BF16), ln1_g[l].reshape(1, D), ln1_b[l].reshape(1, D), B, S)

        wr = jnp.concatenate([w_router[l], jnp.zeros((D, LANES - N_EXPERTS), F32)], axis=1)
        br = jnp.concatenate([b_router[l], jnp.full((LANES - N_EXPERTS,), _NEG_BIG, F32)]).reshape(1, LANES)
        gate, slot, slott, cnt = _route(x1, wr, br)
        tab, gaps, meta = _moe_tables(cnt, m_pad)
        xs = _dispatch(tab, gaps, x1, slott, m_pad)
        bu = b_up[l].reshape(N_EXPERTS, D_FF // LANES, LANES, 2).transpose(0, 1, 3, 2).reshape(N_EXPERTS, 1, 2 * D_FF)
        yb = _experts(meta, xs, w_up, bu, w_down, b_down[l][:, None, :], l, perm)
        x2 = _combine(tab, x1, gate, slot, ln2_g[l].reshape(1, D), ln2_b[l].reshape(1, D), yb)
    return x2.reshape(B, S, D)
```

```python
import math

import jax
import jax.numpy as jnp
import numpy as np
from jax import lax
from jax.experimental import pallas as pl
from jax.experimental.pallas import tpu as pltpu

F32 = jnp.float32
BF16 = jnp.bfloat16
I32 = jnp.int32
U32 = jnp.uint32

D_MODEL = 1024
DEPTH = 4
ATT_HEADS = 8
ATT_HEAD_DIM = 64
ATT_WIDTH = ATT_HEADS * ATT_HEAD_DIM
POOL_WINDOWS = (2, 4, 8, 16)
POOL_GROUPS = 4
POOL_WIDTH = 256
POOL_GROUP_DIM = 64
MAX_WINDOW = max(POOL_WINDOWS)
SSM_WIDTH = 256
SSM_GROUP_DIM = 16
SSM_GROUPS = 16
SSM_STATE = 64
SSM_STATES = SSM_GROUPS * SSM_STATE
N_BRANCH = 3
N_EXPERTS = 32
TOP_K = 4
D_FF = D_MODEL
SWIGLU_LIMIT = 7.0
SWIGLU_ALPHA = 1.702
LN_EPS = 1e-5
DEEPNORM_ALPHA = (2.0 * DEPTH) ** 0.25
GELU_C = math.sqrt(2.0 / math.pi)

LANES = 128
HALF_D = D_MODEL // 2
F_PAD = LANES
V_PAD = LANES
SMALL_W = POOL_WIDTH + SSM_WIDTH + F_PAD
TM_PROJ = 512
T_ATT = 1024
TM_POOL = 256
TT_SSM = 128
SUB_SSM = 128
TM_MERGE = 512
SUBLANES = 8
TM_ROW = 256
ROWS_TILE = TM_ROW * TOP_K + N_EXPERTS * SUBLANES
TAB_W = LANES
assert TAB_W >= 3 * N_EXPERTS + 2
BM_EXP = 512
VMEM_LIMIT = 52 * 1024 * 1024

_NEG_BIG = -1e30


def _cparams(sem, **kw):
    return pltpu.CompilerParams(dimension_semantics=sem, vmem_limit_bytes=VMEM_LIMIT, **kw)


def _sigmoid(x):
    return 1.0 / (1.0 + jnp.exp(-x))


def _layer_norm(z, g, b):
    mu = jnp.mean(z, axis=-1, keepdims=True)
    zc = z - mu
    var = jnp.mean(zc * zc, axis=-1, keepdims=True)
    return zc * lax.rsqrt(var + LN_EPS) * g + b


def _pack_rows(y):
    u = pltpu.bitcast(y.astype(BF16).astype(F32), U32)
    return u[:, :HALF_D] | (u[:, HALF_D:] >> 16)


def _unpack_rows(p):
    hi = pltpu.bitcast(p & jnp.uint32(0xFFFF0000), F32)
    lo = pltpu.bitcast(p << 16, F32)
    return hi, lo


def _proj_kernel(x_ref, wqv_ref, wkt_ref, ws_ref, q_ref, kt_ref, v_ref, up_ref, us_ref, f_ref):
    xb = x_ref[...].astype(BF16)
    h = jnp.dot(xb, wqv_ref[:, :ATT_WIDTH], preferred_element_type=F32)
    for hh in range(ATT_HEADS):
        q_ref[0, hh] = h[:, hh * ATT_HEAD_DIM:(hh + 1) * ATT_HEAD_DIM].astype(BF16)
    ht = lax.dot_general(wkt_ref[...], xb, (((1,), (1,)), ((), ())), preferred_element_type=F32)
    for hh in range(ATT_HEADS):
        kt_ref[0, hh] = ht[hh * ATT_HEAD_DIM:(hh + 1) * ATT_HEAD_DIM, :].astype(BF16)
    h = jnp.dot(xb, wqv_ref[:, ATT_WIDTH:], preferred_element_type=F32)
    lane = lax.broadcasted_iota(I32, (h.shape[0], V_PAD), 1)
    tail = jnp.where(lane == ATT_HEAD_DIM, 1.0, 0.0)
    for pair in range(ATT_HEADS // 2):
        slab = h[:, pair * V_PAD:(pair + 1) * V_PAD]
        v_ref[0, 2 * pair] = jnp.where(lane < ATT_HEAD_DIM, slab, tail).astype(BF16)
        v_ref[0, 2 * pair + 1] = jnp.where(lane < ATT_HEAD_DIM, pltpu.roll(slab, ATT_HEAD_DIM, 1), tail).astype(BF16)
    hs = jnp.dot(xb, ws_ref[...], preferred_element_type=F32)
    up_ref[...] = hs[:, :POOL_WIDTH]
    us_ref[...] = hs[:, POOL_WIDTH:POOL_WIDTH + SSM_WIDTH]
    f_ref[...] = hs[:, POOL_WIDTH + SSM_WIDTH:]


def _proj(x2, wqv, wkt, wsmall, B, S):
    N = B * S
    nS = S // TM_PROJ
    hm = jax.ShapeDtypeStruct((B, ATT_HEADS, S, ATT_HEAD_DIM), BF16)
    hm_spec = pl.BlockSpec((1, ATT_HEADS, TM_PROJ, ATT_HEAD_DIM), lambda b, s: (b, 0, s, 0))
    ht = jax.ShapeDtypeStruct((B, ATT_HEADS, ATT_HEAD_DIM, S), BF16)
    ht_spec = pl.BlockSpec((1, ATT_HEADS, ATT_HEAD_DIM, TM_PROJ), lambda b, s: (b, 0, 0, s))
    hv = jax.ShapeDtypeStruct((B, ATT_HEADS, S, V_PAD), BF16)
    hv_spec = pl.BlockSpec((1, ATT_HEADS, TM_PROJ, V_PAD), lambda b, s: (b, 0, s, 0))
    return pl.pallas_call(
        _proj_kernel,
        grid=(B, nS),
        in_specs=[
            pl.BlockSpec((TM_PROJ, D_MODEL), lambda b, s: (b * nS + s, 0)),
            pl.BlockSpec((D_MODEL, 2 * ATT_WIDTH), lambda b, s: (0, 0)),
            pl.BlockSpec((ATT_WIDTH, D_MODEL), lambda b, s: (0, 0)),
            pl.BlockSpec((D_MODEL, SMALL_W), lambda b, s: (0, 0)),
        ],
        out_specs=[
            hm_spec, ht_spec, hv_spec,
            pl.BlockSpec((TM_PROJ, POOL_WIDTH), lambda b, s: (b * nS + s, 0)),
            pl.BlockSpec((TM_PROJ, SSM_WIDTH), lambda b, s: (s, b)),
            pl.BlockSpec((TM_PROJ, F_PAD), lambda b, s: (b * nS + s, 0)),
        ],
        out_shape=[
            hm, ht, hv,
            jax.ShapeDtypeStruct((N, POOL_WIDTH), F32),
            jax.ShapeDtypeStruct((S, B * SSM_WIDTH), F32),
            jax.ShapeDtypeStruct((N, F_PAD), F32),
        ],
        compiler_params=_cparams(("parallel", "parallel")),
        name="in_proj",
    )(x2, wqv, wkt, wsmall)


def _fcum_kernel(f_ref, b_ref, c_ref):
    rows, S = f_ref.shape
    lane = lax.broadcasted_iota(I32, (rows, LANES), 1)
    carry = jnp.zeros((rows, 1), F32)
    for ch in range(S // LANES):
        z = f_ref[:, ch * LANES:(ch + 1) * LANES] + b_ref[...]
        lf = jnp.minimum(z, 0.0) - jnp.log1p(jnp.exp(-jnp.abs(z)))
        sh = 1
        while sh < LANES:
            lf = lf + jnp.where(lane >= sh, pltpu.roll(lf, sh, 1), 0.0)
            sh *= 2
        lf = lf + carry
        c_ref[:, ch * LANES:(ch + 1) * LANES] = lf
        carry = lf[:, LANES - 1:LANES]


def _fcum(f_rows, b_rows):
    rows, S = f_rows.shape
    return pl.pallas_call(
        _fcum_kernel,
        out_shape=jax.ShapeDtypeStruct((rows, S), F32),
        compiler_params=pltpu.CompilerParams(vmem_limit_bytes=VMEM_LIMIT),
        name="forget_cumsum",
    )(f_rows, b_rows)


def _attn_kernel(q_ref, kt_ref, v_ref, c_ref, o_ref):
    T = T_ATT
    H = T // 2
    qi = pl.program_id(2)
    qs = [q_ref[0, hh] for hh in range(2)]

    def update(carry, q_pair, key_start, n_keys, diag):
        out = []
        for hh in range(2):
            m, acc = carry[2 * hh], carry[2 * hh + 1]
            kt = kt_ref[0, hh, :, pl.ds(key_start, n_keys)]
            v = v_ref[0, hh, pl.ds(key_start, n_keys), :]
            s = jnp.dot(q_pair[hh], kt, preferred_element_type=F32)
            s = s - c_ref[0, hh, :, pl.ds(key_start, n_keys)]
            if diag is not None:
                delta = lax.broadcasted_iota(I32, s.shape, 1) - lax.broadcasted_iota(I32, s.shape, 0)
                s = jnp.where(delta <= diag, s, -jnp.inf)
            m_new = jnp.maximum(m, jnp.max(s, axis=-1, keepdims=True))
            p = jnp.exp(s - m_new)
            acc = jnp.exp(m - m_new) * acc + jnp.dot(p.astype(BF16), v, preferred_element_type=F32)
            out += [m_new, acc]
        return tuple(out)

    m0 = jnp.full((T, 1), -jnp.inf, F32)
    acc0 = jnp.zeros((T, V_PAD), F32)
    carry = lax.fori_loop(
        0, qi, lambda j, c: update(c, qs, pl.multiple_of(j * T, T), T, None), (m0, acc0, m0, acc0))
    diag_start = pl.multiple_of(qi * T, T)
    halves = []
    for half in range(2):
        rows = slice(half * H, (half + 1) * H)
        halves.append(update(tuple(x[rows] for x in carry), [q[rows] for q in qs],
                             diag_start, (half + 1) * H, half * H))
    for hh in range(2):
        acc = jnp.concatenate([halves[0][2 * hh + 1], halves[1][2 * hh + 1]], axis=0)
        out = acc[:, :ATT_HEAD_DIM] / acc[:, ATT_HEAD_DIM:ATT_HEAD_DIM + 1]
        o_ref[0, :, hh * ATT_HEAD_DIM:(hh + 1) * ATT_HEAD_DIM] = out.astype(BF16)


def _attention(q, k, v, c, B, S):
    nq = S // T_ATT
    qspec = pl.BlockSpec((1, 2, T_ATT, ATT_HEAD_DIM), lambda b, hp, i: (b, hp, i, 0))
    kspec = pl.BlockSpec((1, 2, ATT_HEAD_DIM, S), lambda b, hp, i: (b, hp, 0, 0))
    vspec = pl.BlockSpec((1, 2, S, V_PAD), lambda b, hp, i: (b, hp, 0, 0))
    return pl.pallas_call(
        _attn_kernel,
        grid=(B, ATT_HEADS // 2, nq),
        in_specs=[qspec, kspec, vspec,
                  pl.BlockSpec((1, 2, 1, S), lambda b, hp, i: (b, hp, 0, 0))],
        out_specs=pl.BlockSpec((1, T_ATT, 2 * ATT_HEAD_DIM), lambda b, hp, i: (b, i, hp)),
        out_shape=jax.ShapeDtypeStruct((B, S, ATT_WIDTH), BF16),
        compiler_params=_cparams(("parallel", "parallel", "arbitrary")),
        name="fox_attention",
    )(q, k, v, c)


def _pool_kernel(u_ref, w_ref, sc_ref, o_ref, pad_ref):
    S = u_ref.shape[1]
    R = TM_POOL
    pad_ref[0:MAX_WINDOW, :] = jnp.zeros((MAX_WINDOW, POOL_WIDTH), F32)
    pad_ref[MAX_WINDOW:, :] = u_ref[0]
    lane = lax.broadcasted_iota(I32, (R, POOL_WIDTH), 1)
    trow = lax.broadcasted_iota(I32, (R, POOL_WIDTH), 0)
    grp = lane // POOL_GROUP_DIM
    win = jnp.where(grp == 0, POOL_WINDOWS[0],
                    jnp.where(grp == 1, POOL_WINDOWS[1], jnp.where(grp == 2, POOL_WINDOWS[2], POOL_WINDOWS[3])))
    for i in range(S // R):
        base = MAX_WINDOW + i * R
        u0 = pad_ref[base:base + R, :]
        acc = u0
        sums = {}
        for kk in range(1, MAX_WINDOW):
            acc = acc + pad_ref[base - kk:base - kk + R, :]
            if kk + 1 in POOL_WINDOWS:
                sums[kk + 1] = acc
        total = jnp.where(grp == 0, sums[POOL_WINDOWS[0]],
                          jnp.where(grp == 1, sums[POOL_WINDOWS[1]],
                                    jnp.where(grp == 2, sums[POOL_WINDOWS[2]], sums[POOL_WINDOWS[3]])))
        cnt = jnp.minimum(trow + (i * R + 1), win).astype(F32)
        mixed = total / cnt - u0
        y = jnp.dot(mixed.astype(BF16), w_ref[...], preferred_element_type=F32) * sc_ref[...]
        o_ref[0, i * R:(i + 1) * R, :] = y.astype(BF16)


def _pool(u3, w_bd, scale):
    B, S, _ = u3.shape
    return pl.pallas_call(
        _pool_kernel,
        grid=(B,),
        in_specs=[pl.BlockSpec((1, S, POOL_WIDTH), lambda b: (b, 0, 0)),
                  pl.BlockSpec((POOL_WIDTH, POOL_WIDTH), lambda b: (0, 0)),
                  pl.BlockSpec((1, POOL_WIDTH), lambda b: (0, 0))],
        out_specs=pl.BlockSpec((1, S, POOL_WIDTH), lambda b: (b, 0, 0)),
        out_shape=jax.ShapeDtypeStruct((B, S, POOL_WIDTH), BF16),
        scratch_shapes=[pltpu.VMEM((S + MAX_WINDOW, POOL_WIDTH), F32)],
        compiler_params=_cparams(("parallel",)),
        name="multiscale_pool",
    )(u3, w_bd, scale)


def _ssm_prep_kernel(lr_ref, li_ref, ldt_ref, brT_ref, biT_ref, ar_ref, ai_ref, bbr_ref, bbi_ref):
    lr = lr_ref[...]
    li = li_ref[...]
    dt = jnp.exp(ldt_ref[...])
    mag = jnp.exp(lr * dt)
    ar = mag * jnp.cos(li * dt)
    ai = mag * jnp.sin(li * dt)
    den = lr * lr + li * li
    nr = ar - 1.0
    zr = (nr * lr + ai * li) / den
    zi = (ai * lr - nr * li) / den
    ar_ref[...] = ar
    ai_ref[...] = ai
    br = brT_ref[...]
    bi = biT_ref[...]
    bbr_ref[...] = zr[:, None, :] * br - zi[:, None, :] * bi
    bbi_ref[...] = zr[:, None, :] * bi + zi[:, None, :] * br


def _ssm_prep(lr, li, log_dt, b_re, b_im):
    G, P, H = b_re.shape
    gp = jax.ShapeDtypeStruct((G, P), F32)
    ghp = jax.ShapeDtypeStruct((G, H, P), F32)
    return pl.pallas_call(
        _ssm_prep_kernel,
        out_shape=[gp, gp, ghp, ghp],
        name="s5_discretise",
    )(lr, li, log_dt.reshape(G, 1), b_re.transpose(0, 2, 1), b_im.transpose(0, 2, 1))


def _gelu_tanh(y):
    return 0.5 * y * (1.0 + jnp.tanh(GELU_C * (y + 0.044715 * (y * y * y))))


def _ssm_kernel(u_ref, bb_ref, ar_ref, ai_ref, cc_ref, d_ref, wg_ref, bg_ref, o_ref, x_scr, hr_scr, hi_scr):
    nb = hr_scr.shape[0]
    tt = u_ref.shape[0] // nb

    @pl.when(pl.program_id(0) == 0)
    def _():
        hr_scr[...] = jnp.zeros_like(hr_scr)
        hi_scr[...] = jnp.zeros_like(hi_scr)

    hr = hr_scr[...]
    hi = hi_scr[...]
    ar = ar_ref[...]
    ai = ai_ref[...]
    rows_sub = SUB_SSM * nb
    for sc in range(tt // SUB_SSM):
        rows = slice(sc * rows_sub, (sc + 1) * rows_sub)
        u = u_ref[rows, :]
        x = jnp.dot(u.astype(BF16), bb_ref[...], preferred_element_type=F32)
        for t in range(SUB_SSM):
            xr = x[t * nb:(t + 1) * nb, 0:SSM_STATES]
            xi = x[t * nb:(t + 1) * nb, SSM_STATES:2 * SSM_STATES]
            hr, hi = ar * hr - ai * hi + xr, ar * hi + ai * hr + xi
            r0 = sc * rows_sub + t * nb
            x_scr[r0:r0 + nb, 0:SSM_STATES] = hr
            x_scr[r0:r0 + nb, SSM_STATES:2 * SSM_STATES] = hi
        y = jnp.dot(x_scr[rows, :].astype(BF16), cc_ref[...], preferred_element_type=F32) + d_ref[...] * u
        y = _gelu_tanh(y)
        gl = jnp.dot(y.astype(BF16), wg_ref[...], preferred_element_type=F32) + bg_ref[...]
        o_ref[rows, :] = (y * _sigmoid(gl)).astype(BF16)
    hr_scr[...] = hr
    hi_scr[...] = hi


def _ssm(u_tm, bblk, ar_b, ai_b, cblk, dvec, wglu, bglu, B, S):
    rows = TT_SSM * B
    const = lambda shape: pl.BlockSpec(shape, lambda i: (0, 0))
    return pl.pallas_call(
        _ssm_kernel,
        grid=(S // TT_SSM,),
        in_specs=[pl.BlockSpec((rows, SSM_WIDTH), lambda i: (i, 0)),
                  const((SSM_WIDTH, 2 * SSM_STATES)),
                  const((B, SSM_STATES)), const((B, SSM_STATES)),
                  const((2 * SSM_STATES, SSM_WIDTH)),
                  const((1, SSM_WIDTH)), const((SSM_WIDTH, SSM_WIDTH)), const((1, SSM_WIDTH))],
        out_specs=pl.BlockSpec((rows, SSM_WIDTH), lambda i: (i, 0)),
        out_shape=jax.ShapeDtypeStruct((S * B, SSM_WIDTH), BF16),
        scratch_shapes=[pltpu.VMEM((rows, 2 * SSM_STATES), F32),
                        pltpu.VMEM((B, SSM_STATES), F32), pltpu.VMEM((B, SSM_STATES), F32)],
        compiler_params=_cparams(("arbitrary",)),
        name="s5_scan",
    )(u_tm, bblk, ar_b, ai_b, cblk, dvec, wglu, bglu)


def _merge_kernel(x_ref, ya_ref, yb_ref, yc_ref, wg_ref, wa_ref, wb_ref, wc_ref, wo_ref, g_ref, b_ref, o_ref):
    x = x_ref[...]
    xb = x.astype(BF16)
    merged = None
    for i, (y_ref, w_ref) in enumerate(((ya_ref, wa_ref), (yb_ref, wb_ref), (yc_ref, wc_ref))):
        gate = _sigmoid(jnp.dot(xb, wg_ref[:, i * D_MODEL:(i + 1) * D_MODEL], preferred_element_type=F32))
        term = gate * jnp.dot(y_ref[...], w_ref[...], preferred_element_type=F32)
        merged = term if merged is None else merged + term
    mix = jnp.dot(merged.astype(BF16), wo_ref[...], preferred_element_type=F32)
    o_ref[...] = _layer_norm(DEEPNORM_ALPHA * x + mix, g_ref[...], b_ref[...])


def _merge(x2, ya, yb, yc_tm, wg, wa, wb, wc, wo, g, b, B, S):
    N = B * S
    nS = S // TM_MERGE
    const = lambda shape: pl.BlockSpec(shape, lambda bb, s: (0, 0))
    row = lambda w: pl.BlockSpec((TM_MERGE, w), lambda bb, s: (bb * nS + s, 0))
    return pl.pallas_call(
        _merge_kernel,
        grid=(B, nS),
        in_specs=[row(D_MODEL), row(ATT_WIDTH), row(POOL_WIDTH),
                  pl.BlockSpec((TM_MERGE, SSM_WIDTH), lambda bb, s: (s, bb)),
                  const((D_MODEL, N_BRANCH * D_MODEL)), const((ATT_WIDTH, D_MODEL)),
                  const((POOL_WIDTH, D_MODEL)), const((SSM_WIDTH, D_MODEL)), const((D_MODEL, D_MODEL)),
                  const((1, D_MODEL)), const((1, D_MODEL))],
        out_specs=row(D_MODEL),
        out_shape=jax.ShapeDtypeStruct((N, D_MODEL), F32),
        compiler_params=_cparams(("parallel", "parallel")),
        name="merge_ln1",
    )(x2, ya, yb, yc_tm, wg, wa, wb, wc, wo, g, b)


def _route_kernel(x_ref, wr_ref, br_ref, gate_ref, slot_ref, slott_ref, cnt_ref):
    tm = x_ref.shape[0]
    x = x_ref[...]
    xh = x.astype(BF16)
    xl = (x - xh.astype(F32)).astype(BF16)
    w = wr_ref[...]
    wh = w.astype(BF16)
    wl = (w - wh.astype(F32)).astype(BF16)
    logits = (jnp.dot(xh, wh, preferred_element_type=F32) + jnp.dot(xl, wh, preferred_element_type=F32)
              + jnp.dot(xh, wl, preferred_element_type=F32)) + br_ref[...]
    work = logits.T[0:N_EXPERTS, :]
    sub = lax.broadcasted_iota(I32, (N_EXPERTS, tm), 0).astype(F32)
    vals, idxs = [], []
    multihot = jnp.zeros((N_EXPERTS, tm), F32)
    for _ in range(TOP_K):
        mx = jnp.max(work, axis=0, keepdims=True)
        ix = jnp.min(jnp.where(work == mx, sub, float(N_EXPERTS)), axis=0, keepdims=True)
        sel = sub == ix
        work = jnp.where(sel, -jnp.inf, work)
        multihot = multihot + jnp.where(sel, 1.0, 0.0)
        vals.append(mx)
        idxs.append(ix)
    exps = [jnp.exp(v - vals[0]) for v in vals]
    denom = exps[0] + exps[1] + exps[2] + exps[3]
    r = lax.broadcasted_iota(I32, (tm, tm), 0)
    c = lax.broadcasted_iota(I32, (tm, tm), 1)
    earlier = jnp.where(r < c, 1.0, 0.0).astype(BF16)
    before = jnp.dot(multihot.astype(BF16), earlier, preferred_element_type=F32)
    n = jnp.sum(multihot, axis=1, keepdims=True)
    n_pad = jnp.floor((n + float(SUBLANES - 1)) * (1.0 / SUBLANES)) * float(SUBLANES)
    n_pad_b = jnp.broadcast_to(n_pad, (N_EXPERTS, LANES))
    er = lax.broadcasted_iota(I32, (N_EXPERTS, N_EXPERTS), 0)
    ec = lax.broadcasted_iota(I32, (N_EXPERTS, N_EXPERTS), 1)
    below = jnp.where(ec < er, 1.0, 0.0).astype(BF16)
    run_start = jnp.dot(below, n_pad_b.astype(BF16), preferred_element_type=F32)[:, 0:1]
    where_in_tile = before + run_start
    slots = [jnp.sum(jnp.where(sub == idxs[kk], where_in_tile, 0.0), axis=0, keepdims=True) for kk in range(TOP_K)]
    gates = [exps[kk] / denom for kk in range(TOP_K)]
    sub8 = lax.broadcasted_iota(I32, (SUBLANES, tm), 0)
    slot8 = jnp.zeros((SUBLANES, tm), F32)
    gate8 = jnp.zeros((SUBLANES, tm), F32)
    for kk in range(TOP_K):
        slot8 = jnp.where(sub8 == kk, slots[kk], slot8)
        gate8 = jnp.where(sub8 == kk, gates[kk], gate8)
    slott_ref[0] = slot8.astype(I32)
    pad_rows = jnp.zeros((LANES - SUBLANES, tm), F32)
    slot_ref[...] = jnp.concatenate([slot8, pad_rows], axis=0).T.astype(I32)
    gate_ref[...] = jnp.concatenate([gate8, pad_rows], axis=0).T
    cnt_ref[0] = n_pad_b.astype(I32)


def _route(x1, wr, br):
    N = x1.shape[0]
    nt = N // TM_ROW
    row = pl.BlockSpec((TM_ROW, LANES), lambda i: (i, 0))
    return pl.pallas_call(
        _route_kernel,
        grid=(nt,),
        in_specs=[pl.BlockSpec((TM_ROW, D_MODEL), lambda i: (i, 0)),
                  pl.BlockSpec((D_MODEL, LANES), lambda i: (0, 0)),
                  pl.BlockSpec((1, LANES), lambda i: (0, 0))],
        out_specs=[row, row,
                   pl.BlockSpec((1, SUBLANES, TM_ROW), lambda i: (i, 0, 0)),
                   pl.BlockSpec((1, N_EXPERTS, LANES), lambda i: (i, 0, 0))],
        out_shape=[jax.ShapeDtypeStruct((N, LANES), F32), jax.ShapeDtypeStruct((N, LANES), I32),
                   jax.ShapeDtypeStruct((nt, SUBLANES, TM_ROW), I32),
                   jax.ShapeDtypeStruct((nt, N_EXPERTS, LANES), I32)],
        compiler_params=_cparams(("parallel",)),
        name="router_top4",
    )(x1, wr, br)


def _issue_runs(tab_ref, make_copy):
    def issue_one(e, _):
        rows = tab_ref[0, 0, 2 * N_EXPERTS + e]

        @pl.when(rows > 0)
        def _():
            make_copy(pl.multiple_of(tab_ref[0, 0, e], SUBLANES),
                      pl.multiple_of(tab_ref[0, 0, N_EXPERTS + e], SUBLANES),
                      pl.multiple_of(rows, SUBLANES)).start()
        return 0

    lax.fori_loop(0, N_EXPERTS, issue_one, 0)


def _wait_runs(tab_ref, make_copy):
    total = tab_ref[0, 0, 3 * N_EXPERTS]

    @pl.when(total > 0)
    def _():
        make_copy(0, 0, pl.multiple_of(total, SUBLANES)).wait()


def _dispatch_kernel(tab_ref, tab_prev_ref, gaps_ref, x_ref, slott_ref, xs_hbm, srt_scr, sem):
    tm = x_ref.shape[0]
    i = pl.program_id(0)
    last = pl.num_programs(0) - 1
    cur = lax.rem(i, 2)
    r = lax.broadcasted_iota(I32, (ROWS_TILE, tm), 0)
    hit = r == slott_ref[0, 0:1, :]
    for kk in range(1, TOP_K):
        hit = hit | (r == slott_ref[0, kk:kk + 1, :])
    sel = jnp.where(hit, 1.0, 0.0).astype(BF16)
    srt = jnp.dot(sel, x_ref[...].astype(BF16), preferred_element_type=F32)
    srt_scr[cur] = _pack_rows(srt)

    def out_copy(buf):
        return lambda local, remote, rows: pltpu.make_async_copy(
            srt_scr.at[buf, pl.ds(local, rows)], xs_hbm.at[pl.ds(remote, rows)], sem.at[buf])

    _issue_runs(tab_ref, out_copy(cur))

    @pl.when(i > 0)
    def _():
        _wait_runs(tab_prev_ref, out_copy(1 - cur))

    @pl.when(i == last)
    def _():
        _wait_runs(tab_ref, out_copy(cur))
        srt_scr[0, 0:BM_EXP, :] = jnp.zeros((BM_EXP, HALF_D), U32)
        tail = pl.multiple_of(gaps_ref[0, 2 * N_EXPERTS], BM_EXP)
        n_tail = (xs_hbm.shape[0] - tail) // BM_EXP

        def zero_copy(first_row, rows):
            return pltpu.make_async_copy(srt_scr.at[0, pl.ds(0, rows)], xs_hbm.at[pl.ds(first_row, rows)], sem.at[0])

        def gap(e, op):
            rows = gaps_ref[0, N_EXPERTS + e]

            @pl.when(rows > 0)
            def _():
                op(zero_copy(pl.multiple_of(gaps_ref[0, e], SUBLANES), pl.multiple_of(rows, SUBLANES)))

        def tail_block(b, op):
            op(zero_copy(pl.multiple_of(tail + b * BM_EXP, BM_EXP), BM_EXP))

        def for_each(count, body, op):
            def step(k, c):
                body(k, op)
                return c
            lax.fori_loop(0, count, step, 0)

        for op in (lambda cp: cp.start(), lambda cp: cp.wait()):
            for_each(N_EXPERTS, gap, op)
            for_each(n_tail, tail_block, op)


def _dispatch(tab, gaps, x1, slott, m_pad):
    N = x1.shape[0]
    return pl.pallas_call(
        _dispatch_kernel,
        grid=(N // TM_ROW,),
        in_specs=[pl.BlockSpec((1, 1, TAB_W), lambda i: (i, 0, 0), memory_space=pltpu.SMEM),
                  pl.BlockSpec((1, 1, TAB_W), lambda i: (jnp.maximum(i - 1, 0), 0, 0), memory_space=pltpu.SMEM),
                  pl.BlockSpec((1, TAB_W), lambda i: (0, 0), memory_space=pltpu.SMEM),
                  pl.BlockSpec((TM_ROW, D_MODEL), lambda i: (i, 0)),
                  pl.BlockSpec((1, SUBLANES, TM_ROW), lambda i: (i, 0, 0))],
        out_specs=pl.BlockSpec(memory_space=pl.ANY),
        out_shape=jax.ShapeDtypeStruct((m_pad, HALF_D), U32),
        scratch_shapes=[pltpu.VMEM((2, ROWS_TILE, HALF_D), U32), pltpu.SemaphoreType.DMA((2,))],
        compiler_params=_cparams(("arbitrary",), has_side_effects=True),
        name="moe_dispatch",
    )(tab, tab, gaps, x1, slott)


def _expert_kernel(exp_ref, used_ref, new_ref, xs_ref, wu_ref, bu_ref, wd_ref, bd_ref, perm_ref,
                   o_ref, wu_scr, wd_scr):
    w = pl.program_id(0)

    @pl.when(used_ref[w] == 0)
    def _():
        o_ref[...] = jnp.zeros_like(o_ref)

    @pl.when(new_ref[w] == 1)
    def _():
        for cc in range(2 * D_FF // (2 * LANES)):
            cols = slice(cc * 2 * LANES, (cc + 1) * 2 * LANES)
            wu_scr[:, cols] = jnp.dot(wu_ref[0, 0, :, cols].astype(BF16), perm_ref[...],
                                      preferred_element_type=F32).astype(BF16)
        wd_scr[...] = wd_ref[0, 0].astype(BF16)

    @pl.when(used_ref[w] == 1)
    def _():
        xh, xl = _unpack_rows(xs_ref[...])
        x = jnp.concatenate([xh.astype(BF16), xl.astype(BF16)], axis=1)
        hu = jnp.dot(x, wu_scr[...], preferred_element_type=F32) + bu_ref[0]
        chunks = []
        for cc in range(D_FF // LANES):
            g = jnp.minimum(hu[:, 2 * cc * LANES:(2 * cc + 1) * LANES], SWIGLU_LIMIT)
            l = jnp.clip(hu[:, (2 * cc + 1) * LANES:(2 * cc + 2) * LANES], -SWIGLU_LIMIT, SWIGLU_LIMIT)
            chunks.append(g * _sigmoid(SWIGLU_ALPHA * g) * (l + 1.0))
        act = jnp.concatenate(chunks, axis=1)
        y = jnp.dot(act.astype(BF16), wd_scr[...], preferred_element_type=F32) + bd_ref[0]
        o_ref[...] = _pack_rows(y)


def _pair_perm():
    p = np.zeros((2 * LANES, 2 * LANES), np.float32)
    j = np.arange(LANES)
    p[2 * j, j] = 1.0
    p[2 * j + 1, LANES + j] = 1.0
    return jnp.asarray(p, BF16)


def _experts(meta, xs, w_up, bu, w_down, bd, layer, perm):
    M = xs.shape[0]
    by_expert = lambda *shape: pl.BlockSpec((1,) + shape, lambda w, ex, used, new: (ex[w], 0, 0))
    stacked = lambda *shape: pl.BlockSpec((1, 1) + shape, lambda w, ex, used, new: (layer, ex[w], 0, 0))
    rows = pl.BlockSpec((BM_EXP, HALF_D), lambda w, ex, used, new: (w, 0))
    return pl.pallas_call(
        _expert_kernel,
        grid_spec=pltpu.PrefetchScalarGridSpec(
            num_scalar_prefetch=3,
            grid=(M // BM_EXP,),
            in_specs=[rows, stacked(D_MODEL, 2 * D_FF), by_expert(1, 2 * D_FF),
                      stacked(D_FF, D_MODEL), by_expert(1, D_MODEL),
                      pl.BlockSpec((2 * LANES, 2 * LANES), lambda w, ex, used, new: (0, 0))],
            out_specs=rows,
            scratch_shapes=[pltpu.VMEM((D_MODEL, 2 * D_FF), BF16), pltpu.VMEM((D_FF, D_MODEL), BF16)],
        ),
        out_shape=jax.ShapeDtypeStruct((M, HALF_D), U32),
        compiler_params=_cparams(("arbitrary",)),
        name="moe_experts",
    )(*meta, xs, w_up, bu, w_down, bd, perm)


def _combine_kernel(tab_ref, tab_next_ref, x_ref, gate_ref, slot_ref, g_ref, b_ref, yb_hbm, o_ref, buf, sem):
    tm = x_ref.shape[0]
    i = pl.program_id(0)
    cur = lax.rem(i, 2)

    def in_copy(b):
        return lambda local, remote, rows: pltpu.make_async_copy(
            yb_hbm.at[pl.ds(remote, rows)], buf.at[b, pl.ds(local, rows)], sem.at[b])

    @pl.when(i == 0)
    def _():
        buf[...] = jnp.zeros_like(buf)
        _issue_runs(tab_ref, in_copy(cur))

    @pl.when(i + 1 < pl.num_programs(0))
    def _():
        _issue_runs(tab_next_ref, in_copy(1 - cur))

    r = lax.broadcasted_iota(I32, (tm, ROWS_TILE), 1)
    slots = slot_ref[...]
    gates = gate_ref[...]
    pw = jnp.zeros((tm, ROWS_TILE), F32)
    for kk in range(TOP_K):
        pw = jnp.where(r == slots[:, kk:kk + 1], gates[:, kk:kk + 1], pw)
    pw = pw.astype(BF16)
    _wait_runs(tab_ref, in_copy(cur))
    yh, yl = _unpack_rows(buf[cur])
    ys = jnp.concatenate([yh.astype(BF16), yl.astype(BF16)], axis=1)
    y = jnp.dot(pw, ys, preferred_element_type=F32)
    o_ref[...] = _layer_norm(DEEPNORM_ALPHA * x_ref[...] + y, g_ref[...], b_ref[...])


def _combine(tab, x1, gate, slot, g, b, yb):
    N = x1.shape[0]
    nt = N // TM_ROW
    return pl.pallas_call(
        _combine_kernel,
        grid=(N // TM_ROW,),
        in_specs=[pl.BlockSpec((1, 1, TAB_W), lambda i: (i, 0, 0), memory_space=pltpu.SMEM),
                  pl.BlockSpec((1, 1, TAB_W), lambda i: (jnp.minimum(i + 1, nt - 1), 0, 0), memory_space=pltpu.SMEM),
                  pl.BlockSpec((TM_ROW, D_MODEL), lambda i: (i, 0)),
                  pl.BlockSpec((TM_ROW, LANES), lambda i: (i, 0)),
                  pl.BlockSpec((TM_ROW, LANES), lambda i: (i, 0)),
                  pl.BlockSpec((1, D_MODEL), lambda i: (0, 0)),
                  pl.BlockSpec((1, D_MODEL), lambda i: (0, 0)),
                  pl.BlockSpec(memory_space=pl.ANY)],
        out_specs=pl.BlockSpec((TM_ROW, D_MODEL), lambda i: (i, 0)),
        out_shape=jax.ShapeDtypeStruct((N, D_MODEL), F32),
        scratch_shapes=[pltpu.VMEM((2, ROWS_TILE, HALF_D), U32), pltpu.SemaphoreType.DMA((2,))],
        compiler_params=_cparams(("arbitrary",)),
        name="moe_combine_ln2",
    )(tab, tab, x1, gate, slot, g, b, yb)


def _moe_tables(cnt, m_pad):
    n_pad = cnt[:, :, 0]
    rows_e = n_pad.sum(axis=0)
    meta, region, blocks_used = _work_items(rows_e, m_pad)
    run_end = jnp.cumsum(n_pad, axis=1)
    run_start = run_end - n_pad
    hbm_start = region[None, :] + jnp.cumsum(n_pad, axis=0) - n_pad
    total = run_end[:, -1:]
    pad = jnp.zeros((n_pad.shape[0], TAB_W - 3 * N_EXPERTS - 1), I32)
    tab = jnp.concatenate([run_start, hbm_start, n_pad, total, pad], axis=1).astype(I32)
    gap_rows = (rows_e + BM_EXP - 1) // BM_EXP * BM_EXP - rows_e
    gaps = jnp.concatenate([region + rows_e, gap_rows, (blocks_used * BM_EXP)[None],
                            jnp.zeros((TAB_W - 2 * N_EXPERTS - 1,), I32)]).astype(I32)
    return tab[:, None, :], gaps[None, :], meta


def _work_items(counts, M):
    nblk = M // BM_EXP
    n_items = (counts + BM_EXP - 1) // BM_EXP
    item_end = jnp.cumsum(n_items)
    item_start = item_end - n_items
    total = item_end[-1]
    w = jnp.arange(nblk, dtype=I32)
    w_eff = jnp.minimum(w, total - 1)
    owner = (item_end[None, :] <= w_eff[:, None]).sum(axis=1).astype(I32)
    sel = owner[:, None] == jnp.arange(N_EXPERTS, dtype=I32)[None, :]
    first_of_owner = jnp.sum(jnp.where(sel, item_start[None, :], 0), axis=1)
    used = w < total
    new = used & (w == first_of_owner)
    return (owner, used.astype(I32), new.astype(I32)), item_start * BM_EXP, total


def _block_diag(blocks):
    G, a, b = blocks.shape
    eye = jnp.eye(G, dtype=blocks.dtype)
    return jnp.einsum("gab,gh->gahb", blocks, eye).reshape(G * a, G * b)


def kernel(x, w_in, b_forget, w_pool, pool_scale, ssm_lambda_re, ssm_lambda_im, ssm_log_dt, ssm_b_re, ssm_b_im, ssm_c_re, ssm_c_im, ssm_d, w_glu, b_glu, w_branch_a, w_branch_b, w_branch_c, w_out, ln1_g, ln1_b, w_router, b_router, w_up, b_up, w_down, b_down, ln2_g, ln2_b):
    B, S, D = x.shape
    assert D == D_MODEL and S % T_ATT == 0 and S % TM_PROJ == 0
    N = B * S
    m_pad = (N // TM_ROW) * ROWS_TILE + N_EXPERTS * BM_EXP
    assert m_pad % BM_EXP == 0
    perm = _pair_perm()
    x2 = x.reshape(N, D)
    scale = ATT_HEAD_DIM ** -0.5
    for l in range(DEPTH):
        wl = w_in[l]
        wq = wl[:, :ATT_WIDTH] * scale
        wqv = jnp.concatenate([wq, wl[:, 2 * ATT_WIDTH:3 * ATT_WIDTH]], axis=1).astype(BF16)
        wkt = wl[:, ATT_WIDTH:2 * ATT_WIDTH].T.astype(BF16)
        c0 = 3 * ATT_WIDTH
        wf = wl[:, c0:c0 + ATT_HEADS]
        wsmall = jnp.concatenate(
            [wl[:, c0 + ATT_HEADS:c0 + ATT_HEADS + POOL_WIDTH + SSM_WIDTH], wf,
             jnp.zeros((D, F_PAD - ATT_HEADS), F32)], axis=1).astype(BF16)
        wgates = wl[:, c0 + ATT_HEADS + POOL_WIDTH + SSM_WIDTH:].astype(BF16)

        q, kt, v, u_pool, u_ssm_tm, f_pad = _proj(x2, wqv, wkt, wsmall, B, S)

        f_rows = f_pad[:, :ATT_HEADS].reshape(B, S, ATT_HEADS).transpose(0, 2, 1).reshape(B * ATT_HEADS, S)
        b_rows = jnp.tile(b_forget[l], B).reshape(B * ATT_HEADS, 1)
        c = _fcum(f_rows, b_rows).reshape(B, ATT_HEADS, 1, S)
        y_a = _attention(q, kt, v, c, B, S).reshape(N, ATT_WIDTH)

        w_pool_bd = _block_diag(w_pool[l]).astype(BF16)
        y_b = _pool(u_pool.reshape(B, S, POOL_WIDTH), w_pool_bd, pool_scale[l].reshape(1, POOL_WIDTH)).reshape(N, POOL_WIDTH)

        ar, ai, bbrT, bbiT = _ssm_prep(ssm_lambda_re[l], ssm_lambda_im[l], ssm_log_dt[l], ssm_b_re[l], ssm_b_im[l])
        bblk = jnp.concatenate([_block_diag(bbrT), _block_diag(bbiT)], axis=1).astype(BF16)
        cblk = jnp.concatenate([_block_diag(ssm_c_re[l].transpose(0, 2, 1)),
                                -_block_diag(ssm_c_im[l].transpose(0, 2, 1))], axis=0).astype(BF16)
        ar_b = jnp.broadcast_to(ar.reshape(1, SSM_STATES), (B, SSM_STATES))
        ai_b = jnp.broadcast_to(ai.reshape(1, SSM_STATES), (B, SSM_STATES))
        y_c_tm = _ssm(u_ssm_tm.reshape(S * B, SSM_WIDTH), bblk, ar_b, ai_b, cblk,
                      ssm_d[l].reshape(1, SSM_WIDTH), w_glu[l].astype(BF16), b_glu[l].reshape(1, SSM_WIDTH), B, S)

        x1 = _merge(x2, y_a, y_b, y_c_tm.reshape(S, B * SSM_WIDTH), wgates,
                    w_branch_a[l].astype(BF16), w_branch_b[l].astype(BF16), w_branch_c[l].astype(BF16),
                    w_out[l].astype(BF16), ln1_g[l].reshape(1, D), ln1_b[l].reshape(1, D), B, S)

        wr = jnp.concatenate([w_router[l], jnp.zeros((D, LANES - N_EXPERTS), F32)], axis=1)
        br = jnp.concatenate([b_router[l], jnp.full((LANES - N_EXPERTS,), _NEG_BIG, F32)]).reshape(1, LANES)
        gate, slot, slott, cnt = _route(x1, wr, br)
        tab, gaps, meta = _moe_tables(cnt, m_pad)
        xs = _dispatch(tab, gaps, x1, slott, m_pad)
        bu = b_up[l].reshape(N_EXPERTS, D_FF // LANES, LANES, 2).transpose(0, 1, 3, 2).reshape(N_EXPERTS, 1, 2 * D_FF)
        yb = _experts(meta, xs, w_up, bu, w_down, b_down[l][:, None, :], l, perm)
        x2 = _combine(tab, x1, gate, slot, ln2_g[l].reshape(1, D), ln2_b[l].reshape(1, D), yb)
    return x2.reshape(B, S, D)
```
